```python
import math
import jax, jax.numpy as jnp
from jax import lax
import numpy as np

D_MODEL = 4096
BATCH = 2
SEQ = 4096
DEPTH = 2

HEAD_DIM = 128
N_ATT_HEADS = 16
N_REC_HEADS = 16
ATT_WIDTH = N_ATT_HEADS * HEAD_DIM
REC_KEY_DIM = 128
REC_VAL_DIM = 128
REC_WIDTH = N_REC_HEADS * REC_KEY_DIM
MIX_WIDTH = ATT_WIDTH + REC_WIDTH
KV_LORA_RANK = 512
N_IDX_HEADS = 16
IDX_DIM = 64
TOPK_MAX = 256
Q_BLOCK = 128
REC_CHUNK = 64
N_BUCKETS = 32
MAX_DISTANCE = 128
N_MEM = 256
N_CROSS_HEADS = 4
CROSS_WIDTH = N_CROSS_HEADS * HEAD_DIM
D_FF = 11008
ALPHA = (2 * DEPTH) ** 0.25
BETA = (8 * DEPTH) ** -0.25
LN_EPS = 1e-5
ATT_SCALE = HEAD_DIM ** -0.5
IDX_SCALE = IDX_DIM ** -0.5
IDX_HEAD_SCALE = N_IDX_HEADS ** -0.5
IN_SIZES = (ATT_WIDTH, KV_LORA_RANK, N_IDX_HEADS * IDX_DIM, IDX_DIM, N_IDX_HEADS,
            REC_WIDTH, REC_WIDTH, N_REC_HEADS * REC_VAL_DIM, N_REC_HEADS * REC_VAL_DIM)
IN_WIDTH = sum(IN_SIZES)

kernel_name = "hymba_dsa_hgrn2_macaron_deepnorm"


def layer_norm(x, g, b):
    xf = x.astype(jnp.float32)
    mu = jnp.mean(xf, -1, keepdims=True)
    var = jnp.mean(jnp.square(xf - mu), -1, keepdims=True)
    return ((xf - mu) * lax.rsqrt(var + LN_EPS) * g + b).astype(x.dtype)


def rms_norm(x, g):
    xf = x.astype(jnp.float32)
    return (xf * lax.rsqrt(jnp.mean(jnp.square(xf), -1, keepdims=True) + LN_EPS) * g).astype(x.dtype)


def swiglu(x, w_gate, w_up, w_down):
    return (jax.nn.silu(x @ w_gate) * (x @ w_up)) @ w_down


def t5_bucket(rel):
    rel = jnp.maximum(rel, 0)
    max_exact = N_BUCKETS // 2
    relf = jnp.maximum(rel, 1).astype(jnp.float32)
    large = max_exact + (jnp.log(relf / max_exact) / math.log(MAX_DISTANCE / max_exact)
                         * (N_BUCKETS - max_exact)).astype(jnp.int32)
    large = jnp.minimum(large, N_BUCKETS - 1)
    return jnp.where(rel < max_exact, rel, large)


def dsa_attention(q, c_kv, q_idx, k_idx, w_idx, w_uk, w_uv, rel_bias):
    b, s, h, dh = q.shape
    topk = min(TOPK_MAX, s // 4)
    nb = s // Q_BLOCK
    key_pos = jnp.arange(s)

    def to_blocks(a):
        return jnp.swapaxes(a.reshape((b, nb, Q_BLOCK) + a.shape[2:]), 0, 1)

    def one_block(args):
        blk, qb, qib, wb = args
        t = blk * Q_BLOCK + jnp.arange(Q_BLOCK)
        causal = key_pos[None, :] <= t[:, None]
        idx_logits = jnp.einsum('bthd,bsd->bths', qib, k_idx) * IDX_SCALE
        score = jnp.einsum('bths,bth->bts', jax.nn.relu(idx_logits), wb).astype(jnp.float32)
        score = jnp.where(causal[None], score, -jnp.inf)
        top_val, top_idx = lax.top_k(score, topk)
        valid = jnp.isfinite(top_val)
        c_sel = jax.vmap(lambda c, i: c[i])(c_kv, top_idx)
        q_abs = jnp.einsum('bthd,hrd->bthr', qb, w_uk)
        logits = jnp.einsum('bthr,btkr->bthk', q_abs, c_sel).astype(jnp.float32) * ATT_SCALE
        bias = rel_bias[t5_bucket(t[None, :, None] - top_idx)]
        logits = logits + jnp.transpose(bias, (0, 1, 3, 2)).astype(jnp.float32)
        logits = jnp.where(valid[:, :, None, :], logits, -jnp.inf)
        p = jax.nn.softmax(logits, axis=-1).astype(c_sel.dtype)
        o_lat = jnp.einsum('bthk,btkr->bthr', p, c_sel)
        return jnp.einsum('bthr,hrd->bthd', o_lat, w_uv)

    out = lax.map(one_block, (jnp.arange(nb), to_blocks(q), to_blocks(q_idx), to_blocks(w_idx)))
    return jnp.swapaxes(out, 0, 1).reshape(b, s, h * dh)


def hgrn2_mixer(q, f_pre, i_in, g, lb, norm_g):
    b, s, _ = q.shape
    h, dk, dv, c = N_REC_HEADS, REC_KEY_DIM, REC_VAL_DIM, REC_CHUNK
    nc = s // c
    qf = jax.nn.silu(q.astype(jnp.float32))
    forget = lb + (1.0 - lb) * jax.nn.sigmoid(f_pre.astype(jnp.float32))
    kf = 1.0 - forget
    log_f = jnp.log(forget)
    vf = i_in.astype(jnp.float32)

    def to_chunks(a, d):
        return jnp.transpose(a.reshape(b, nc, c, h, d), (1, 0, 3, 2, 4))

    mask = jnp.tril(jnp.ones((c, c), dtype=bool))

    def step(state, inp):
        qc, kc, vc, lfc = inp
        cum = jnp.cumsum(lfc, axis=2)
        diff = cum[:, :, :, None, :] - cum[:, :, None, :, :]
        decay = jnp.exp(jnp.where(mask[:, :, None], diff, -jnp.inf))
        a = jnp.einsum('bhtd,bhsd,bhtsd->bhts', qc, kc, decay)
        o = jnp.einsum('bhts,bhse->bhte', a, vc) + jnp.einsum('bhtd,bhde->bhte', qc * jnp.exp(cum), state)
        last = cum[:, :, -1:, :]
        new_state = (jnp.exp(last[:, :, 0, :])[..., None] * state
                     + jnp.einsum('bhsd,bhse->bhde', kc * jnp.exp(last - cum), vc))
        return new_state, o

    s0 = jnp.zeros((b, h, dk, dv), jnp.float32)
    _, o = lax.scan(step, s0, (to_chunks(qf, dk), to_chunks(kf, dk), to_chunks(vf, dv), to_chunks(log_f, dk)))
    o = jnp.transpose(o, (1, 0, 3, 2, 4)).reshape(b, s, h, dv)
    o = rms_norm(o, norm_g.reshape(h, dv)).reshape(b, s, h * dv)
    return (o * jax.nn.silu(g.astype(jnp.float32))).astype(q.dtype)


def cross_attention(x, mem, wq, wk, wv, wo):
    b, s, _ = x.shape
    m = mem.shape[1]
    q = (x @ wq).reshape(b, s, N_CROSS_HEADS, HEAD_DIM)
    k = (mem @ wk).reshape(b, m, N_CROSS_HEADS, HEAD_DIM)
    v = (mem @ wv).reshape(b, m, N_CROSS_HEADS, HEAD_DIM)
    logits = jnp.einsum('bshd,bmhd->bhsm', q, k).astype(jnp.float32) * ATT_SCALE
    p = jax.nn.softmax(logits, axis=-1).astype(v.dtype)
    o = jnp.einsum('bhsm,bmhd->bshd', p, v).reshape(b, s, CROSS_WIDTH)
    return o @ wo


def setup_inputs(seed: int = 0) -> dict:
    key = jax.random.key(seed)
    ks = jax.random.split(key, 24)
    f32 = jnp.float32
    nrm = lambda k, shape, scale: jax.random.normal(k, shape, f32) * scale
    return {
        "x": nrm(ks[0], (BATCH, SEQ, D_MODEL), 1.0),
        "mem": nrm(ks[1], (BATCH, N_MEM, D_MODEL), 1.0),
        "w_in": nrm(ks[2], (DEPTH, D_MODEL, IN_WIDTH), D_MODEL ** -0.5),
        "w_uk": nrm(ks[3], (DEPTH, N_ATT_HEADS, KV_LORA_RANK, HEAD_DIM), KV_LORA_RANK ** -0.5),
        "w_uv": nrm(ks[4], (DEPTH, N_ATT_HEADS, KV_LORA_RANK, HEAD_DIM), KV_LORA_RANK ** -0.5),
        "kv_norm_g": 1.0 + nrm(ks[5], (DEPTH, KV_LORA_RANK), 0.02),
        "idx_kn_g": 1.0 + nrm(ks[6], (DEPTH, IDX_DIM), 0.02),
        "idx_kn_b": nrm(ks[7], (DEPTH, IDX_DIM), 0.02),
        "rec_lb": nrm(ks[8], (DEPTH, REC_WIDTH), 0.1),
        "rec_norm_g": 1.0 + nrm(ks[9], (DEPTH, N_REC_HEADS * REC_VAL_DIM), 0.02),
        "w_out": nrm(ks[10], (DEPTH, MIX_WIDTH, D_MODEL), MIX_WIDTH ** -0.5 * BETA),
        "rel_bias": nrm(ks[11], (N_BUCKETS, N_ATT_HEADS), 0.2),
        "wq_c": nrm(ks[12], (DEPTH, D_MODEL, CROSS_WIDTH), D_MODEL ** -0.5),
        "wk_c": nrm(ks[13], (DEPTH, D_MODEL, CROSS_WIDTH), D_MODEL ** -0.5),
        "wv_c": nrm(ks[14], (DEPTH, D_MODEL, CROSS_WIDTH), D_MODEL ** -0.5),
        "wo_c": nrm(ks[15], (DEPTH, CROSS_WIDTH, D_MODEL), CROSS_WIDTH ** -0.5 * BETA),
        "ffn_gate": nrm(ks[16], (DEPTH, 2, D_MODEL, D_FF), D_MODEL ** -0.5),
        "ffn_up": nrm(ks[17], (DEPTH, 2, D_MODEL, D_FF), D_MODEL ** -0.5),
        "ffn_down": nrm(ks[18], (DEPTH, 2, D_FF, D_MODEL), D_FF ** -0.5 * BETA),
        "ln_g": 1.0 + nrm(ks[19], (DEPTH, 4, D_MODEL), 0.02),
        "ln_b": nrm(ks[20], (DEPTH, 4, D_MODEL), 0.02),
    }


def reference(x, mem, w_in, w_uk, w_uv, kv_norm_g, idx_kn_g, idx_kn_b, rec_lb, rec_norm_g,
              w_out, rel_bias, wq_c, wk_c, wv_c, wo_c, ffn_gate, ffn_up, ffn_down, ln_g, ln_b):
    b, s, _ = x.shape
    lb_p = jax.nn.softmax(rec_lb.astype(jnp.float32), axis=0)
    lower_bounds = jnp.cumsum(lb_p, axis=0) - lb_p[0:1]
    offsets = [int(o) for o in np.cumsum(IN_SIZES)[:-1]]
    for l in range(DEPTH):
        x = layer_norm(ALPHA * x + 0.5 * swiglu(x, ffn_gate[l, 0], ffn_up[l, 0], ffn_down[l, 0]),
                       ln_g[l, 0], ln_b[l, 0])
        (q_att, kv_lat, q_idx, k_idx, w_idx, q_rec, f_rec, i_rec, g_rec) = jnp.split(x @ w_in[l], offsets, axis=-1)
        c_kv = rms_norm(kv_lat, kv_norm_g[l])
        k_idx = layer_norm(k_idx, idx_kn_g[l], idx_kn_b[l])
        o_att = dsa_attention(q_att.reshape(b, s, N_ATT_HEADS, HEAD_DIM), c_kv,
                              q_idx.reshape(b, s, N_IDX_HEADS, IDX_DIM), k_idx,
                              w_idx * IDX_HEAD_SCALE, w_uk[l], w_uv[l], rel_bias)
        o_rec = hgrn2_mixer(q_rec, f_rec, i_rec, g_rec, lower_bounds[l], rec_norm_g[l])
        mix = jnp.concatenate([o_att, o_rec], axis=-1) @ w_out[l]
        x = layer_norm(ALPHA * x + mix, ln_g[l, 1], ln_b[l, 1])
        x = layer_norm(ALPHA * x + cross_attention(x, mem, wq_c[l], wk_c[l], wv_c[l], wo_c[l]),
                       ln_g[l, 2], ln_b[l, 2])
        x = layer_norm(ALPHA * x + 0.5 * swiglu(x, ffn_gate[l, 1], ffn_up[l, 1], ffn_down[l, 1]),
                       ln_g[l, 3], ln_b[l, 3])
    return x
```

```python
import functools
import math

import jax
import jax.numpy as jnp
import numpy as np
from jax import lax
from jax.experimental import pallas as pl
from jax.experimental.pallas import tpu as pltpu

F32 = jnp.float32
BF16 = jnp.bfloat16
I32 = jnp.int32

LANE = 128
VMEM_LIMIT_V7X = 60 * 1024 * 1024

HEAD_DIM = 128
N_ATT_HEADS = 16
N_REC_HEADS = 16
KV_LORA_RANK = 512
N_IDX_HEADS = 16
IDX_DIM = 64
TOPK_MAX = 256
Q_BLOCK = 128
N_BUCKETS = 32
MAX_DISTANCE = 128
N_CROSS_HEADS = 4
LN_EPS = 1e-5
ATT_SCALE = HEAD_DIM ** -0.5
IDX_SCALE = IDX_DIM ** -0.5
IDX_HEAD_SCALE = N_IDX_HEADS ** -0.5

ATT_WIDTH = N_ATT_HEADS * HEAD_DIM
IDX_WIDTH = N_IDX_HEADS * IDX_DIM
REC_WIDTH = N_REC_HEADS * HEAD_DIM
COL_QIDX = ATT_WIDTH
COL_KV = COL_QIDX + IDX_WIDTH
COL_SMALL = COL_KV + KV_LORA_RANK
SMALL_WIDTH = 256
COL_REC = COL_SMALL + SMALL_WIDTH
PROJ_WIDTH = COL_REC + 4 * REC_WIDTH

NEG_BIG = -1e30
INT_MIN = -2 ** 31

REC_CHUNK = 64
REC_TBLOCK = 512
KEY_GROUP = 256
SCORE_GROUP = 512
IDX_K = 256


def _cparams(sem):
    return pltpu.CompilerParams(dimension_semantics=sem, vmem_limit_bytes=VMEM_LIMIT_V7X)


def _single(shape, imap):
    return pl.BlockSpec(shape, imap, pipeline_mode=pl.Buffered(1))


def _layer_norm_rows(y, g, b):
    mu = jnp.mean(y, axis=-1, keepdims=True)
    d = y - mu
    var = jnp.mean(d * d, axis=-1, keepdims=True)
    return d * lax.rsqrt(var + LN_EPS) * g + b


def _dot(a, b):
    return jnp.dot(a, b, preferred_element_type=F32)


def _dot_nt(a, b):
    return lax.dot_general(a, b, (((1,), (1,)), ((), ())), preferred_element_type=F32)


def _dot_tn(a, b):
    return lax.dot_general(a, b, (((0,), (0,)), ((), ())), preferred_element_type=F32)


def _ffn_ln_kernel(x_ref, wg_ref, wu_ref, wd_ref, g_ref, b_ref, o_ref, ob_ref, xb_ref, *, alpha):
    f = pl.program_id(1)

    @pl.when(f == 0)
    def _():
        xb_ref[...] = x_ref[...].astype(BF16)
        o_ref[...] = jnp.zeros_like(o_ref)

    xb = xb_ref[...]
    h = _dot(xb, wg_ref[0])
    u = _dot(xb, wu_ref[0])
    a = (h * jax.nn.sigmoid(h) * u).astype(BF16)
    o_ref[...] += _dot(a, wd_ref[...])

    @pl.when(f == pl.num_programs(1) - 1)
    def _():
        y = alpha * x_ref[...] + 0.5 * o_ref[...]
        out = _layer_norm_rows(y, g_ref[...], b_ref[...])
        o_ref[...] = out
        ob_ref[...] = out.astype(BF16)


def ffn_ln(x, wg_blk, wu_blk, wd, g, b, *, alpha, tm):
    m, d = x.shape
    nf, _, tf = wg_blk.shape
    return pl.pallas_call(
        functools.partial(_ffn_ln_kernel, alpha=alpha),
        out_shape=(jax.ShapeDtypeStruct((m, d), F32), jax.ShapeDtypeStruct((m, d), BF16)),
        grid=(m // tm, nf),
        in_specs=[
            _single((tm, d), lambda i, f: (i, 0)),
            pl.BlockSpec((1, d, tf), lambda i, f: (f, 0, 0)),
            pl.BlockSpec((1, d, tf), lambda i, f: (f, 0, 0)),
            pl.BlockSpec((tf, d), lambda i, f: (f, 0)),
            pl.BlockSpec((1, d), lambda i, f: (0, 0)),
            pl.BlockSpec((1, d), lambda i, f: (0, 0)),
        ],
        out_specs=(_single((tm, d), lambda i, f: (i, 0)), _single((tm, d), lambda i, f: (i, 0))),
        scratch_shapes=[pltpu.VMEM((tm, d), BF16)],
        compiler_params=_cparams(("parallel", "arbitrary")),
        name="ffn_ln",
    )(x, wg_blk, wu_blk, wd, g, b)


def _matmul_kernel(x_ref, w_ref, o_ref):
    o_ref[...] = _dot(x_ref[...], w_ref[0]).astype(o_ref.dtype)


def matmul(x, w_blk, *, tm, out_dtype):
    m, k = x.shape
    nn, _, tn = w_blk.shape
    return pl.pallas_call(
        _matmul_kernel,
        out_shape=jax.ShapeDtypeStruct((m, nn * tn), out_dtype),
        grid=(m // tm, nn),
        in_specs=[
            pl.BlockSpec((tm, k), lambda i, j: (i, 0)),
            pl.BlockSpec((1, k, tn), lambda i, j: (j, 0, 0)),
        ],
        out_specs=pl.BlockSpec((tm, tn), lambda i, j: (i, j)),
        compiler_params=_cparams(("parallel", "arbitrary")),
        name="matmul",
    )(x, w_blk)


def _proj_ln_kernel(a_ref, w_ref, x_ref, g_ref, b_ref, o_ref, ob_ref, *, alpha):
    k = pl.program_id(1)

    @pl.when(k == 0)
    def _():
        o_ref[...] = jnp.zeros_like(o_ref)

    o_ref[...] += _dot(a_ref[...], w_ref[...])

    @pl.when(k == pl.num_programs(1) - 1)
    def _():
        y = alpha * x_ref[...] + o_ref[...]
        out = _layer_norm_rows(y, g_ref[...], b_ref[...])
        o_ref[...] = out
        ob_ref[...] = out.astype(BF16)


def proj_ln(a, w, x, g, b, *, alpha, tm, tk):
    m, kdim = a.shape
    d = w.shape[1]
    return pl.pallas_call(
        functools.partial(_proj_ln_kernel, alpha=alpha),
        out_shape=(jax.ShapeDtypeStruct((m, d), F32), jax.ShapeDtypeStruct((m, d), BF16)),
        grid=(m // tm, kdim // tk),
        in_specs=[
            pl.BlockSpec((tm, tk), lambda i, k: (i, k)),
            pl.BlockSpec((tk, d), lambda i, k: (k, 0)),
            _single((tm, d), lambda i, k: (i, 0)),
            pl.BlockSpec((1, d), lambda i, k: (0, 0)),
            pl.BlockSpec((1, d), lambda i, k: (0, 0)),
        ],
        out_specs=(_single((tm, d), lambda i, k: (i, 0)), _single((tm, d), lambda i, k: (i, 0))),
        compiler_params=_cparams(("parallel", "arbitrary")),
        name="proj_ln",
    )(a, w, x, g, b)


def _cross_ln_kernel(x_ref, wq_ref, kv_ref, wo_ref, g_ref, b_ref, o_ref, ob_ref, *, alpha, n_heads):
    x = x_ref[...]
    q = _dot(x.astype(BF16), wq_ref[...])
    kv = kv_ref[0]
    cw = n_heads * HEAD_DIM
    outs = []
    for h in range(n_heads):
        qh = q[:, h * HEAD_DIM:(h + 1) * HEAD_DIM].astype(BF16)
        kh = kv[:, h * HEAD_DIM:(h + 1) * HEAD_DIM]
        vh = kv[:, cw + h * HEAD_DIM:cw + (h + 1) * HEAD_DIM]
        logits = _dot_nt(qh, kh) * ATT_SCALE
        mx = jnp.max(logits, axis=-1, keepdims=True)
        e = jnp.exp(logits - mx)
        p = e / jnp.sum(e, axis=-1, keepdims=True)
        outs.append(_dot(p.astype(BF16), vh))
    o = jnp.concatenate(outs, axis=-1).astype(BF16)
    y = alpha * x + _dot(o, wo_ref[...])
    out = _layer_norm_rows(y, g_ref[...], b_ref[...])
    o_ref[...] = out
    ob_ref[...] = out.astype(BF16)


def cross_ln(x, wq, kv, wo, g, b, *, alpha, tm, rows_per_batch):
    m, d = x.shape
    cw = wq.shape[1]
    nmem = kv.shape[1]
    steps_per_batch = rows_per_batch // tm
    return pl.pallas_call(
        functools.partial(_cross_ln_kernel, alpha=alpha, n_heads=cw // HEAD_DIM),
        out_shape=(jax.ShapeDtypeStruct((m, d), F32), jax.ShapeDtypeStruct((m, d), BF16)),
        grid=(m // tm,),
        in_specs=[
            pl.BlockSpec((tm, d), lambda i: (i, 0)),
            _single((d, cw), lambda i: (0, 0)),
            pl.BlockSpec((1, nmem, 2 * cw), lambda i: (i // steps_per_batch, 0, 0)),
            _single((cw, d), lambda i: (0, 0)),
            pl.BlockSpec((1, d), lambda i: (0, 0)),
            pl.BlockSpec((1, d), lambda i: (0, 0)),
        ],
        out_specs=(pl.BlockSpec((tm, d), lambda i: (i, 0)), pl.BlockSpec((tm, d), lambda i: (i, 0))),
        compiler_params=_cparams(("parallel",)),
        name="cross_ln",
    )(x, wq, kv, wo, g, b)


def _split3(x, order):
    hi = x.astype(BF16).astype(F32)
    lo = x - hi
    parts = [hi if c == "h" else lo for c in order]
    parts.append(jnp.zeros((x.shape[0], IDX_K - 3 * x.shape[1]), F32))
    return jnp.concatenate(parts, axis=-1).astype(BF16)


def _dsa_prep_kernel(kv_ref, qi_ref, sm_ref, kvg_ref, kng_ref, knb_ref, ckv_ref, k3_ref, qi3_ref, wt_ref):
    kv = kv_ref[...]
    ms = jnp.mean(kv * kv, axis=-1, keepdims=True)
    ckv_ref[...] = (kv * lax.rsqrt(ms + LN_EPS) * kvg_ref[...]).astype(BF16)

    sm = sm_ref[...]
    kidx = _layer_norm_rows(sm[:, :IDX_DIM], kng_ref[...], knb_ref[...])
    k3_ref[...] = _split3(kidx, "hlh")

    wt = jnp.transpose(sm)
    wt_ref[...] = wt[IDX_DIM:IDX_DIM + N_IDX_HEADS, :] * (IDX_HEAD_SCALE * IDX_SCALE)

    qi = qi_ref[...]
    for h in range(N_IDX_HEADS):
        qi3_ref[h] = _split3(qi[:, h * IDX_DIM:(h + 1) * IDX_DIM], "hhl")


def dsa_prep(proj, kv_g, kn_g, kn_b):
    m = proj.shape[0]
    tq = Q_BLOCK
    nq = m // tq
    return pl.pallas_call(
        _dsa_prep_kernel,
        out_shape=(
            jax.ShapeDtypeStruct((m, KV_LORA_RANK), BF16),
            jax.ShapeDtypeStruct((m, IDX_K), BF16),
            jax.ShapeDtypeStruct((nq * N_IDX_HEADS, tq, IDX_K), BF16),
            jax.ShapeDtypeStruct((nq * N_IDX_HEADS, tq), F32),
        ),
        grid=(nq,),
        in_specs=[
            pl.BlockSpec((tq, KV_LORA_RANK), lambda i: (i, COL_KV // KV_LORA_RANK)),
            pl.BlockSpec((tq, IDX_WIDTH), lambda i: (i, COL_QIDX // IDX_WIDTH)),
            pl.BlockSpec((tq, LANE), lambda i: (i, COL_SMALL // LANE)),
            pl.BlockSpec((1, KV_LORA_RANK), lambda i: (0, 0)),
            pl.BlockSpec((1, IDX_DIM), lambda i: (0, 0)),
            pl.BlockSpec((1, IDX_DIM), lambda i: (0, 0)),
        ],
        out_specs=(
            pl.BlockSpec((tq, KV_LORA_RANK), lambda i: (i, 0)),
            pl.BlockSpec((tq, IDX_K), lambda i: (i, 0)),
            pl.BlockSpec((N_IDX_HEADS, tq, IDX_K), lambda i: (i, 0, 0)),
            pl.BlockSpec((N_IDX_HEADS, tq), lambda i: (i, 0)),
        ),
        compiler_params=_cparams(("parallel",)),
        name="dsa_prep",
    )(proj, proj, proj, kv_g, kn_g, kn_b)


def _bucket_thresholds():
    max_exact = N_BUCKETS // 2
    rel = np.arange(max_exact, 4 * MAX_DISTANCE, dtype=np.float32)
    large = max_exact + (np.log(rel / np.float32(max_exact)) / np.float32(math.log(MAX_DISTANCE / max_exact))
                         * np.float32(N_BUCKETS - max_exact)).astype(np.int32)
    large = np.minimum(large, N_BUCKETS - 1)
    return [int(rel[np.argmax(large >= bkt)]) for bkt in range(max_exact + 1, N_BUCKETS)]


def _bias_table_kernel(rb_ref, o_ref):
    di = pl.program_id(0)
    h = pl.program_id(1)
    max_exact = N_BUCKETS // 2
    tl = lax.broadcasted_iota(I32, (Q_BLOCK, KEY_GROUP), 0)
    sl = lax.broadcasted_iota(I32, (Q_BLOCK, KEY_GROUP), 1)
    rel = jnp.maximum(di * Q_BLOCK + tl - sl, 0)
    large = jnp.full(rel.shape, max_exact, I32)
    for th in _bucket_thresholds():
        large = large + (rel >= th).astype(I32)
    bucket = jnp.where(rel < max_exact, rel, large)
    far = rb_ref[N_BUCKETS - 1, h]
    val = jnp.zeros(rel.shape, F32)
    for bkt in range(N_BUCKETS - 1):
        val = jnp.where(bucket == bkt, rb_ref[bkt, h] - far, val)
    o_ref[0, 0] = val


def bias_tables(rel_bias):
    nh = rel_bias.shape[1]
    return pl.pallas_call(
        _bias_table_kernel,
        out_shape=jax.ShapeDtypeStruct((3, nh, Q_BLOCK, KEY_GROUP), F32),
        grid=(3, nh),
        in_specs=[pl.BlockSpec(memory_space=pltpu.SMEM)],
        out_specs=pl.BlockSpec((1, 1, Q_BLOCK, KEY_GROUP), lambda d, h: (d, h, 0, 0)),
        compiler_params=_cparams(("arbitrary", "arbitrary")),
        name="bias_tables",
    )(rel_bias)


def _dsa_kernel(q_ref, ckv_ref, k3_ref, qi3_ref, wt_ref, wuk_ref, wuv_ref, bias_ref, o_ref,
                keys_ref, mask_ref, qabs_ref, acc_ref, m_ref, l_ref, *, topk):
    j = pl.program_id(1)
    nh = N_ATT_HEADS
    tq = Q_BLOCK
    t_lane = j * tq + lax.broadcasted_iota(I32, (1, tq), 1)

    n_sgrp = (j + 1 + (SCORE_GROUP // tq - 1)) // (SCORE_GROUP // tq)
    qi3 = qi3_ref[...].reshape(N_IDX_HEADS * tq, IDX_K)

    def score_body(g, carry):
        r0 = pl.multiple_of(g * SCORE_GROUP, SCORE_GROUP)
        lt = _dot_nt(k3_ref[pl.ds(r0, SCORE_GROUP), :], qi3)
        sc = jnp.zeros((SCORE_GROUP, tq), F32)
        for h in range(N_IDX_HEADS):
            sc = sc + jnp.maximum(lt[:, h * tq:(h + 1) * tq], 0.0) * wt_ref[h:h + 1, :]
        s_row = r0 + lax.broadcasted_iota(I32, (SCORE_GROUP, 1), 0)
        sc = jnp.where(s_row <= t_lane, sc + 0.0, -jnp.inf)
        bits = pltpu.bitcast(sc, I32)
        keys_ref[pl.ds(r0, SCORE_GROUP), :] = jnp.where(bits < 0, bits ^ 0x7FFFFFFF, bits)
        return carry

    lax.fori_loop(0, n_sgrp, score_body, 0)

    def count_ge(cand):
        def body(g, c):
            r0 = pl.multiple_of(g * SCORE_GROUP, SCORE_GROUP)
            blk = keys_ref[pl.ds(r0, SCORE_GROUP), :]
            return c + jnp.sum(jnp.where(blk >= cand, 1.0, 0.0), axis=0, keepdims=True)
        return lax.fori_loop(0, n_sgrp, body, jnp.zeros((1, tq), F32))

    thr0 = jnp.where(count_ge(jnp.zeros((1, tq), I32)) >= topk, 0, INT_MIN).astype(I32)

    def bit_body(i, thr):
        cand = thr | jnp.left_shift(jnp.int32(1), 30 - i)
        return jnp.where(count_ge(cand) >= topk, cand, thr)

    thr = lax.fori_loop(0, 31, bit_body, thr0)

    n_agrp = j // 2 + 1

    def mask_body(kb, carry):
        r0 = pl.multiple_of(kb * tq, tq)
        blk = keys_ref[pl.ds(r0, tq), :]
        s_row = r0 + lax.broadcasted_iota(I32, (tq, 1), 0)
        sel = (blk >= thr) & (s_row <= t_lane)
        mask_ref[:, pl.ds(r0, tq)] = jnp.transpose(jnp.where(sel, 0.0, NEG_BIG).astype(F32))
        return carry

    lax.fori_loop(0, 2 * n_agrp, mask_body, 0)

    for h in range(nh):
        qh = q_ref[:, h * HEAD_DIM:(h + 1) * HEAD_DIM].astype(BF16)
        qabs_ref[h * tq:(h + 1) * tq, :] = (_dot_nt(qh, wuk_ref[h]) * ATT_SCALE).astype(BF16)

    m_ref[...] = jnp.full(m_ref.shape, NEG_BIG, F32)
    l_ref[...] = jnp.zeros(l_ref.shape, F32)
    acc_ref[...] = jnp.zeros(acc_ref.shape, F32)

    def attend(g, bias):
        c0 = pl.multiple_of(g * KEY_GROUP, KEY_GROUP)
        c = ckv_ref[pl.ds(c0, KEY_GROUP), :]
        s = _dot_nt(qabs_ref[...], c).reshape(nh, tq, KEY_GROUP)
        s = s + mask_ref[:, pl.ds(c0, KEY_GROUP)][None]
        if bias is not None:
            s = s + bias
        s = s.reshape(nh * tq, KEY_GROUP)
        m_prev = m_ref[...]
        m_new = jnp.maximum(m_prev, jnp.max(s, axis=-1, keepdims=True))
        p = jnp.exp(s - m_new)
        corr = jnp.exp(m_prev - m_new)
        l_ref[...] = corr * l_ref[...] + jnp.sum(p, axis=-1, keepdims=True)
        acc_ref[...] = corr * acc_ref[...] + _dot(p.astype(BF16), c)
        m_ref[...] = m_new

    n_far = jnp.maximum(j // 2 - 1 + j % 2, 0)

    def far_body(g, carry):
        attend(g, None)
        return carry

    lax.fori_loop(0, n_far, far_body, 0)

    @pl.when((j % 2 == 0) & (j >= 2))
    def _():
        attend(j // 2 - 1, bias_ref[2])

    attend(j // 2, bias_ref[j % 2])

    o_lat = (acc_ref[...] / l_ref[...]).astype(BF16)
    for h in range(nh):
        o_ref[:, h * HEAD_DIM:(h + 1) * HEAD_DIM] = _dot(
            o_lat[h * tq:(h + 1) * tq, :], wuv_ref[h]).astype(o_ref.dtype)


def dsa_attention(proj, ckv, k3, qi3, wt, wuk, wuv, bias, *, batch, seq, topk):
    tq = Q_BLOCK
    nq = seq // tq
    nh = N_ATT_HEADS
    r = KV_LORA_RANK
    seq_pad = -(-seq // SCORE_GROUP) * SCORE_GROUP
    return pl.pallas_call(
        functools.partial(_dsa_kernel, topk=topk),
        out_shape=jax.ShapeDtypeStruct((batch * seq, ATT_WIDTH), BF16),
        grid=(batch, nq),
        in_specs=[
            pl.BlockSpec((tq, ATT_WIDTH), lambda b, j: (b * nq + j, 0)),
            _single((seq, r), lambda b, j: (b, 0)),
            _single((seq, IDX_K), lambda b, j: (b, 0)),
            pl.BlockSpec((N_IDX_HEADS, tq, IDX_K), lambda b, j: (b * nq + j, 0, 0)),
            pl.BlockSpec((N_IDX_HEADS, tq), lambda b, j: (b * nq + j, 0)),
            _single((nh, r, HEAD_DIM), lambda b, j: (0, 0, 0)),
            _single((nh, r, HEAD_DIM), lambda b, j: (0, 0, 0)),
            _single((3, nh, tq, KEY_GROUP), lambda b, j: (0, 0, 0, 0)),
        ],
        out_specs=pl.BlockSpec((tq, ATT_WIDTH), lambda b, j: (b * nq + j, 0)),
        scratch_shapes=[
            pltpu.VMEM((seq_pad, tq), I32),
            pltpu.VMEM((tq, seq_pad), F32),
            pltpu.VMEM((nh * tq, r), BF16),
            pltpu.VMEM((nh * tq, r), F32),
            pltpu.VMEM((nh * tq, 1), F32),
            pltpu.VMEM((nh * tq, 1), F32),
        ],
        compiler_params=_cparams(("parallel", "arbitrary")),
        name="dsa",
    )(proj, ckv, k3, qi3, wt, wuk, wuv, bias)


def _cumsum_rows(x):
    n = x.shape[0]
    row = lax.broadcasted_iota(I32, (n, 1), 0)
    sh = 1
    while sh < n:
        x = x + jnp.where(row >= sh, pltpu.roll(x, sh, 0), 0.0)
        sh *= 2
    return x


def _hgrn2_kernel(q_ref, f_ref, i_ref, g_ref, lb_ref, ng_ref, o_ref, st_ref, *, layer):
    c = REC_CHUNK
    dk = HEAD_DIM

    @pl.when(pl.program_id(2) == 0)
    def _():
        st_ref[...] = jnp.zeros_like(st_ref)

    lbr = lb_ref[...]
    e = jnp.exp(lbr - jnp.max(lbr, axis=0, keepdims=True))
    pr = e / jnp.sum(e, axis=0, keepdims=True)
    lb = jnp.zeros((1, dk), F32)
    for i in range(1, layer + 1):
        lb = lb + pr[i:i + 1, :]

    row = lax.broadcasted_iota(I32, (c, 1), 0)
    col = lax.broadcasted_iota(I32, (1, c), 1)
    sub = 8
    nsub = c // sub
    tl3 = lax.broadcasted_iota(I32, (1, sub, 1), 1)
    g3 = lax.broadcasted_iota(I32, (nsub, 1, 1), 0)
    col3 = lax.broadcasted_iota(I32, (1, 1, c), 2)
    levels = []
    m = c // 2
    while m >= sub:
        levels.append(m)
        m //= 2

    def chunk_body(ci, carry):
        r0 = pl.multiple_of(ci * c, c)
        q = q_ref[pl.ds(r0, c), :]
        qf = q * jax.nn.sigmoid(q)
        f = lb + (1.0 - lb) * jax.nn.sigmoid(f_ref[pl.ds(r0, c), :])
        k = 1.0 - f
        v = i_ref[pl.ds(r0, c), :].astype(BF16)
        cum = _cumsum_rows(jnp.log(f))

        a = jnp.zeros((c, c), F32)
        for m in levels:
            first = (row & (2 * m - 1)) < m
            bnd = jnp.concatenate(
                [jnp.broadcast_to(cum[b * 2 * m + m - 1:b * 2 * m + m, :], (2 * m, dk))
                 for b in range(c // (2 * m))], axis=0)
            qs = jnp.where(first, 0.0, qf * jnp.exp(jnp.minimum(cum - bnd, 0.0))).astype(BF16)
            ks = jnp.where(first, k * jnp.exp(jnp.minimum(bnd - cum, 0.0)), 0.0).astype(BF16)
            same = (row & -(2 * m)) == (col & -(2 * m))
            a = a + jnp.where(same, _dot_nt(qs, ks), 0.0)
        cum3 = cum.reshape(nsub, sub, dk)
        k3 = k.reshape(nsub, sub, dk)
        qf3 = qf.reshape(nsub, sub, dk)
        a3 = jnp.zeros((nsub, sub, c), F32)
        for s in range(sub):
            d = jnp.where(tl3 >= s, cum3 - cum3[:, s:s + 1, :], -jnp.inf)
            prod = qf3 * k3[:, s:s + 1, :] * jnp.exp(d)
            colsum = jnp.sum(prod, axis=-1, keepdims=True)
            a3 = a3 + jnp.where(col3 == g3 * sub + s, colsum, 0.0)
        a = a + a3.reshape(c, c)

        st = st_ref[...]
        o = _dot(a.astype(BF16), v) + _dot_nt((qf * jnp.exp(cum)).astype(BF16), st.astype(BF16))
        last = cum[c - 1:c, :]
        kl = (k * jnp.exp(last - cum)).astype(BF16)
        st_ref[...] = st * jnp.exp(last) + _dot_tn(v, kl)

        ms = jnp.mean(o * o, axis=-1, keepdims=True)
        gt = g_ref[pl.ds(r0, c), :]
        on = o * lax.rsqrt(ms + LN_EPS) * ng_ref[...]
        o_ref[pl.ds(r0, c), :] = (on * (gt * jax.nn.sigmoid(gt))).astype(o_ref.dtype)
        return carry

    lax.fori_loop(0, q_ref.shape[0] // c, chunk_body, 0)


def hgrn2(proj, rec_lb, norm_g, *, layer, batch, seq):
    tb = min(REC_TBLOCK, seq)
    nt = seq // tb
    nh = N_REC_HEADS
    cb = COL_REC // HEAD_DIM

    def seg(k):
        return pl.BlockSpec((tb, HEAD_DIM), lambda b, h, t, k=k: (b * nt + t, cb + k * nh + h))

    return pl.pallas_call(
        functools.partial(_hgrn2_kernel, layer=layer),
        out_shape=jax.ShapeDtypeStruct((batch * seq, REC_WIDTH), BF16),
        grid=(batch, nh, nt),
        in_specs=[
            seg(0), seg(1), seg(2), seg(3),
            pl.BlockSpec((rec_lb.shape[0], HEAD_DIM), lambda b, h, t: (0, h)),
            pl.BlockSpec((1, HEAD_DIM), lambda b, h, t: (0, h)),
        ],
        out_specs=pl.BlockSpec((tb, HEAD_DIM), lambda b, h, t: (b * nt + t, h)),
        scratch_shapes=[pltpu.VMEM((HEAD_DIM, HEAD_DIM), F32)],
        compiler_params=_cparams(("parallel", "parallel", "arbitrary")),
        name="hgrn2",
    )(proj, proj, proj, proj, rec_lb, norm_g)


def _col_blocks(w, tn):
    k, n = w.shape
    return jnp.transpose(w.astype(BF16).reshape(k, n // tn, tn), (1, 0, 2))


def _pack_w_in(w):
    kv0 = ATT_WIDTH
    qi0 = kv0 + KV_LORA_RANK
    sm0 = qi0 + IDX_WIDTH
    sm1 = sm0 + IDX_DIM + N_IDX_HEADS
    pad = jnp.zeros((w.shape[0], SMALL_WIDTH - IDX_DIM - N_IDX_HEADS), w.dtype)
    return jnp.concatenate([w[:, :kv0], w[:, qi0:sm0], w[:, kv0:qi0], w[:, sm0:sm1], pad, w[:, sm1:]], axis=1)


def _row_tile(m, want):
    t = min(want, m)
    while m % t:
        t //= 2
    return t


def kernel(x, mem, w_in, w_uk, w_uv, kv_norm_g, idx_kn_g, idx_kn_b, rec_lb, rec_norm_g, w_out, rel_bias,
           wq_c, wk_c, wv_c, wo_c, ffn_gate, ffn_up, ffn_down, ln_g, ln_b):
    batch, seq, d = x.shape
    depth = w_in.shape[0]
    m = batch * seq
    alpha = (2 * depth) ** 0.25
    topk = min(TOPK_MAX, seq // 4)
    d_ff = ffn_gate.shape[-1]
    tf = 256 if d_ff % 256 == 0 else 128

    xf = x.reshape(m, d)
    mem_b = mem.reshape(batch * mem.shape[1], d).astype(BF16)
    bias = bias_tables(rel_bias)
    rec_lb = rec_lb.astype(F32)

    def ffn(xf, l, i):
        return ffn_ln(xf, _col_blocks(ffn_gate[l, i], tf), _col_blocks(ffn_up[l, i], tf),
                      ffn_down[l, i].astype(BF16), ln_g[l, 3 * i:3 * i + 1], ln_b[l, 3 * i:3 * i + 1],
                      alpha=alpha, tm=_row_tile(m, 512))

    xb = None
    for l in range(depth):
        xf, xb = ffn(xf, l, 0)

        proj = matmul(xb, _col_blocks(_pack_w_in(w_in[l]), 256), tm=_row_tile(m, 1024), out_dtype=F32)
        ckv, k3, qi3, wt = dsa_prep(proj, kv_norm_g[l][None], idx_kn_g[l][None], idx_kn_b[l][None])
        o_att = dsa_attention(proj, ckv, k3, qi3, wt, w_uk[l].astype(BF16), w_uv[l].astype(BF16), bias,
                              batch=batch, seq=seq, topk=topk)
        o_rec = hgrn2(proj, rec_lb, rec_norm_g[l][None], layer=l, batch=batch, seq=seq)
        mix_in = jnp.concatenate([o_att, o_rec], axis=-1)
        xf, xb = proj_ln(mix_in, w_out[l].astype(BF16), xf, ln_g[l, 1:2], ln_b[l, 1:2],
                         alpha=alpha, tm=_row_tile(m, 512), tk=512)

        kv = matmul(mem_b, _col_blocks(jnp.concatenate([wk_c[l], wv_c[l]], axis=1), 256),
                    tm=_row_tile(mem_b.shape[0], 512), out_dtype=BF16)
        kv = kv.reshape(batch, mem.shape[1], kv.shape[-1])
        xf, xb = cross_ln(xf, wq_c[l].astype(BF16), kv, wo_c[l].astype(BF16), ln_g[l, 2:3], ln_b[l, 2:3],
                          alpha=alpha, tm=_row_tile(seq, 256), rows_per_batch=seq)

        xf, xb = ffn(xf, l, 1)
    return xf.reshape(batch, seq, d)
```

```python
import functools
import math

import jax
import jax.numpy as jnp
import numpy as np
from jax import lax
from jax.experimental import pallas as pl
from jax.experimental.pallas import tpu as pltpu

F32 = jnp.float32
BF16 = jnp.bfloat16
I32 = jnp.int32

LANE = 128
VMEM_LIMIT_V7X = 60 * 1024 * 1024

HEAD_DIM = 128
N_ATT_HEADS = 16
N_REC_HEADS = 16
KV_LORA_RANK = 512
N_IDX_HEADS = 16
IDX_DIM = 64
TOPK_MAX = 256
Q_BLOCK = 128
N_BUCKETS = 32
MAX_DISTANCE = 128
N_CROSS_HEADS = 4
LN_EPS = 1e-5
ATT_SCALE = HEAD_DIM ** -0.5
IDX_SCALE = IDX_DIM ** -0.5
IDX_HEAD_SCALE = N_IDX_HEADS ** -0.5

ATT_WIDTH = N_ATT_HEADS * HEAD_DIM
IDX_WIDTH = N_IDX_HEADS * IDX_DIM
REC_WIDTH = N_REC_HEADS * HEAD_DIM
COL_QIDX = ATT_WIDTH
COL_KV = COL_QIDX + IDX_WIDTH
COL_SMALL = COL_KV + KV_LORA_RANK
SMALL_WIDTH = 256
COL_REC = COL_SMALL + SMALL_WIDTH
PROJ_WIDTH = COL_REC + 4 * REC_WIDTH

NEG_BIG = -1e30
INT_MIN = -2 ** 31

REC_CHUNK = 64
REC_TBLOCK = 512
KEY_GROUP = 256
SCORE_GROUP = 512
IDX_K = 256
COUNT_ROWS = 64


def _cparams(sem):
    return pltpu.CompilerParams(dimension_semantics=sem, vmem_limit_bytes=VMEM_LIMIT_V7X)


def _single(shape, imap):
    return pl.BlockSpec(shape, imap, pipeline_mode=pl.Buffered(1))


def _layer_norm_rows(y, g, b):
    mu = jnp.mean(y, axis=-1, keepdims=True)
    d = y - mu
    var = jnp.mean(d * d, axis=-1, keepdims=True)
    return d * lax.rsqrt(var + LN_EPS) * g + b


def _dot(a, b):
    return jnp.dot(a, b, preferred_element_type=F32)


def _dot_nt(a, b):
    return lax.dot_general(a, b, (((1,), (1,)), ((), ())), preferred_element_type=F32)


def _dot_tn(a, b):
    return lax.dot_general(a, b, (((0,), (0,)), ((), ())), preferred_element_type=F32)


def _ffn_ln_kernel(x_ref, wg_ref, wu_ref, wd_ref, g_ref, b_ref, o_ref, ob_ref, xb_ref, *, alpha):
    f = pl.program_id(1)

    @pl.when(f == 0)
    def _():
        xb_ref[...] = x_ref[...].astype(BF16)
        o_ref[...] = jnp.zeros_like(o_ref)

    xb = xb_ref[...]
    h = _dot(xb, wg_ref[0])
    u = _dot(xb, wu_ref[0])
    a = (h * jax.nn.sigmoid(h) * u).astype(BF16)
    o_ref[...] += _dot(a, wd_ref[...])

    @pl.when(f == pl.num_programs(1) - 1)
    def _():
        y = alpha * x_ref[...] + 0.5 * o_ref[...]
        out = _layer_norm_rows(y, g_ref[...], b_ref[...])
        o_ref[...] = out
        ob_ref[...] = out.astype(BF16)


def ffn_ln(x, wg_blk, wu_blk, wd, g, b, *, alpha, tm):
    m, d = x.shape
    nf, _, tf = wg_blk.shape
    return pl.pallas_call(
        functools.partial(_ffn_ln_kernel, alpha=alpha),
        out_shape=(jax.ShapeDtypeStruct((m, d), F32), jax.ShapeDtypeStruct((m, d), BF16)),
        grid=(m // tm, nf),
        in_specs=[
            _single((tm, d), lambda i, f: (i, 0)),
            pl.BlockSpec((1, d, tf), lambda i, f: (f, 0, 0)),
            pl.BlockSpec((1, d, tf), lambda i, f: (f, 0, 0)),
            pl.BlockSpec((tf, d), lambda i, f: (f, 0)),
            pl.BlockSpec((1, d), lambda i, f: (0, 0)),
            pl.BlockSpec((1, d), lambda i, f: (0, 0)),
        ],
        out_specs=(_single((tm, d), lambda i, f: (i, 0)), _single((tm, d), lambda i, f: (i, 0))),
        scratch_shapes=[pltpu.VMEM((tm, d), BF16)],
        compiler_params=_cparams(("parallel", "arbitrary")),
        name="ffn_ln",
    )(x, wg_blk, wu_blk, wd, g, b)


def _matmul_kernel(x_ref, w_ref, o_ref):
    o_ref[...] = _dot(x_ref[...], w_ref[0]).astype(o_ref.dtype)


def matmul(x, w_blk, *, tm, out_dtype):
    m, k = x.shape
    nn, _, tn = w_blk.shape
    return pl.pallas_call(
        _matmul_kernel,
        out_shape=jax.ShapeDtypeStruct((m, nn * tn), out_dtype),
        grid=(m // tm, nn),
        in_specs=[
            pl.BlockSpec((tm, k), lambda i, j: (i, 0)),
            pl.BlockSpec((1, k, tn), lambda i, j: (j, 0, 0)),
        ],
        out_specs=pl.BlockSpec((tm, tn), lambda i, j: (i, j)),
        compiler_params=_cparams(("parallel", "arbitrary")),
        name="matmul",
    )(x, w_blk)


def _proj_ln_kernel(a_ref, w_ref, x_ref, g_ref, b_ref, o_ref, ob_ref, *, alpha):
    k = pl.program_id(1)

    @pl.when(k == 0)
    def _():
        o_ref[...] = jnp.zeros_like(o_ref)

    o_ref[...] += _dot(a_ref[...], w_ref[...])

    @pl.when(k == pl.num_programs(1) - 1)
    def _():
        y = alpha * x_ref[...] + o_ref[...]
        out = _layer_norm_rows(y, g_ref[...], b_ref[...])
        o_ref[...] = out
        ob_ref[...] = out.astype(BF16)


def proj_ln(a, w, x, g, b, *, alpha, tm, tk):
    m, kdim = a.shape
    d = w.shape[1]
    return pl.pallas_call(
        functools.partial(_proj_ln_kernel, alpha=alpha),
        out_shape=(jax.ShapeDtypeStruct((m, d), F32), jax.ShapeDtypeStruct((m, d), BF16)),
        grid=(m // tm, kdim // tk),
        in_specs=[
            pl.BlockSpec((tm, tk), lambda i, k: (i, k)),
            pl.BlockSpec((tk, d), lambda i, k: (k, 0)),
            _single((tm, d), lambda i, k: (i, 0)),
            pl.BlockSpec((1, d), lambda i, k: (0, 0)),
            pl.BlockSpec((1, d), lambda i, k: (0, 0)),
        ],
        out_specs=(_single((tm, d), lambda i, k: (i, 0)), _single((tm, d), lambda i, k: (i, 0))),
        compiler_params=_cparams(("parallel", "arbitrary")),
        name="proj_ln",
    )(a, w, x, g, b)


def _cross_ln_kernel(x_ref, wq_ref, kv_ref, wo_ref, g_ref, b_ref, o_ref, ob_ref, *, alpha, n_heads):
    x = x_ref[...]
    q = _dot(x.astype(BF16), wq_ref[...])
    kv = kv_ref[0]
    cw = n_heads * HEAD_DIM
    outs = []
    for h in range(n_heads):
        qh = q[:, h * HEAD_DIM:(h + 1) * HEAD_DIM].astype(BF16)
        kh = kv[:, h * HEAD_DIM:(h + 1) * HEAD_DIM]
        vh = kv[:, cw + h * HEAD_DIM:cw + (h + 1) * HEAD_DIM]
        logits = _dot_nt(qh, kh) * ATT_SCALE
        mx = jnp.max(logits, axis=-1, keepdims=True)
        e = jnp.exp(logits - mx)
        p = e / jnp.sum(e, axis=-1, keepdims=True)
        outs.append(_dot(p.astype(BF16), vh))
    o = jnp.concatenate(outs, axis=-1).astype(BF16)
    y = alpha * x + _dot(o, wo_ref[...])
    out = _layer_norm_rows(y, g_ref[...], b_ref[...])
    o_ref[...] = out
    ob_ref[...] = out.astype(BF16)


def cross_ln(x, wq, kv, wo, g, b, *, alpha, tm, rows_per_batch):
    m, d = x.shape
    cw = wq.shape[1]
    nmem = kv.shape[1]
    steps_per_batch = rows_per_batch // tm
    return pl.pallas_call(
        functools.partial(_cross_ln_kernel, alpha=alpha, n_heads=cw // HEAD_DIM),
        out_shape=(jax.ShapeDtypeStruct((m, d), F32), jax.ShapeDtypeStruct((m, d), BF16)),
        grid=(m // tm,),
        in_specs=[
            pl.BlockSpec((tm, d), lambda i: (i, 0)),
            _single((d, cw), lambda i: (0, 0)),
            pl.BlockSpec((1, nmem, 2 * cw), lambda i: (i // steps_per_batch, 0, 0)),
            _single((cw, d), lambda i: (0, 0)),
            pl.BlockSpec((1, d), lambda i: (0, 0)),
            pl.BlockSpec((1, d), lambda i: (0, 0)),
        ],
        out_specs=(pl.BlockSpec((tm, d), lambda i: (i, 0)), pl.BlockSpec((tm, d), lambda i: (i, 0))),
        compiler_params=_cparams(("parallel",)),
        name="cross_ln",
    )(x, wq, kv, wo, g, b)


def _split3(x, order):
    hi = x.astype(BF16).astype(F32)
    lo = x - hi
    parts = [hi if c == "h" else lo for c in order]
    parts.append(jnp.zeros((x.shape[0], IDX_K - 3 * x.shape[1]), F32))
    return jnp.concatenate(parts, axis=-1).astype(BF16)


def _dsa_prep_kernel(kv_ref, qi_ref, sm_ref, kvg_ref, kng_ref, knb_ref,
                     ckv_ref, ckvt_ref, k3_ref, qi3_ref, wt_ref):
    kv = kv_ref[...]
    ms = jnp.mean(kv * kv, axis=-1, keepdims=True)
    ckv = kv * lax.rsqrt(ms + LN_EPS) * kvg_ref[...]
    ckv_ref[...] = ckv.astype(BF16)
    ckvt_ref[0] = jnp.transpose(ckv).astype(BF16)

    sm = sm_ref[...]
    kidx = _layer_norm_rows(sm[:, :IDX_DIM], kng_ref[...], knb_ref[...])
    k3_ref[...] = _split3(kidx, "hlh")

    wt = jnp.transpose(sm)
    wt_ref[...] = wt[IDX_DIM:IDX_DIM + N_IDX_HEADS, :] * (IDX_HEAD_SCALE * IDX_SCALE)

    qi = qi_ref[...]
    for h in range(N_IDX_HEADS):
        qi3_ref[h] = _split3(qi[:, h * IDX_DIM:(h + 1) * IDX_DIM], "hhl")


def dsa_prep(proj, kv_g, kn_g, kn_b, *, seq):
    m = proj.shape[0]
    tq = Q_BLOCK
    nq = m // tq
    nqb = seq // tq
    return pl.pallas_call(
        _dsa_prep_kernel,
        out_shape=(
            jax.ShapeDtypeStruct((m, KV_LORA_RANK), BF16),
            jax.ShapeDtypeStruct((m // seq, KV_LORA_RANK, seq), BF16),
            jax.ShapeDtypeStruct((m, IDX_K), BF16),
            jax.ShapeDtypeStruct((nq * N_IDX_HEADS, tq, IDX_K), BF16),
            jax.ShapeDtypeStruct((nq * N_IDX_HEADS, tq), F32),
        ),
        grid=(nq,),
        in_specs=[
            pl.BlockSpec((tq, KV_LORA_RANK), lambda i: (i, COL_KV // KV_LORA_RANK)),
            pl.BlockSpec((tq, IDX_WIDTH), lambda i: (i, COL_QIDX // IDX_WIDTH)),
            pl.BlockSpec((tq, LANE), lambda i: (i, COL_SMALL // LANE)),
            pl.BlockSpec((1, KV_LORA_RANK), lambda i: (0, 0)),
            pl.BlockSpec((1, IDX_DIM), lambda i: (0, 0)),
            pl.BlockSpec((1, IDX_DIM), lambda i: (0, 0)),
        ],
        out_specs=(
            pl.BlockSpec((tq, KV_LORA_RANK), lambda i: (i, 0)),
            pl.BlockSpec((1, KV_LORA_RANK, tq), lambda i: (i // nqb, 0, i % nqb)),
            pl.BlockSpec((tq, IDX_K), lambda i: (i, 0)),
            pl.BlockSpec((N_IDX_HEADS, tq, IDX_K), lambda i: (i, 0, 0)),
            pl.BlockSpec((N_IDX_HEADS, tq), lambda i: (i, 0)),
        ),
        compiler_params=_cparams(("parallel",)),
        name="dsa_prep",
    )(proj, proj, proj, kv_g, kn_g, kn_b)


def _bucket_thresholds():
    max_exact = N_BUCKETS // 2
    rel = np.arange(max_exact, 4 * MAX_DISTANCE, dtype=np.float32)
    large = max_exact + (np.log(rel / np.float32(max_exact)) / np.float32(math.log(MAX_DISTANCE / max_exact))
                         * np.float32(N_BUCKETS - max_exact)).astype(np.int32)
    large = np.minimum(large, N_BUCKETS - 1)
    return [int(rel[np.argmax(large >= bkt)]) for bkt in range(max_exact + 1, N_BUCKETS)]


def _bias_table_kernel(rb_ref, o_ref):
    di = pl.program_id(0)
    h = pl.program_id(1)
    max_exact = N_BUCKETS // 2
    sl = lax.broadcasted_iota(I32, (KEY_GROUP, Q_BLOCK), 0)
    tl = lax.broadcasted_iota(I32, (KEY_GROUP, Q_BLOCK), 1)
    rel = jnp.maximum(di * Q_BLOCK + tl - sl, 0)
    large = jnp.full(rel.shape, max_exact, I32)
    for th in _bucket_thresholds():
        large = large + (rel >= th).astype(I32)
    bucket = jnp.where(rel < max_exact, rel, large)
    far = rb_ref[N_BUCKETS - 1, h]
    val = jnp.zeros(rel.shape, F32)
    for bkt in range(N_BUCKETS - 1):
        val = jnp.where(bucket == bkt, rb_ref[bkt, h] - far, val)
    o_ref[0] = val


def bias_tables(rel_bias):
    nh = rel_bias.shape[1]
    return pl.pallas_call(
        _bias_table_kernel,
        out_shape=jax.ShapeDtypeStruct((3, KEY_GROUP, nh * Q_BLOCK), F32),
        grid=(3, nh),
        in_specs=[pl.BlockSpec(memory_space=pltpu.SMEM)],
        out_specs=pl.BlockSpec((1, KEY_GROUP, Q_BLOCK), lambda d, h: (d, 0, h)),
        compiler_params=_cparams(("arbitrary", "arbitrary")),
        name="bias_tables",
    )(rel_bias)


def _dsa_kernel(q_ref, ckv_ref, ckvt_ref, k3_ref, qi3_ref, wt_ref, wuk_ref, wuv_ref, bias_ref, o_ref,
                keys_ref, qabs_ref, acc_ref, *, topk):
    j = pl.program_id(1)
    nh = N_ATT_HEADS
    tq = Q_BLOCK
    t_lane = j * tq + lax.broadcasted_iota(I32, (1, tq), 1)

    n_sgrp = (j + 1 + (SCORE_GROUP // tq - 1)) // (SCORE_GROUP // tq)
    qi3 = qi3_ref[...].reshape(N_IDX_HEADS * tq, IDX_K)

    def score_body(g, carry):
        r0 = pl.multiple_of(g * SCORE_GROUP, SCORE_GROUP)
        lt = _dot_nt(k3_ref[pl.ds(r0, SCORE_GROUP), :], qi3)
        sc = jnp.zeros((SCORE_GROUP, tq), F32)
        for h in range(N_IDX_HEADS):
            sc = sc + jnp.maximum(lt[:, h * tq:(h + 1) * tq], 0.0) * wt_ref[h:h + 1, :]
        s_row = r0 + lax.broadcasted_iota(I32, (SCORE_GROUP, 1), 0)
        sc = jnp.where(s_row <= t_lane, sc + 0.0, -jnp.inf)
        bits = pltpu.bitcast(sc, I32)
        keys_ref[pl.ds(r0, SCORE_GROUP), :] = jnp.where(bits < 0, bits ^ 0x7FFFFFFF, bits)
        return carry

    lax.fori_loop(0, n_sgrp, score_body, 0)

    def count_ge(cand):
        def body(g, part):
            r0 = pl.multiple_of(g * SCORE_GROUP, SCORE_GROUP)
            ind = jnp.where(keys_ref[pl.ds(r0, SCORE_GROUP), :] >= cand, 1.0, 0.0)
            return part + jnp.sum(ind.reshape(SCORE_GROUP // COUNT_ROWS, COUNT_ROWS, tq), axis=0)
        part = lax.fori_loop(0, n_sgrp, body, jnp.zeros((COUNT_ROWS, tq), F32))
        return jnp.sum(part, axis=0, keepdims=True)

    thr0 = jnp.where(count_ge(jnp.zeros((1, tq), I32)) >= topk, 0, INT_MIN).astype(I32)

    def bit_body(i, thr):
        cand = thr | jnp.left_shift(jnp.int32(1), 30 - i)
        return jnp.where(count_ge(cand) >= topk, cand, thr)

    thr = lax.fori_loop(0, 31, bit_body, thr0)

    for h in range(nh):
        qh = q_ref[:, h * HEAD_DIM:(h + 1) * HEAD_DIM].astype(BF16)
        qabs_ref[:, h * tq:(h + 1) * tq] = (_dot_nt(wuk_ref[h], qh) * ATT_SCALE).astype(BF16)

    acc_ref[...] = jnp.zeros(acc_ref.shape, F32)

    def attend(g, carry, bias_di):
        m, l = carry
        c0 = pl.multiple_of(g * KEY_GROUP, KEY_GROUP)
        s_all = _dot(ckv_ref[pl.ds(c0, KEY_GROUP), :], qabs_ref[...])
        s_row = c0 + lax.broadcasted_iota(I32, (KEY_GROUP, 1), 0)
        sel = (keys_ref[pl.ds(c0, KEY_GROUP), :] >= thr) & (s_row <= t_lane)
        addm = jnp.where(sel, 0.0, NEG_BIG)
        ps, ms, ls, corrs = [], [], [], []
        for h in range(nh):
            lanes = slice(h * tq, (h + 1) * tq)
            s = s_all[:, lanes] + addm
            if bias_di is not None:
                s = s + bias_ref[bias_di, :, lanes]
            m_prev = m[:, lanes]
            m_new = jnp.maximum(m_prev, jnp.max(s, axis=0, keepdims=True))
            p = jnp.exp(s - m_new)
            corr = jnp.exp(m_prev - m_new)
            ls.append(corr * l[:, lanes] + jnp.sum(p, axis=0, keepdims=True))
            ms.append(m_new)
            corrs.append(corr)
            ps.append(p.astype(BF16))
        pt = jnp.concatenate(ps, axis=1)
        corr_all = jnp.concatenate(corrs, axis=1)
        acc_ref[...] = acc_ref[...] * corr_all + _dot(ckvt_ref[0, :, pl.ds(c0, KEY_GROUP)], pt)
        return jnp.concatenate(ms, axis=1), jnp.concatenate(ls, axis=1)

    carry = (jnp.full((1, nh * tq), NEG_BIG, F32), jnp.zeros((1, nh * tq), F32))
    n_far = jnp.maximum(j // 2 - 1 + j % 2, 0)
    carry = lax.fori_loop(0, n_far, lambda g, cr: attend(g, cr, None), carry)
    carry = lax.cond((j % 2 == 0) & (j >= 2), lambda cr: attend(j // 2 - 1, cr, 2), lambda cr: cr, carry)
    _, l = attend(j // 2, carry, j % 2)

    o_lat = (acc_ref[...] / l).astype(BF16)
    for h in range(nh):
        o_ref[:, h * HEAD_DIM:(h + 1) * HEAD_DIM] = _dot_tn(
            o_lat[:, h * tq:(h + 1) * tq], wuv_ref[h]).astype(o_ref.dtype)


def dsa_attention(proj, ckv, ckvt, k3, qi3, wt, wuk, wuv, bias, *, batch, seq, topk):
    tq = Q_BLOCK
    nq = seq // tq
    nh = N_ATT_HEADS
    r = KV_LORA_RANK
    seq_pad = -(-seq // SCORE_GROUP) * SCORE_GROUP
    return pl.pallas_call(
        functools.partial(_dsa_kernel, topk=topk),
        out_shape=jax.ShapeDtypeStruct((batch * seq, ATT_WIDTH), BF16),
        grid=(batch, nq),
        in_specs=[
            pl.BlockSpec((tq, ATT_WIDTH), lambda b, j: (b * nq + j, 0)),
            _single((seq, r), lambda b, j: (b, 0)),
            _single((1, r, seq), lambda b, j: (b, 0, 0)),
            _single((seq, IDX_K), lambda b, j: (b, 0)),
            pl.BlockSpec((N_IDX_HEADS, tq, IDX_K), lambda b, j: (b * nq + j, 0, 0)),
            pl.BlockSpec((N_IDX_HEADS, tq), lambda b, j: (b * nq + j, 0)),
            _single((nh, r, HEAD_DIM), lambda b, j: (0, 0, 0)),
            _single((nh, r, HEAD_DIM), lambda b, j: (0, 0, 0)),
            _single((3, KEY_GROUP, nh * tq), lambda b, j: (0, 0, 0)),
        ],
        out_specs=pl.BlockSpec((tq, ATT_WIDTH), lambda b, j: (b * nq + j, 0)),
        scratch_shapes=[
            pltpu.VMEM((seq_pad, tq), I32),
            pltpu.VMEM((r, nh * tq), BF16),
            pltpu.VMEM((r, nh * tq), F32),
        ],
        compiler_params=_cparams(("parallel", "arbitrary")),
        name="dsa",
    )(proj, ckv, ckvt, k3, qi3, wt, wuk, wuv, bias)


def _cumsum_rows(x):
    n = x.shape[0]
    row = lax.broadcasted_iota(I32, (n, 1), 0)
    sh = 1
    while sh < n:
        x = x + jnp.where(row >= sh, pltpu.roll(x, sh, 0), 0.0)
        sh *= 2
    return x


def _hgrn2_kernel(q_ref, f_ref, i_ref, g_ref, lb_ref, ng_ref, o_ref, st_ref, *, layer):
    c = REC_CHUNK
    dk = HEAD_DIM

    @pl.when(pl.program_id(2) == 0)
    def _():
        st_ref[...] = jnp.zeros_like(st_ref)

    lbr = lb_ref[...]
    e = jnp.exp(lbr - jnp.max(lbr, axis=0, keepdims=True))
    pr = e / jnp.sum(e, axis=0, keepdims=True)
    lb = jnp.zeros((1, dk), F32)
    for i in range(1, layer + 1):
        lb = lb + pr[i:i + 1, :]

    row = lax.broadcasted_iota(I32, (c, 1), 0)
    col = lax.broadcasted_iota(I32, (1, c), 1)
    sub = 8
    nsub = c // sub
    tl3 = lax.broadcasted_iota(I32, (1, sub, 1), 1)
    g3 = lax.broadcasted_iota(I32, (nsub, 1, 1), 0)
    col3 = lax.broadcasted_iota(I32, (1, 1, c), 2)
    levels = []
    m = c // 2
    while m >= sub:
        levels.append(m)
        m //= 2

    def chunk_body(ci, carry):
        r0 = pl.multiple_of(ci * c, c)
        q = q_ref[pl.ds(r0, c), :]
        qf = q * jax.nn.sigmoid(q)
        f = lb + (1.0 - lb) * jax.nn.sigmoid(f_ref[pl.ds(r0, c), :])
        k = 1.0 - f
        v = i_ref[pl.ds(r0, c), :].astype(BF16)
        cum = _cumsum_rows(jnp.log(f))

        a = jnp.zeros((c, c), F32)
        for m in levels:
            first = (row & (2 * m - 1)) < m
            bnd = jnp.concatenate(
                [jnp.broadcast_to(cum[b * 2 * m + m - 1:b * 2 * m + m, :], (2 * m, dk))
                 for b in range(c // (2 * m))], axis=0)
            qs = jnp.where(first, 0.0, qf * jnp.exp(jnp.minimum(cum - bnd, 0.0))).astype(BF16)
            ks = jnp.where(first, k * jnp.exp(jnp.minimum(bnd - cum, 0.0)), 0.0).astype(BF16)
            same = (row & -(2 * m)) == (col & -(2 * m))
            a = a + jnp.where(same, _dot_nt(qs, ks), 0.0)
        cum3 = cum.reshape(nsub, sub, dk)
        k3 = k.reshape(nsub, sub, dk)
        qf3 = qf.reshape(nsub, sub, dk)
        a3 = jnp.zeros((nsub, sub, c), F32)
        for s in range(sub):
            d = jnp.where(tl3 >= s, cum3 - cum3[:, s:s + 1, :], -jnp.inf)
            prod = qf3 * k3[:, s:s + 1, :] * jnp.exp(d)
            colsum = jnp.sum(prod, axis=-1, keepdims=True)
            a3 = a3 + jnp.where(col3 == g3 * sub + s, colsum, 0.0)
        a = a + a3.reshape(c, c)

        st = st_ref[...]
        o = _dot(a.astype(BF16), v) + _dot_nt((qf * jnp.exp(cum)).astype(BF16), st.astype(BF16))
        last = cum[c - 1:c, :]
        kl = (k * jnp.exp(last - cum)).astype(BF16)
        st_ref[...] = st * jnp.exp(last) + _dot_tn(v, kl)

        ms = jnp.mean(o * o, axis=-1, keepdims=True)
        gt = g_ref[pl.ds(r0, c), :]
        on = o * lax.rsqrt(ms + LN_EPS) * ng_ref[...]
        o_ref[pl.ds(r0, c), :] = (on * (gt * jax.nn.sigmoid(gt))).astype(o_ref.dtype)
        return carry

    lax.fori_loop(0, q_ref.shape[0] // c, chunk_body, 0)


def hgrn2(proj, rec_lb, norm_g, *, layer, batch, seq):
    tb = min(REC_TBLOCK, seq)
    nt = seq // tb
    nh = N_REC_HEADS
    cb = COL_REC // HEAD_DIM

    def seg(k):
        return pl.BlockSpec((tb, HEAD_DIM), lambda b, h, t, k=k: (b * nt + t, cb + k * nh + h))

    return pl.pallas_call(
        functools.partial(_hgrn2_kernel, layer=layer),
        out_shape=jax.ShapeDtypeStruct((batch * seq, REC_WIDTH), BF16),
        grid=(batch, nh, nt),
        in_specs=[
            seg(0), seg(1), seg(2), seg(3),
            pl.BlockSpec((rec_lb.shape[0], HEAD_DIM), lambda b, h, t: (0, h)),
            pl.BlockSpec((1, HEAD_DIM), lambda b, h, t: (0, h)),
        ],
        out_specs=pl.BlockSpec((tb, HEAD_DIM), lambda b, h, t: (b * nt + t, h)),
        scratch_shapes=[pltpu.VMEM((HEAD_DIM, HEAD_DIM), F32)],
        compiler_params=_cparams(("parallel", "parallel", "arbitrary")),
        name="hgrn2",
    )(proj, proj, proj, proj, rec_lb, norm_g)


def _col_blocks(w, tn):
    k, n = w.shape
    return jnp.transpose(w.astype(BF16).reshape(k, n // tn, tn), (1, 0, 2))


def _pack_w_in(w):
    kv0 = ATT_WIDTH
    qi0 = kv0 + KV_LORA_RANK
    sm0 = qi0 + IDX_WIDTH
    sm1 = sm0 + IDX_DIM + N_IDX_HEADS
    pad = jnp.zeros((w.shape[0], SMALL_WIDTH - IDX_DIM - N_IDX_HEADS), w.dtype)
    return jnp.concatenate([w[:, :kv0], w[:, qi0:sm0], w[:, kv0:qi0], w[:, sm0:sm1], pad, w[:, sm1:]], axis=1)


def _row_tile(m, want):
    t = min(want, m)
    while m % t:
        t //= 2
    return t


def kernel(x, mem, w_in, w_uk, w_uv, kv_norm_g, idx_kn_g, idx_kn_b, rec_lb, rec_norm_g, w_out, rel_bias,
           wq_c, wk_c, wv_c, wo_c, ffn_gate, ffn_up, ffn_down, ln_g, ln_b):
    batch, seq, d = x.shape
    depth = w_in.shape[0]
    m = batch * seq
    alpha = (2 * depth) ** 0.25
    topk = min(TOPK_MAX, seq // 4)
    d_ff = ffn_gate.shape[-1]
    tf = 256 if d_ff % 256 == 0 else 128

    xf = x.reshape(m, d)
    mem_b = mem.reshape(batch * mem.shape[1], d).astype(BF16)
    bias = bias_tables(rel_bias)
    rec_lb = rec_lb.astype(F32)

    def ffn(xf, l, i):
        return ffn_ln(xf, _col_blocks(ffn_gate[l, i], tf), _col_blocks(ffn_up[l, i], tf),
                      ffn_down[l, i].astype(BF16), ln_g[l, 3 * i:3 * i + 1], ln_b[l, 3 * i:3 * i + 1],
                      alpha=alpha, tm=_row_tile(m, 512))

    xb = None
    for l in range(depth):
        xf, xb = ffn(xf, l, 0)

        proj = matmul(xb, _col_blocks(_pack_w_in(w_in[l]), 256), tm=_row_tile(m, 1024), out_dtype=F32)
        ckv, ckvt, k3, qi3, wt = dsa_prep(proj, kv_norm_g[l][None], idx_kn_g[l][None], idx_kn_b[l][None], seq=seq)
        o_att = dsa_attention(proj, ckv, ckvt, k3, qi3, wt, w_uk[l].astype(BF16), w_uv[l].astype(BF16), bias,
                              batch=batch, seq=seq, topk=topk)
        o_rec = hgrn2(proj, rec_lb, rec_norm_g[l][None], layer=l, batch=batch, seq=seq)
        mix_in = jnp.concatenate([o_att, o_rec], axis=-1)
        xf, xb = proj_ln(mix_in, w_out[l].astype(BF16), xf, ln_g[l, 1:2], ln_b[l, 1:2],
                         alpha=alpha, tm=_row_tile(m, 512), tk=512)

        kv = matmul(mem_b, _col_blocks(jnp.concatenate([wk_c[l], wv_c[l]], axis=1), 256),
                    tm=_row_tile(mem_b.shape[0], 512), out_dtype=BF16)
        kv = kv.reshape(batch, mem.shape[1], kv.shape[-1])
        xf, xb = cross_ln(xf, wq_c[l].astype(BF16), kv, wo_c[l].astype(BF16), ln_g[l, 2:3], ln_b[l, 2:3],
                          alpha=alpha, tm=_row_tile(seq, 256), rows_per_batch=seq)

        xf, xb = ffn(xf, l, 1)
    return xf.reshape(batch, seq, d)
```

```python
import functools
import math

import jax
import jax.numpy as jnp
import numpy as np
from jax import lax
from jax.experimental import pallas as pl
from jax.experimental.pallas import tpu as pltpu

F32 = jnp.float32
BF16 = jnp.bfloat16
I32 = jnp.int32

LANE = 128
VMEM_LIMIT_V7X = 60 * 1024 * 1024

HEAD_DIM = 128
N_ATT_HEADS = 16
N_REC_HEADS = 16
KV_LORA_RANK = 512
N_IDX_HEADS = 16
IDX_DIM = 64
TOPK_MAX = 256
Q_BLOCK = 128
N_BUCKETS = 32
MAX_DISTANCE = 128
N_CROSS_HEADS = 4
LN_EPS = 1e-5
ATT_SCALE = HEAD_DIM ** -0.5
IDX_SCALE = IDX_DIM ** -0.5
IDX_HEAD_SCALE = N_IDX_HEADS ** -0.5

ATT_WIDTH = N_ATT_HEADS * HEAD_DIM
IDX_WIDTH = N_IDX_HEADS * IDX_DIM
REC_WIDTH = N_REC_HEADS * HEAD_DIM
COL_QIDX = ATT_WIDTH
COL_KV = COL_QIDX + IDX_WIDTH
COL_SMALL = COL_KV + KV_LORA_RANK
SMALL_WIDTH = 256
COL_REC = COL_SMALL + SMALL_WIDTH
PROJ_WIDTH = COL_REC + 4 * REC_WIDTH

NEG_BIG = -1e30
INT_MIN = -2 ** 31

REC_CHUNK = 64
REC_TBLOCK = 512
REC_HEADS_PER_STEP = 2
KEY_GROUP = 256
SCORE_GROUP = 512
IDX_K = 256
COUNT_ROWS = 64


def _cparams(sem):
    return pltpu.CompilerParams(dimension_semantics=sem, vmem_limit_bytes=VMEM_LIMIT_V7X)


def _single(shape, imap):
    return pl.BlockSpec(shape, imap, pipeline_mode=pl.Buffered(1))


def _layer_norm_rows(y, g, b):
    mu = jnp.mean(y, axis=-1, keepdims=True)
    d = y - mu
    var = jnp.mean(d * d, axis=-1, keepdims=True)
    return d * lax.rsqrt(var + LN_EPS) * g + b


def _dot(a, b):
    return jnp.dot(a, b, preferred_element_type=F32)


def _dot_nt(a, b):
    return lax.dot_general(a, b, (((1,), (1,)), ((), ())), preferred_element_type=F32)


def _dot_tn(a, b):
    return lax.dot_general(a, b, (((0,), (0,)), ((), ())), preferred_element_type=F32)


def _ffn_ln_kernel(x_ref, wg_ref, wu_ref, wd_ref, g_ref, b_ref, o_ref, ob_ref, xb_ref, *, alpha):
    f = pl.program_id(1)

    @pl.when(f == 0)
    def _():
        xb_ref[...] = x_ref[...].astype(BF16)
        o_ref[...] = jnp.zeros_like(o_ref)

    xb = xb_ref[...]
    h = _dot(xb, wg_ref[0, 0])
    u = _dot(xb, wu_ref[0, 0])
    a = (h * jax.nn.sigmoid(h) * u).astype(BF16)
    o_ref[...] += _dot(a, wd_ref[0, 0])

    @pl.when(f == pl.num_programs(1) - 1)
    def _():
        y = alpha * x_ref[...] + 0.5 * o_ref[...]
        out = _layer_norm_rows(y, g_ref[...], b_ref[...])
        o_ref[...] = out
        ob_ref[...] = out.astype(BF16)


def ffn_ln(x, wg, wu, wd, g, b, *, layer, which, alpha, tm, tf):
    m, d = x.shape
    nf = wg.shape[-1] // tf
    return pl.pallas_call(
        functools.partial(_ffn_ln_kernel, alpha=alpha),
        out_shape=(jax.ShapeDtypeStruct((m, d), F32), jax.ShapeDtypeStruct((m, d), BF16)),
        grid=(m // tm, nf),
        in_specs=[
            _single((tm, d), lambda i, f: (i, 0)),
            pl.BlockSpec((1, 1, d, tf), lambda i, f: (layer, which, 0, f)),
            pl.BlockSpec((1, 1, d, tf), lambda i, f: (layer, which, 0, f)),
            pl.BlockSpec((1, 1, tf, d), lambda i, f: (layer, which, f, 0)),
            pl.BlockSpec((1, d), lambda i, f: (0, 0)),
            pl.BlockSpec((1, d), lambda i, f: (0, 0)),
        ],
        out_specs=(_single((tm, d), lambda i, f: (i, 0)), _single((tm, d), lambda i, f: (i, 0))),
        scratch_shapes=[pltpu.VMEM((tm, d), BF16)],
        compiler_params=_cparams(("parallel", "arbitrary")),
        name="ffn_ln",
    )(x, wg, wu, wd, g, b)


def _matmul_kernel(x_ref, w_ref, o_ref):
    o_ref[...] = _dot(x_ref[...], w_ref[...]).astype(o_ref.dtype)


def matmul(x, w, *, tm, tn, out_dtype):
    m, k = x.shape
    n = w.shape[1]
    return pl.pallas_call(
        _matmul_kernel,
        out_shape=jax.ShapeDtypeStruct((m, n), out_dtype),
        grid=(m // tm, n // tn),
        in_specs=[
            pl.BlockSpec((tm, k), lambda i, j: (i, 0)),
            pl.BlockSpec((k, tn), lambda i, j: (0, j)),
        ],
        out_specs=pl.BlockSpec((tm, tn), lambda i, j: (i, j)),
        compiler_params=_cparams(("parallel", "arbitrary")),
        name="matmul",
    )(x, w)


def _proj_ln_kernel(a1_ref, a2_ref, w1_ref, w2_ref, x_ref, g_ref, b_ref, o_ref, ob_ref, *, alpha, tn):
    n = pl.program_id(1)
    c0 = pl.multiple_of(n * tn, tn)
    o_ref[:, pl.ds(c0, tn)] = _dot(a1_ref[...], w1_ref[...]) + _dot(a2_ref[...], w2_ref[...])

    @pl.when(n == pl.num_programs(1) - 1)
    def _():
        y = alpha * x_ref[...] + o_ref[...]
        out = _layer_norm_rows(y, g_ref[...], b_ref[...])
        o_ref[...] = out
        ob_ref[...] = out.astype(BF16)


def proj_ln(a1, a2, w, x, g, b, *, alpha, tm, tn):
    m, kh = a1.shape
    d = w.shape[1]
    return pl.pallas_call(
        functools.partial(_proj_ln_kernel, alpha=alpha, tn=tn),
        out_shape=(jax.ShapeDtypeStruct((m, d), F32), jax.ShapeDtypeStruct((m, d), BF16)),
        grid=(m // tm, d // tn),
        in_specs=[
            pl.BlockSpec((tm, kh), lambda i, n: (i, 0)),
            pl.BlockSpec((tm, kh), lambda i, n: (i, 0)),
            pl.BlockSpec((kh, tn), lambda i, n: (0, n)),
            pl.BlockSpec((kh, tn), lambda i, n: (1, n)),
            _single((tm, d), lambda i, n: (i, 0)),
            pl.BlockSpec((1, d), lambda i, n: (0, 0)),
            pl.BlockSpec((1, d), lambda i, n: (0, 0)),
        ],
        out_specs=(_single((tm, d), lambda i, n: (i, 0)), _single((tm, d), lambda i, n: (i, 0))),
        compiler_params=_cparams(("parallel", "arbitrary")),
        name="proj_ln",
    )(a1, a2, w, w, x, g, b)


def _cross_ln_kernel(x_ref, wq_ref, kv_ref, wo_ref, g_ref, b_ref, o_ref, ob_ref, *, alpha, n_heads):
    x = x_ref[...]
    q = _dot(x.astype(BF16), wq_ref[...])
    kv = kv_ref[0]
    cw = n_heads * HEAD_DIM
    outs = []
    for h in range(n_heads):
        qh = q[:, h * HEAD_DIM:(h + 1) * HEAD_DIM].astype(BF16)
        kh = kv[:, h * HEAD_DIM:(h + 1) * HEAD_DIM]
        vh = kv[:, cw + h * HEAD_DIM:cw + (h + 1) * HEAD_DIM]
        logits = _dot_nt(qh, kh) * ATT_SCALE
        mx = jnp.max(logits, axis=-1, keepdims=True)
        e = jnp.exp(logits - mx)
        p = e / jnp.sum(e, axis=-1, keepdims=True)
        outs.append(_dot(p.astype(BF16), vh))
    o = jnp.concatenate(outs, axis=-1).astype(BF16)
    y = alpha * x + _dot(o, wo_ref[...])
    out = _layer_norm_rows(y, g_ref[...], b_ref[...])
    o_ref[...] = out
    ob_ref[...] = out.astype(BF16)


def cross_ln(x, wq, kv, wo, g, b, *, alpha, tm, rows_per_batch):
    m, d = x.shape
    cw = wq.shape[1]
    nmem = kv.shape[1]
    steps_per_batch = rows_per_batch // tm
    return pl.pallas_call(
        functools.partial(_cross_ln_kernel, alpha=alpha, n_heads=cw // HEAD_DIM),
        out_shape=(jax.ShapeDtypeStruct((m, d), F32), jax.ShapeDtypeStruct((m, d), BF16)),
        grid=(m // tm,),
        in_specs=[
            pl.BlockSpec((tm, d), lambda i: (i, 0)),
            _single((d, cw), lambda i: (0, 0)),
            pl.BlockSpec((1, nmem, 2 * cw), lambda i: (i // steps_per_batch, 0, 0)),
            _single((cw, d), lambda i: (0, 0)),
            pl.BlockSpec((1, d), lambda i: (0, 0)),
            pl.BlockSpec((1, d), lambda i: (0, 0)),
        ],
        out_specs=(pl.BlockSpec((tm, d), lambda i: (i, 0)), pl.BlockSpec((tm, d), lambda i: (i, 0))),
        compiler_params=_cparams(("parallel",)),
        name="cross_ln",
    )(x, wq, kv, wo, g, b)


def _split3(x, order):
    hi = x.astype(BF16).astype(F32)
    lo = x - hi
    parts = [hi if c == "h" else lo for c in order]
    parts.append(jnp.zeros((x.shape[0], IDX_K - 3 * x.shape[1]), F32))
    return jnp.concatenate(parts, axis=-1).astype(BF16)


def _dsa_prep_kernel(kv_ref, qi_ref, sm_ref, kvg_ref, kng_ref, knb_ref,
                     ckv_ref, ckvt_ref, k3_ref, qi3_ref, wt_ref):
    kv = kv_ref[...]
    ms = jnp.mean(kv * kv, axis=-1, keepdims=True)
    ckv = kv * lax.rsqrt(ms + LN_EPS) * kvg_ref[...]
    ckv_ref[...] = ckv.astype(BF16)
    ckvt_ref[0] = jnp.transpose(ckv).astype(BF16)

    sm = sm_ref[...]
    kidx = _layer_norm_rows(sm[:, :IDX_DIM], kng_ref[...], knb_ref[...])
    k3_ref[...] = _split3(kidx, "hlh")

    wt = jnp.transpose(sm)
    wt_ref[...] = wt[IDX_DIM:IDX_DIM + N_IDX_HEADS, :] * (IDX_HEAD_SCALE * IDX_SCALE)

    qi = qi_ref[...]
    for h in range(N_IDX_HEADS):
        qi3_ref[h] = _split3(qi[:, h * IDX_DIM:(h + 1) * IDX_DIM], "hhl")


def dsa_prep(proj, kv_g, kn_g, kn_b, *, seq):
    m = proj.shape[0]
    tq = Q_BLOCK
    nq = m // tq
    nqb = seq // tq
    return pl.pallas_call(
        _dsa_prep_kernel,
        out_shape=(
            jax.ShapeDtypeStruct((m, KV_LORA_RANK), BF16),
            jax.ShapeDtypeStruct((m // seq, KV_LORA_RANK, seq), BF16),
            jax.ShapeDtypeStruct((m, IDX_K), BF16),
            jax.ShapeDtypeStruct((nq * N_IDX_HEADS, tq, IDX_K), BF16),
            jax.ShapeDtypeStruct((nq * N_IDX_HEADS, tq), F32),
        ),
        grid=(nq,),
        in_specs=[
            pl.BlockSpec((tq, KV_LORA_RANK), lambda i: (i, COL_KV // KV_LORA_RANK)),
            pl.BlockSpec((tq, IDX_WIDTH), lambda i: (i, COL_QIDX // IDX_WIDTH)),
            pl.BlockSpec((tq, LANE), lambda i: (i, COL_SMALL // LANE)),
            pl.BlockSpec((1, KV_LORA_RANK), lambda i: (0, 0)),
            pl.BlockSpec((1, IDX_DIM), lambda i: (0, 0)),
            pl.BlockSpec((1, IDX_DIM), lambda i: (0, 0)),
        ],
        out_specs=(
            pl.BlockSpec((tq, KV_LORA_RANK), lambda i: (i, 0)),
            pl.BlockSpec((1, KV_LORA_RANK, tq), lambda i: (i // nqb, 0, i % nqb)),
            pl.BlockSpec((tq, IDX_K), lambda i: (i, 0)),
            pl.BlockSpec((N_IDX_HEADS, tq, IDX_K), lambda i: (i, 0, 0)),
            pl.BlockSpec((N_IDX_HEADS, tq), lambda i: (i, 0)),
        ),
        compiler_params=_cparams(("parallel",)),
        name="dsa_prep",
    )(proj, proj, proj, kv_g, kn_g, kn_b)


def _bucket_thresholds():
    max_exact = N_BUCKETS // 2
    rel = np.arange(max_exact, 4 * MAX_DISTANCE, dtype=np.float32)
    large = max_exact + (np.log(rel / np.float32(max_exact)) / np.float32(math.log(MAX_DISTANCE / max_exact))
                         * np.float32(N_BUCKETS - max_exact)).astype(np.int32)
    large = np.minimum(large, N_BUCKETS - 1)
    return [int(rel[np.argmax(large >= bkt)]) for bkt in range(max_exact + 1, N_BUCKETS)]


def _bias_table_kernel(rb_ref, o_ref):
    di = pl.program_id(0)
    h = pl.program_id(1)
    max_exact = N_BUCKETS // 2
    sl = lax.broadcasted_iota(I32, (KEY_GROUP, Q_BLOCK), 0)
    tl = lax.broadcasted_iota(I32, (KEY_GROUP, Q_BLOCK), 1)
    rel = jnp.maximum(di * Q_BLOCK + tl - sl, 0)
    large = jnp.full(rel.shape, max_exact, I32)
    for th in _bucket_thresholds():
        large = large + (rel >= th).astype(I32)
    bucket = jnp.where(rel < max_exact, rel, large)
    far = rb_ref[N_BUCKETS - 1, h]
    val = jnp.zeros(rel.shape, F32)
    for bkt in range(N_BUCKETS - 1):
        val = jnp.where(bucket == bkt, rb_ref[bkt, h] - far, val)
    o_ref[0] = val


def bias_tables(rel_bias):
    nh = rel_bias.shape[1]
    return pl.pallas_call(
        _bias_table_kernel,
        out_shape=jax.ShapeDtypeStruct((3, KEY_GROUP, nh * Q_BLOCK), F32),
        grid=(3, nh),
        in_specs=[pl.BlockSpec(memory_space=pltpu.SMEM)],
        out_specs=pl.BlockSpec((1, KEY_GROUP, Q_BLOCK), lambda d, h: (d, 0, h)),
        compiler_params=_cparams(("arbitrary", "arbitrary")),
        name="bias_tables",
    )(rel_bias)


def _dsa_kernel(q_ref, ckv_ref, ckvt_ref, k3_ref, qi3_ref, wt_ref, wuk_ref, wuv_ref, bias_ref, o_ref,
                keys_ref, qabs_ref, acc_ref, *, topk):
    j = pl.program_id(1)
    nh = N_ATT_HEADS
    tq = Q_BLOCK
    t_lane = j * tq + lax.broadcasted_iota(I32, (1, tq), 1)

    n_sgrp = (j + 1 + (SCORE_GROUP // tq - 1)) // (SCORE_GROUP // tq)
    qi3 = qi3_ref[...].reshape(N_IDX_HEADS * tq, IDX_K)

    def score_body(g, carry):
        r0 = pl.multiple_of(g * SCORE_GROUP, SCORE_GROUP)
        lt = _dot_nt(k3_ref[pl.ds(r0, SCORE_GROUP), :], qi3)
        sc = jnp.zeros((SCORE_GROUP, tq), F32)
        for h in range(N_IDX_HEADS):
            sc = sc + jnp.maximum(lt[:, h * tq:(h + 1) * tq], 0.0) * wt_ref[h:h + 1, :]
        s_row = r0 + lax.broadcasted_iota(I32, (SCORE_GROUP, 1), 0)
        sc = jnp.where(s_row <= t_lane, sc + 0.0, -jnp.inf)
        bits = pltpu.bitcast(sc, I32)
        keys_ref[pl.ds(r0, SCORE_GROUP), :] = jnp.where(bits < 0, bits ^ 0x7FFFFFFF, bits)
        return carry

    lax.fori_loop(0, n_sgrp, score_body, 0)

    def count_ge(cand):
        def body(g, part):
            r0 = pl.multiple_of(g * SCORE_GROUP, SCORE_GROUP)
            ind = jnp.where(keys_ref[pl.ds(r0, SCORE_GROUP), :] >= cand, 1.0, 0.0)
            return part + jnp.sum(ind.reshape(SCORE_GROUP // COUNT_ROWS, COUNT_ROWS, tq), axis=0)
        part = lax.fori_loop(0, n_sgrp, body, jnp.zeros((COUNT_ROWS, tq), F32))
        return jnp.sum(part, axis=0, keepdims=True)

    thr0 = jnp.where(count_ge(jnp.zeros((1, tq), I32)) >= topk, 0, INT_MIN).astype(I32)

    def bit_body(i, thr):
        cand = thr | jnp.left_shift(jnp.int32(1), 30 - i)
        return jnp.where(count_ge(cand) >= topk, cand, thr)

    thr = lax.fori_loop(0, 31, bit_body, thr0)

    for h in range(nh):
        qh = q_ref[:, h * HEAD_DIM:(h + 1) * HEAD_DIM].astype(BF16)
        qabs_ref[:, h * tq:(h + 1) * tq] = (_dot_nt(wuk_ref[h], qh) * ATT_SCALE).astype(BF16)

    acc_ref[...] = jnp.zeros(acc_ref.shape, F32)

    def attend(g, carry, bias_di):
        m, l = carry
        c0 = pl.multiple_of(g * KEY_GROUP, KEY_GROUP)
        s_all = _dot(ckv_ref[pl.ds(c0, KEY_GROUP), :], qabs_ref[...])
        s_row = c0 + lax.broadcasted_iota(I32, (KEY_GROUP, 1), 0)
        sel = (keys_ref[pl.ds(c0, KEY_GROUP), :] >= thr) & (s_row <= t_lane)
        addm = jnp.where(sel, 0.0, NEG_BIG)
        ps, ms, ls, corrs = [], [], [], []
        for h in range(nh):
            lanes = slice(h * tq, (h + 1) * tq)
            s = s_all[:, lanes] + addm
            if bias_di is not None:
                s = s + bias_ref[bias_di, :, lanes]
            m_prev = m[:, lanes]
            m_new = jnp.maximum(m_prev, jnp.max(s, axis=0, keepdims=True))
            p = jnp.exp(s - m_new)
            corr = jnp.exp(m_prev - m_new)
            ls.append(corr * l[:, lanes] + jnp.sum(p, axis=0, keepdims=True))
            ms.append(m_new)
            corrs.append(corr)
            ps.append(p.astype(BF16))
        pt = jnp.concatenate(ps, axis=1)
        corr_all = jnp.concatenate(corrs, axis=1)
        acc_ref[...] = acc_ref[...] * corr_all + _dot(ckvt_ref[0, :, pl.ds(c0, KEY_GROUP)], pt)
        return jnp.concatenate(ms, axis=1), jnp.concatenate(ls, axis=1)

    carry = (jnp.full((1, nh * tq), NEG_BIG, F32), jnp.zeros((1, nh * tq), F32))
    n_far = jnp.maximum(j // 2 - 1 + j % 2, 0)
    carry = lax.fori_loop(0, n_far, lambda g, cr: attend(g, cr, None), carry)
    carry = lax.cond((j % 2 == 0) & (j >= 2), lambda cr: attend(j // 2 - 1, cr, 2), lambda cr: cr, carry)
    _, l = attend(j // 2, carry, j % 2)

    o_lat = (acc_ref[...] / l).astype(BF16)
    for h in range(nh):
        o_ref[:, h * HEAD_DIM:(h + 1) * HEAD_DIM] = _dot_tn(
            o_lat[:, h * tq:(h + 1) * tq], wuv_ref[h]).astype(o_ref.dtype)


def dsa_attention(proj, ckv, ckvt, k3, qi3, wt, wuk, wuv, bias, *, batch, seq, topk):
    tq = Q_BLOCK
    nq = seq // tq
    nh = N_ATT_HEADS
    r = KV_LORA_RANK
    seq_pad = -(-seq // SCORE_GROUP) * SCORE_GROUP
    return pl.pallas_call(
        functools.partial(_dsa_kernel, topk=topk),
        out_shape=jax.ShapeDtypeStruct((batch * seq, ATT_WIDTH), BF16),
        grid=(batch, nq),
        in_specs=[
            pl.BlockSpec((tq, ATT_WIDTH), lambda b, j: (b * nq + j, 0)),
            _single((seq, r), lambda b, j: (b, 0)),
            _single((1, r, seq), lambda b, j: (b, 0, 0)),
            _single((seq, IDX_K), lambda b, j: (b, 0)),
            pl.BlockSpec((N_IDX_HEADS, tq, IDX_K), lambda b, j: (b * nq + j, 0, 0)),
            pl.BlockSpec((N_IDX_HEADS, tq), lambda b, j: (b * nq + j, 0)),
            _single((nh, r, HEAD_DIM), lambda b, j: (0, 0, 0)),
            _single((nh, r, HEAD_DIM), lambda b, j: (0, 0, 0)),
            _single((3, KEY_GROUP, nh * tq), lambda b, j: (0, 0, 0)),
        ],
        out_specs=pl.BlockSpec((tq, ATT_WIDTH), lambda b, j: (b * nq + j, 0)),
        scratch_shapes=[
            pltpu.VMEM((seq_pad, tq), I32),
            pltpu.VMEM((r, nh * tq), BF16),
            pltpu.VMEM((r, nh * tq), F32),
        ],
        compiler_params=_cparams(("parallel", "arbitrary")),
        name="dsa",
    )(proj, ckv, ckvt, k3, qi3, wt, wuk, wuv, bias)


def _cumsum_rows(x):
    n = x.shape[0]
    row = lax.broadcasted_iota(I32, (n, 1), 0)
    sh = 1
    while sh < n:
        x = x + jnp.where(row >= sh, pltpu.roll(x, sh, 0), 0.0)
        sh *= 2
    return x


def _hgrn2_kernel(q_ref, f_ref, i_ref, g_ref, lb_ref, ng_ref, o_ref, st_ref, *, layer):
    c = REC_CHUNK
    dk = HEAD_DIM

    @pl.when(pl.program_id(2) == 0)
    def _():
        st_ref[...] = jnp.zeros_like(st_ref)

    lbr = lb_ref[...]
    e = jnp.exp(lbr - jnp.max(lbr, axis=0, keepdims=True))
    pr = e / jnp.sum(e, axis=0, keepdims=True)
    lb_all = jnp.zeros((1, lbr.shape[1]), F32)
    for i in range(1, layer + 1):
        lb_all = lb_all + pr[i:i + 1, :]

    row = lax.broadcasted_iota(I32, (c, 1), 0)
    col = lax.broadcasted_iota(I32, (1, c), 1)
    sub = 8
    nsub = c // sub
    tl3 = lax.broadcasted_iota(I32, (1, sub, 1), 1)
    g3 = lax.broadcasted_iota(I32, (nsub, 1, 1), 0)
    col3 = lax.broadcasted_iota(I32, (1, 1, c), 2)
    levels = []
    m = c // 2
    while m >= sub:
        levels.append(m)
        m //= 2

    def head_chunk(r0, hh):
        lanes = slice(hh * dk, (hh + 1) * dk)
        lb = lb_all[:, lanes]
        q = q_ref[pl.ds(r0, c), lanes]
        qf = q * jax.nn.sigmoid(q)
        f = lb + (1.0 - lb) * jax.nn.sigmoid(f_ref[pl.ds(r0, c), lanes])
        k = 1.0 - f
        v = i_ref[pl.ds(r0, c), lanes].astype(BF16)
        cum = _cumsum_rows(jnp.log(f))

        a = jnp.zeros((c, c), F32)
        for m in levels:
            first = (row & (2 * m - 1)) < m
            bnd = jnp.concatenate(
                [jnp.broadcast_to(cum[b * 2 * m + m - 1:b * 2 * m + m, :], (2 * m, dk))
                 for b in range(c // (2 * m))], axis=0)
            qs = jnp.where(first, 0.0, qf * jnp.exp(jnp.minimum(cum - bnd, 0.0))).astype(BF16)
            ks = jnp.where(first, k * jnp.exp(jnp.minimum(bnd - cum, 0.0)), 0.0).astype(BF16)
            same = (row & -(2 * m)) == (col & -(2 * m))
            a = a + jnp.where(same, _dot_nt(qs, ks), 0.0)
        cum3 = cum.reshape(nsub, sub, dk)
        k3 = k.reshape(nsub, sub, dk)
        qf3 = qf.reshape(nsub, sub, dk)
        a3 = jnp.zeros((nsub, sub, c), F32)
        for s in range(sub):
            d = jnp.where(tl3 >= s, cum3 - cum3[:, s:s + 1, :], -jnp.inf)
            prod = qf3 * k3[:, s:s + 1, :] * jnp.exp(d)
            colsum = jnp.sum(prod, axis=-1, keepdims=True)
            a3 = a3 + jnp.where(col3 == g3 * sub + s, colsum, 0.0)
        a = a + a3.reshape(c, c)

        st = st_ref[hh]
        o = _dot(a.astype(BF16), v) + _dot_nt((qf * jnp.exp(cum)).astype(BF16), st.astype(BF16))
        last = cum[c - 1:c, :]
        kl = (k * jnp.exp(last - cum)).astype(BF16)
        st_ref[hh] = st * jnp.exp(last) + _dot_tn(v, kl)

        ms = jnp.mean(o * o, axis=-1, keepdims=True)
        gt = g_ref[pl.ds(r0, c), lanes]
        on = o * lax.rsqrt(ms + LN_EPS) * ng_ref[:, lanes]
        o_ref[pl.ds(r0, c), lanes] = (on * (gt * jax.nn.sigmoid(gt))).astype(o_ref.dtype)

    def chunk_body(ci, carry):
        r0 = pl.multiple_of(ci * c, c)
        for hh in range(REC_HEADS_PER_STEP):
            head_chunk(r0, hh)
        return carry

    lax.fori_loop(0, q_ref.shape[0] // c, chunk_body, 0)


def hgrn2(proj, rec_lb, norm_g, *, layer, batch, seq):
    tb = min(REC_TBLOCK, seq)
    nt = seq // tb
    hw = REC_HEADS_PER_STEP * HEAD_DIM
    ng = N_REC_HEADS // REC_HEADS_PER_STEP
    cb = COL_REC // hw

    def seg(k):
        return pl.BlockSpec((tb, hw), lambda b, h, t, k=k: (b * nt + t, cb + k * ng + h))

    return pl.pallas_call(
        functools.partial(_hgrn2_kernel, layer=layer),
        out_shape=jax.ShapeDtypeStruct((batch * seq, REC_WIDTH), BF16),
        grid=(batch, ng, nt),
        in_specs=[
            seg(0), seg(1), seg(2), seg(3),
            pl.BlockSpec((rec_lb.shape[0], hw), lambda b, h, t: (0, h)),
            pl.BlockSpec((1, hw), lambda b, h, t: (0, h)),
        ],
        out_specs=pl.BlockSpec((tb, hw), lambda b, h, t: (b * nt + t, h)),
        scratch_shapes=[pltpu.VMEM((REC_HEADS_PER_STEP, HEAD_DIM, HEAD_DIM), F32)],
        compiler_params=_cparams(("parallel", "parallel", "arbitrary")),
        name="hgrn2",
    )(proj, proj, proj, proj, rec_lb, norm_g)


def _pack_w_in(w):
    kv0 = ATT_WIDTH
    qi0 = kv0 + KV_LORA_RANK
    sm0 = qi0 + IDX_WIDTH
    sm1 = sm0 + IDX_DIM + N_IDX_HEADS
    pad = jnp.zeros((w.shape[0], SMALL_WIDTH - IDX_DIM - N_IDX_HEADS), w.dtype)
    return jnp.concatenate([w[:, :kv0], w[:, qi0:sm0], w[:, kv0:qi0], w[:, sm0:sm1], pad, w[:, sm1:]], axis=1)


def _row_tile(m, want):
    t = min(want, m)
    while m % t:
        t //= 2
    return t


def kernel(x, mem, w_in, w_uk, w_uv, kv_norm_g, idx_kn_g, idx_kn_b, rec_lb, rec_norm_g, w_out, rel_bias,
           wq_c, wk_c, wv_c, wo_c, ffn_gate, ffn_up, ffn_down, ln_g, ln_b):
    batch, seq, d = x.shape
    depth = w_in.shape[0]
    m = batch * seq
    alpha = (2 * depth) ** 0.25
    topk = min(TOPK_MAX, seq // 4)
    d_ff = ffn_gate.shape[-1]
    tf = 256 if d_ff % 256 == 0 else 128

    xf = x.reshape(m, d)
    mem_b = mem.reshape(batch * mem.shape[1], d).astype(BF16)
    bias = bias_tables(rel_bias)
    rec_lb = rec_lb.astype(F32)
    wg_b, wu_b, wd_b = ffn_gate.astype(BF16), ffn_up.astype(BF16), ffn_down.astype(BF16)

    def ffn(xf, l, i):
        return ffn_ln(xf, wg_b, wu_b, wd_b, ln_g[l, 3 * i:3 * i + 1], ln_b[l, 3 * i:3 * i + 1],
                      layer=l, which=i, alpha=alpha, tm=_row_tile(m, 512), tf=tf)

    xb = None
    for l in range(depth):
        xf, xb = ffn(xf, l, 0)

        proj = matmul(xb, _pack_w_in(w_in[l]).astype(BF16), tm=_row_tile(m, 1024), tn=256, out_dtype=F32)
        ckv, ckvt, k3, qi3, wt = dsa_prep(proj, kv_norm_g[l][None], idx_kn_g[l][None], idx_kn_b[l][None], seq=seq)
        o_att = dsa_attention(proj, ckv, ckvt, k3, qi3, wt, w_uk[l].astype(BF16), w_uv[l].astype(BF16), bias,
                              batch=batch, seq=seq, topk=topk)
        o_rec = hgrn2(proj, rec_lb, rec_norm_g[l][None], layer=l, batch=batch, seq=seq)
        xf, xb = proj_ln(o_att, o_rec, w_out[l].astype(BF16), xf, ln_g[l, 1:2], ln_b[l, 1:2],
                         alpha=alpha, tm=_row_tile(m, 512), tn=512)

        kv = matmul(mem_b, jnp.concatenate([wk_c[l], wv_c[l]], axis=1).astype(BF16),
                    tm=_row_tile(mem_b.shape[0], 512), tn=256, out_dtype=BF16)
        kv = kv.reshape(batch, mem.shape[1], kv.shape[-1])
        xf, xb = cross_ln(xf, wq_c[l].astype(BF16), kv, wo_c[l].astype(BF16), ln_g[l, 2:3], ln_b[l, 2:3],
                          alpha=alpha, tm=_row_tile(seq, 256), rows_per_batch=seq)

        xf, xb = ffn(xf, l, 1)
    return xf.reshape(batch, seq, d)
```

```python
import functools
import math

import jax
import jax.numpy as jnp
import numpy as np
from jax import lax
from jax.experimental import pallas as pl
from jax.experimental.pallas import tpu as pltpu

F32 = jnp.float32
BF16 = jnp.bfloat16
I32 = jnp.int32

LANE = 128
VMEM_LIMIT_V7X = 60 * 1024 * 1024

HEAD_DIM = 128
N_ATT_HEADS = 16
N_REC_HEADS = 16
KV_LORA_RANK = 512
N_IDX_HEADS = 16
IDX_DIM = 64
TOPK_MAX = 256
Q_BLOCK = 128
N_BUCKETS = 32
MAX_DISTANCE = 128
N_CROSS_HEADS = 4
LN_EPS = 1e-5
ATT_SCALE = HEAD_DIM ** -0.5
IDX_SCALE = IDX_DIM ** -0.5
IDX_HEAD_SCALE = N_IDX_HEADS ** -0.5

ATT_WIDTH = N_ATT_HEADS * HEAD_DIM
IDX_WIDTH = N_IDX_HEADS * IDX_DIM
REC_WIDTH = N_REC_HEADS * HEAD_DIM
COL_QIDX = ATT_WIDTH
COL_KV = COL_QIDX + IDX_WIDTH
COL_SMALL = COL_KV + KV_LORA_RANK
SMALL_WIDTH = 256
COL_REC = COL_SMALL + SMALL_WIDTH

NEG_BIG = -1e30
INT_MIN = -2 ** 31

REC_CHUNK = 64
REC_TBLOCK = 512
REC_HEADS_PER_STEP = 4
KEY_GROUP = 256
SCORE_GROUP = 512
IDX_K = 256
COUNT_ROWS = 64
LN_ROWS = 128


def _cparams(sem):
    return pltpu.CompilerParams(dimension_semantics=sem, vmem_limit_bytes=VMEM_LIMIT_V7X)


def _single(shape, imap):
    return pl.BlockSpec(shape, imap, pipeline_mode=pl.Buffered(1))


def _layer_norm_rows(y, g, b):
    mu = jnp.mean(y, axis=-1, keepdims=True)
    d = y - mu
    var = jnp.mean(d * d, axis=-1, keepdims=True)
    return d * lax.rsqrt(var + LN_EPS) * g + b


def _residual_ln_inplace(x_ref, o_ref, g_ref, b_ref, *, alpha, scale):
    g = g_ref[...]
    b = b_ref[...]

    def body(i, carry):
        rows = pl.ds(pl.multiple_of(i * LN_ROWS, LN_ROWS), LN_ROWS)
        y = alpha * x_ref[rows, :] + scale * o_ref[rows, :]
        o_ref[rows, :] = _layer_norm_rows(y, g, b)
        return carry

    lax.fori_loop(0, o_ref.shape[0] // LN_ROWS, body, 0)


def _dot(a, b):
    return jnp.dot(a, b, preferred_element_type=F32)


def _dot_nt(a, b):
    return lax.dot_general(a, b, (((1,), (1,)), ((), ())), preferred_element_type=F32)


def _dot_tn(a, b):
    return lax.dot_general(a, b, (((0,), (0,)), ((), ())), preferred_element_type=F32)


def _ffn_ln_kernel(x_ref, wg_ref, wu_ref, wd_ref, g_ref, b_ref, o_ref, xb_ref, *, alpha):
    f = pl.program_id(1)

    @pl.when(f == 0)
    def _():
        xb_ref[...] = x_ref[...].astype(BF16)
        o_ref[...] = jnp.zeros_like(o_ref)

    xb = xb_ref[...]
    h = _dot(xb, wg_ref[0, 0])
    u = _dot(xb, wu_ref[0, 0])
    a = (h * jax.nn.sigmoid(h) * u).astype(BF16)
    o_ref[...] += _dot(a, wd_ref[0, 0])

    @pl.when(f == pl.num_programs(1) - 1)
    def _():
        _residual_ln_inplace(x_ref, o_ref, g_ref, b_ref, alpha=alpha, scale=0.5)


def ffn_ln(x, wg, wu, wd, g, b, *, layer, which, alpha, tm, tf):
    m, d = x.shape
    nf = wg.shape[-1] // tf
    return pl.pallas_call(
        functools.partial(_ffn_ln_kernel, alpha=alpha),
        out_shape=jax.ShapeDtypeStruct((m, d), F32),
        grid=(m // tm, nf),
        in_specs=[
            pl.BlockSpec((tm, d), lambda i, f: (i, 0)),
            pl.BlockSpec((1, 1, d, tf), lambda i, f: (layer, which, 0, f)),
            pl.BlockSpec((1, 1, d, tf), lambda i, f: (layer, which, 0, f)),
            pl.BlockSpec((1, 1, tf, d), lambda i, f: (layer, which, f, 0)),
            pl.BlockSpec((1, d), lambda i, f: (0, 0)),
            pl.BlockSpec((1, d), lambda i, f: (0, 0)),
        ],
        out_specs=pl.BlockSpec((tm, d), lambda i, f: (i, 0)),
        scratch_shapes=[pltpu.VMEM((tm, d), BF16)],
        compiler_params=_cparams(("parallel", "arbitrary")),
        name="ffn_ln",
    )(x, wg, wu, wd, g, b)


def _matmul_kernel(x_ref, w_ref, o_ref, xb_ref):
    @pl.when(pl.program_id(1) == 0)
    def _():
        xb_ref[...] = x_ref[...].astype(BF16)

    o_ref[...] = _dot(xb_ref[...], w_ref[...]).astype(o_ref.dtype)


def matmul(x, w, *, tm, tn, out_dtype):
    m, k = x.shape
    n = w.shape[1]
    return pl.pallas_call(
        _matmul_kernel,
        out_shape=jax.ShapeDtypeStruct((m, n), out_dtype),
        grid=(m // tm, n // tn),
        in_specs=[
            pl.BlockSpec((tm, k), lambda i, j: (i, 0)),
            pl.BlockSpec((k, tn), lambda i, j: (0, j)),
        ],
        out_specs=pl.BlockSpec((tm, tn), lambda i, j: (i, j)),
        scratch_shapes=[pltpu.VMEM((tm, k), BF16)],
        compiler_params=_cparams(("parallel", "arbitrary")),
        name="matmul",
    )(x, w)


def _proj_ln_kernel(a1_ref, a2_ref, w1_ref, w2_ref, x_ref, g_ref, b_ref, o_ref, *, alpha, tn):
    n = pl.program_id(1)
    c0 = pl.multiple_of(n * tn, tn)
    o_ref[:, pl.ds(c0, tn)] = _dot(a1_ref[...], w1_ref[...]) + _dot(a2_ref[...], w2_ref[...])

    @pl.when(n == pl.num_programs(1) - 1)
    def _():
        _residual_ln_inplace(x_ref, o_ref, g_ref, b_ref, alpha=alpha, scale=1.0)


def proj_ln(a1, a2, w, x, g, b, *, alpha, tm, tn):
    m, kh = a1.shape
    d = w.shape[1]
    return pl.pallas_call(
        functools.partial(_proj_ln_kernel, alpha=alpha, tn=tn),
        out_shape=jax.ShapeDtypeStruct((m, d), F32),
        grid=(m // tm, d // tn),
        in_specs=[
            pl.BlockSpec((tm, kh), lambda i, n: (i, 0)),
            pl.BlockSpec((tm, kh), lambda i, n: (i, 0)),
            pl.BlockSpec((kh, tn), lambda i, n: (0, n)),
            pl.BlockSpec((kh, tn), lambda i, n: (1, n)),
            pl.BlockSpec((tm, d), lambda i, n: (i, 0)),
            pl.BlockSpec((1, d), lambda i, n: (0, 0)),
            pl.BlockSpec((1, d), lambda i, n: (0, 0)),
        ],
        out_specs=pl.BlockSpec((tm, d), lambda i, n: (i, 0)),
        compiler_params=_cparams(("parallel", "arbitrary")),
        name="proj_ln",
    )(a1, a2, w, w, x, g, b)


def _cross_ln_kernel(x_ref, wq_ref, kv_ref, wo_ref, g_ref, b_ref, o_ref, *, alpha, n_heads):
    x = x_ref[...]
    q = _dot(x.astype(BF16), wq_ref[...])
    kv = kv_ref[0]
    cw = n_heads * HEAD_DIM
    outs = []
    for h in range(n_heads):
        qh = q[:, h * HEAD_DIM:(h + 1) * HEAD_DIM].astype(BF16)
        kh = kv[:, h * HEAD_DIM:(h + 1) * HEAD_DIM]
        vh = kv[:, cw + h * HEAD_DIM:cw + (h + 1) * HEAD_DIM]
        logits = _dot_nt(qh, kh) * ATT_SCALE
        mx = jnp.max(logits, axis=-1, keepdims=True)
        e = jnp.exp(logits - mx)
        p = e / jnp.sum(e, axis=-1, keepdims=True)
        outs.append(_dot(p.astype(BF16), vh))
    o = jnp.concatenate(outs, axis=-1).astype(BF16)
    o_ref[...] = _dot(o, wo_ref[...])
    _residual_ln_inplace(x_ref, o_ref, g_ref, b_ref, alpha=alpha, scale=1.0)


def cross_ln(x, wq, kv, wo, g, b, *, alpha, tm, rows_per_batch):
    m, d = x.shape
    cw = wq.shape[1]
    nmem = kv.shape[1]
    steps_per_batch = rows_per_batch // tm
    return pl.pallas_call(
        functools.partial(_cross_ln_kernel, alpha=alpha, n_heads=cw // HEAD_DIM),
        out_shape=jax.ShapeDtypeStruct((m, d), F32),
        grid=(m // tm,),
        in_specs=[
            pl.BlockSpec((tm, d), lambda i: (i, 0)),
            _single((d, cw), lambda i: (0, 0)),
            pl.BlockSpec((1, nmem, 2 * cw), lambda i: (i // steps_per_batch, 0, 0)),
            _single((cw, d), lambda i: (0, 0)),
            pl.BlockSpec((1, d), lambda i: (0, 0)),
            pl.BlockSpec((1, d), lambda i: (0, 0)),
        ],
        out_specs=pl.BlockSpec((tm, d), lambda i: (i, 0)),
        compiler_params=_cparams(("parallel",)),
        name="cross_ln",
    )(x, wq, kv, wo, g, b)


def _split3(x, order):
    hi = x.astype(BF16).astype(F32)
    lo = x - hi
    parts = [hi if c == "h" else lo for c in order]
    parts.append(jnp.zeros((x.shape[0], IDX_K - 3 * x.shape[1]), F32))
    return jnp.concatenate(parts, axis=-1).astype(BF16)


def _dsa_prep_kernel(kv_ref, qi_ref, sm_ref, kvg_ref, kng_ref, knb_ref,
                     ckv_ref, ckvt_ref, k3_ref, qi3_ref, wt_ref):
    kv = kv_ref[...]
    ms = jnp.mean(kv * kv, axis=-1, keepdims=True)
    ckv = kv * lax.rsqrt(ms + LN_EPS) * kvg_ref[...]
    ckv_ref[...] = ckv.astype(BF16)
    ckvt_ref[0] = jnp.transpose(ckv).astype(BF16)

    sm = sm_ref[...]
    kidx = _layer_norm_rows(sm[:, :IDX_DIM], kng_ref[...], knb_ref[...])
    k3_ref[...] = _split3(kidx, "hlh")

    wt = jnp.transpose(sm)
    wt_ref[...] = wt[IDX_DIM:IDX_DIM + N_IDX_HEADS, :] * (IDX_HEAD_SCALE * IDX_SCALE)

    qi = qi_ref[...]
    for h in range(N_IDX_HEADS):
        qi3_ref[h] = _split3(qi[:, h * IDX_DIM:(h + 1) * IDX_DIM], "hhl")


def dsa_prep(proj, kv_g, kn_g, kn_b, *, seq):
    m = proj.shape[0]
    tq = Q_BLOCK
    nq = m // tq
    nqb = seq // tq
    return pl.pallas_call(
        _dsa_prep_kernel,
        out_shape=(
            jax.ShapeDtypeStruct((m, KV_LORA_RANK), BF16),
            jax.ShapeDtypeStruct((m // seq, KV_LORA_RANK, seq), BF16),
            jax.ShapeDtypeStruct((m, IDX_K), BF16),
            jax.ShapeDtypeStruct((nq * N_IDX_HEADS, tq, IDX_K), BF16),
            jax.ShapeDtypeStruct((nq * N_IDX_HEADS, tq), F32),
        ),
        grid=(nq,),
        in_specs=[
            pl.BlockSpec((tq, KV_LORA_RANK), lambda i: (i, COL_KV // KV_LORA_RANK)),
            pl.BlockSpec((tq, IDX_WIDTH), lambda i: (i, COL_QIDX // IDX_WIDTH)),
            pl.BlockSpec((tq, LANE), lambda i: (i, COL_SMALL // LANE)),
            pl.BlockSpec((1, KV_LORA_RANK), lambda i: (0, 0)),
            pl.BlockSpec((1, IDX_DIM), lambda i: (0, 0)),
            pl.BlockSpec((1, IDX_DIM), lambda i: (0, 0)),
        ],
        out_specs=(
            pl.BlockSpec((tq, KV_LORA_RANK), lambda i: (i, 0)),
            pl.BlockSpec((1, KV_LORA_RANK, tq), lambda i: (i // nqb, 0, i % nqb)),
            pl.BlockSpec((tq, IDX_K), lambda i: (i, 0)),
            pl.BlockSpec((N_IDX_HEADS, tq, IDX_K), lambda i: (i, 0, 0)),
            pl.BlockSpec((N_IDX_HEADS, tq), lambda i: (i, 0)),
        ),
        compiler_params=_cparams(("parallel",)),
        name="dsa_prep",
    )(proj, proj, proj, kv_g, kn_g, kn_b)


def _bucket_thresholds():
    max_exact = N_BUCKETS // 2
    rel = np.arange(max_exact, 4 * MAX_DISTANCE, dtype=np.float32)
    large = max_exact + (np.log(rel / np.float32(max_exact)) / np.float32(math.log(MAX_DISTANCE / max_exact))
                         * np.float32(N_BUCKETS - max_exact)).astype(np.int32)
    large = np.minimum(large, N_BUCKETS - 1)
    return [int(rel[np.argmax(large >= bkt)]) for bkt in range(max_exact + 1, N_BUCKETS)]


def _bias_table_kernel(rb_ref, o_ref):
    di = pl.program_id(0)
    h = pl.program_id(1)
    max_exact = N_BUCKETS // 2
    sl = lax.broadcasted_iota(I32, (KEY_GROUP, Q_BLOCK), 0)
    tl = lax.broadcasted_iota(I32, (KEY_GROUP, Q_BLOCK), 1)
    rel = jnp.maximum(di * Q_BLOCK + tl - sl, 0)
    large = jnp.full(rel.shape, max_exact, I32)
    for th in _bucket_thresholds():
        large = large + (rel >= th).astype(I32)
    bucket = jnp.where(rel < max_exact, rel, large)
    far = rb_ref[N_BUCKETS - 1, h]
    val = jnp.zeros(rel.shape, F32)
    for bkt in range(N_BUCKETS - 1):
        val = jnp.where(bucket == bkt, rb_ref[bkt, h] - far, val)
    o_ref[0] = val


def bias_tables(rel_bias):
    nh = rel_bias.shape[1]
    return pl.pallas_call(
        _bias_table_kernel,
        out_shape=jax.ShapeDtypeStruct((3, KEY_GROUP, nh * Q_BLOCK), F32),
        grid=(3, nh),
        in_specs=[pl.BlockSpec(memory_space=pltpu.SMEM)],
        out_specs=pl.BlockSpec((1, KEY_GROUP, Q_BLOCK), lambda d, h: (d, 0, h)),
        compiler_params=_cparams(("arbitrary", "arbitrary")),
        name="bias_tables",
    )(rel_bias)


def _dsa_kernel(q_ref, ckv_ref, ckvt_ref, k3_ref, qi3_ref, wt_ref, wuk_ref, wuv_ref, bias_ref, o_ref,
                keys_ref, qabs_ref, acc_ref, *, topk):
    j = pl.program_id(1)
    nh = N_ATT_HEADS
    tq = Q_BLOCK
    t_lane = j * tq + lax.broadcasted_iota(I32, (1, tq), 1)

    n_sgrp = (j + 1 + (SCORE_GROUP // tq - 1)) // (SCORE_GROUP // tq)
    qi3 = qi3_ref[...].reshape(N_IDX_HEADS * tq, IDX_K)

    def score_body(g, carry):
        r0 = pl.multiple_of(g * SCORE_GROUP, SCORE_GROUP)
        lt = _dot_nt(k3_ref[pl.ds(r0, SCORE_GROUP), :], qi3)
        sc = jnp.zeros((SCORE_GROUP, tq), F32)
        for h in range(N_IDX_HEADS):
            sc = sc + jnp.maximum(lt[:, h * tq:(h + 1) * tq], 0.0) * wt_ref[h:h + 1, :]
        s_row = r0 + lax.broadcasted_iota(I32, (SCORE_GROUP, 1), 0)
        sc = jnp.where(s_row <= t_lane, sc + 0.0, -jnp.inf)
        bits = pltpu.bitcast(sc, I32)
        keys_ref[pl.ds(r0, SCORE_GROUP), :] = jnp.where(bits < 0, bits ^ 0x7FFFFFFF, bits)
        return carry

    lax.fori_loop(0, n_sgrp, score_body, 0)

    def count_ge(cand):
        def body(g, part):
            r0 = pl.multiple_of(g * SCORE_GROUP, SCORE_GROUP)
            ind = jnp.where(keys_ref[pl.ds(r0, SCORE_GROUP), :] >= cand, 1.0, 0.0)
            return part + jnp.sum(ind.reshape(SCORE_GROUP // COUNT_ROWS, COUNT_ROWS, tq), axis=0)
        part = lax.fori_loop(0, n_sgrp, body, jnp.zeros((COUNT_ROWS, tq), F32))
        return jnp.sum(part, axis=0, keepdims=True)

    thr0 = jnp.where(count_ge(jnp.zeros((1, tq), I32)) >= topk, 0, INT_MIN).astype(I32)

    def bit_body(i, thr):
        cand = thr | jnp.left_shift(jnp.int32(1), 30 - i)
        return jnp.where(count_ge(cand) >= topk, cand, thr)

    thr = lax.fori_loop(0, 31, bit_body, thr0)

    for h in range(nh):
        qh = q_ref[:, h * HEAD_DIM:(h + 1) * HEAD_DIM].astype(BF16)
        qabs_ref[:, h * tq:(h + 1) * tq] = (_dot_nt(wuk_ref[h], qh) * ATT_SCALE).astype(BF16)

    acc_ref[...] = jnp.zeros(acc_ref.shape, F32)

    def attend(g, carry, bias_di):
        m, l = carry
        c0 = pl.multiple_of(g * KEY_GROUP, KEY_GROUP)
        s_all = _dot(ckv_ref[pl.ds(c0, KEY_GROUP), :], qabs_ref[...])
        s_row = c0 + lax.broadcasted_iota(I32, (KEY_GROUP, 1), 0)
        sel = (keys_ref[pl.ds(c0, KEY_GROUP), :] >= thr) & (s_row <= t_lane)
        addm = jnp.where(sel, 0.0, NEG_BIG)
        ps, ms, ls, corrs = [], [], [], []
        for h in range(nh):
            lanes = slice(h * tq, (h + 1) * tq)
            s = s_all[:, lanes] + addm
            if bias_di is not None:
                s = s + bias_ref[bias_di, :, lanes]
            m_prev = m[:, lanes]
            m_new = jnp.maximum(m_prev, jnp.max(s, axis=0, keepdims=True))
            p = jnp.exp(s - m_new)
            corr = jnp.exp(m_prev - m_new)
            ls.append(corr * l[:, lanes] + jnp.sum(p, axis=0, keepdims=True))
            ms.append(m_new)
            corrs.append(corr)
            ps.append(p.astype(BF16))
        pt = jnp.concatenate(ps, axis=1)
        corr_all = jnp.concatenate(corrs, axis=1)
        acc_ref[...] = acc_ref[...] * corr_all + _dot(ckvt_ref[0, :, pl.ds(c0, KEY_GROUP)], pt)
        return jnp.concatenate(ms, axis=1), jnp.concatenate(ls, axis=1)

    carry = (jnp.full((1, nh * tq), NEG_BIG, F32), jnp.zeros((1, nh * tq), F32))
    n_far = jnp.maximum(j // 2 - 1 + j % 2, 0)
    carry = lax.fori_loop(0, n_far, lambda g, cr: attend(g, cr, None), carry)
    carry = lax.cond((j % 2 == 0) & (j >= 2), lambda cr: attend(j // 2 - 1, cr, 2), lambda cr: cr, carry)
    _, l = attend(j // 2, carry, j % 2)

    o_lat = (acc_ref[...] / l).astype(BF16)
    for h in range(nh):
        o_ref[:, h * HEAD_DIM:(h + 1) * HEAD_DIM] = _dot_tn(
            o_lat[:, h * tq:(h + 1) * tq], wuv_ref[h]).astype(o_ref.dtype)


def dsa_attention(proj, ckv, ckvt, k3, qi3, wt, wuk, wuv, bias, *, batch, seq, topk):
    tq = Q_BLOCK
    nq = seq // tq
    nh = N_ATT_HEADS
    r = KV_LORA_RANK
    seq_pad = -(-seq // SCORE_GROUP) * SCORE_GROUP
    return pl.pallas_call(
        functools.partial(_dsa_kernel, topk=topk),
        out_shape=jax.ShapeDtypeStruct((batch * seq, ATT_WIDTH), BF16),
        grid=(batch, nq),
        in_specs=[
            pl.BlockSpec((tq, ATT_WIDTH), lambda b, j: (b * nq + j, 0)),
            _single((seq, r), lambda b, j: (b, 0)),
            _single((1, r, seq), lambda b, j: (b, 0, 0)),
            _single((seq, IDX_K), lambda b, j: (b, 0)),
            pl.BlockSpec((N_IDX_HEADS, tq, IDX_K), lambda b, j: (b * nq + j, 0, 0)),
            pl.BlockSpec((N_IDX_HEADS, tq), lambda b, j: (b * nq + j, 0)),
            _single((nh, r, HEAD_DIM), lambda b, j: (0, 0, 0)),
            _single((nh, r, HEAD_DIM), lambda b, j: (0, 0, 0)),
            _single((3, KEY_GROUP, nh * tq), lambda b, j: (0, 0, 0)),
        ],
        out_specs=pl.BlockSpec((tq, ATT_WIDTH), lambda b, j: (b * nq + j, 0)),
        scratch_shapes=[
            pltpu.VMEM((seq_pad, tq), I32),
            pltpu.VMEM((r, nh * tq), BF16),
            pltpu.VMEM((r, nh * tq), F32),
        ],
        compiler_params=_cparams(("parallel", "arbitrary")),
        name="dsa",
    )(proj, ckv, ckvt, k3, qi3, wt, wuk, wuv, bias)


def _cumsum_rows(x):
    n = x.shape[0]
    row = lax.broadcasted_iota(I32, (n, 1), 0)
    sh = 1
    while sh < n:
        x = x + jnp.where(row >= sh, pltpu.roll(x, sh, 0), 0.0)
        sh *= 2
    return x


def _hgrn2_kernel(q_ref, f_ref, i_ref, g_ref, lb_ref, ng_ref, o_ref, st_ref, *, layer):
    c = REC_CHUNK
    dk = HEAD_DIM

    @pl.when(pl.program_id(2) == 0)
    def _():
        st_ref[...] = jnp.zeros_like(st_ref)

    lbr = lb_ref[...]
    e = jnp.exp(lbr - jnp.max(lbr, axis=0, keepdims=True))
    pr = e / jnp.sum(e, axis=0, keepdims=True)
    lb_all = jnp.zeros((1, lbr.shape[1]), F32)
    for i in range(1, layer + 1):
        lb_all = lb_all + pr[i:i + 1, :]

    row = lax.broadcasted_iota(I32, (c, 1), 0)
    col = lax.broadcasted_iota(I32, (1, c), 1)
    sub = 8
    nsub = c // sub
    tl3 = lax.broadcasted_iota(I32, (1, sub, 1), 1)
    g3 = lax.broadcasted_iota(I32, (nsub, 1, 1), 0)
    col3 = lax.broadcasted_iota(I32, (1, 1, c), 2)
    levels = []
    m = c // 2
    while m >= sub:
        levels.append(m)
        m //= 2

    def head_chunk(r0, hh):
        lanes = slice(hh * dk, (hh + 1) * dk)
        lb = lb_all[:, lanes]
        q = q_ref[pl.ds(r0, c), lanes]
        qf = q * jax.nn.sigmoid(q)
        f = lb + (1.0 - lb) * jax.nn.sigmoid(f_ref[pl.ds(r0, c), lanes])
        k = 1.0 - f
        v = i_ref[pl.ds(r0, c), lanes].astype(BF16)
        cum = _cumsum_rows(jnp.log(f))

        a = jnp.zeros((c, c), F32)
        for m in levels:
            first = (row & (2 * m - 1)) < m
            bnd = jnp.concatenate(
                [jnp.broadcast_to(cum[b * 2 * m + m - 1:b * 2 * m + m, :], (2 * m, dk))
                 for b in range(c // (2 * m))], axis=0)
            qs = jnp.where(first, 0.0, qf * jnp.exp(jnp.minimum(cum - bnd, 0.0))).astype(BF16)
            ks = jnp.where(first, k * jnp.exp(jnp.minimum(bnd - cum, 0.0)), 0.0).astype(BF16)
            same = (row & -(2 * m)) == (col & -(2 * m))
            a = a + jnp.where(same, _dot_nt(qs, ks), 0.0)
        cum3 = cum.reshape(nsub, sub, dk)
        k3 = k.reshape(nsub, sub, dk)
        qf3 = qf.reshape(nsub, sub, dk)
        a3 = jnp.zeros((nsub, sub, c), F32)
        for s in range(sub):
            d = jnp.where(tl3 >= s, cum3 - cum3[:, s:s + 1, :], -jnp.inf)
            prod = qf3 * k3[:, s:s + 1, :] * jnp.exp(d)
            colsum = jnp.sum(prod, axis=-1, keepdims=True)
            a3 = a3 + jnp.where(col3 == g3 * sub + s, colsum, 0.0)
        a = a + a3.reshape(c, c)

        st = st_ref[hh]
        o = _dot(a.astype(BF16), v) + _dot_nt((qf * jnp.exp(cum)).astype(BF16), st.astype(BF16))
        last = cum[c - 1:c, :]
        kl = (k * jnp.exp(last - cum)).astype(BF16)
        st_ref[hh] = st * jnp.exp(last) + _dot_tn(v, kl)

        ms = jnp.mean(o * o, axis=-1, keepdims=True)
        gt = g_ref[pl.ds(r0, c), lanes]
        on = o * lax.rsqrt(ms + LN_EPS) * ng_ref[:, lanes]
        o_ref[pl.ds(r0, c), lanes] = (on * (gt * jax.nn.sigmoid(gt))).astype(o_ref.dtype)

    def chunk_body(ci, carry):
        r0 = pl.multiple_of(ci * c, c)
        for hh in range(REC_HEADS_PER_STEP):
            head_chunk(r0, hh)
        return carry

    lax.fori_loop(0, q_ref.shape[0] // c, chunk_body, 0)


def hgrn2(proj, rec_lb, norm_g, *, layer, batch, seq):
    tb = min(REC_TBLOCK, seq)
    nt = seq // tb
    hw = REC_HEADS_PER_STEP * HEAD_DIM
    ng = N_REC_HEADS // REC_HEADS_PER_STEP

    def seg(k):
        return pl.BlockSpec((tb, hw), lambda b, h, t, k=k: (b * nt + t, k * ng + h))

    return pl.pallas_call(
        functools.partial(_hgrn2_kernel, layer=layer),
        out_shape=jax.ShapeDtypeStruct((batch * seq, REC_WIDTH), BF16),
        grid=(batch, ng, nt),
        in_specs=[
            seg(0), seg(1), seg(2), seg(3),
            pl.BlockSpec((rec_lb.shape[0], hw), lambda b, h, t: (0, h)),
            pl.BlockSpec((1, hw), lambda b, h, t: (0, h)),
        ],
        out_specs=pl.BlockSpec((tb, hw), lambda b, h, t: (b * nt + t, h)),
        scratch_shapes=[pltpu.VMEM((REC_HEADS_PER_STEP, HEAD_DIM, HEAD_DIM), F32)],
        compiler_params=_cparams(("parallel", "parallel", "arbitrary")),
        name="hgrn2",
    )(proj, proj, proj, proj, rec_lb, norm_g)


def _pack_w_in(w):
    kv0 = ATT_WIDTH
    qi0 = kv0 + KV_LORA_RANK
    sm0 = qi0 + IDX_WIDTH
    sm1 = sm0 + IDX_DIM + N_IDX_HEADS
    wb = w.astype(BF16)
    pad = jnp.zeros((w.shape[0], SMALL_WIDTH - IDX_DIM - N_IDX_HEADS), BF16)
    w_att = jnp.concatenate([wb[:, :kv0], wb[:, qi0:sm0], wb[:, kv0:qi0], wb[:, sm0:sm1], pad], axis=1)
    return w_att, wb[:, sm1:]


def _row_tile(m, want):
    t = min(want, m)
    while m % t:
        t //= 2
    return t


def kernel(x, mem, w_in, w_uk, w_uv, kv_norm_g, idx_kn_g, idx_kn_b, rec_lb, rec_norm_g, w_out, rel_bias,
           wq_c, wk_c, wv_c, wo_c, ffn_gate, ffn_up, ffn_down, ln_g, ln_b):
    batch, seq, d = x.shape
    depth = w_in.shape[0]
    m = batch * seq
    alpha = (2 * depth) ** 0.25
    topk = min(TOPK_MAX, seq // 4)
    d_ff = ffn_gate.shape[-1]
    tf = 256 if d_ff % 256 == 0 else 128

    xf = x.reshape(m, d)
    mem_f = mem.reshape(batch * mem.shape[1], d)
    bias = bias_tables(rel_bias)
    rec_lb = rec_lb.astype(F32)
    wg_b, wu_b, wd_b = ffn_gate.astype(BF16), ffn_up.astype(BF16), ffn_down.astype(BF16)

    def ffn(xf, l, i):
        return ffn_ln(xf, wg_b, wu_b, wd_b, ln_g[l, 3 * i:3 * i + 1], ln_b[l, 3 * i:3 * i + 1],
                      layer=l, which=i, alpha=alpha, tm=_row_tile(m, 512), tf=tf)

    for l in range(depth):
        xf = ffn(xf, l, 0)

        w_att, w_rec = _pack_w_in(w_in[l])
        proj_a = matmul(xf, w_att, tm=_row_tile(m, 1024), tn=256, out_dtype=F32)
        proj_r = matmul(xf, w_rec, tm=_row_tile(m, 1024), tn=256, out_dtype=F32)
        ckv, ckvt, k3, qi3, wt = dsa_prep(proj_a, kv_norm_g[l][None], idx_kn_g[l][None], idx_kn_b[l][None], seq=seq)
        o_att = dsa_attention(proj_a, ckv, ckvt, k3, qi3, wt, w_uk[l].astype(BF16), w_uv[l].astype(BF16), bias,
                              batch=batch, seq=seq, topk=topk)
        o_rec = hgrn2(proj_r, rec_lb, rec_norm_g[l][None], layer=l, batch=batch, seq=seq)
        xf = proj_ln(o_att, o_rec, w_out[l].astype(BF16), xf, ln_g[l, 1:2], ln_b[l, 1:2],
                     alpha=alpha, tm=_row_tile(m, 512), tn=512)

        kv = matmul(mem_f, jnp.concatenate([wk_c[l], wv_c[l]], axis=1).astype(BF16),
                    tm=_row_tile(mem_f.shape[0], 512), tn=256, out_dtype=BF16)
        kv = kv.reshape(batch, mem.shape[1], kv.shape[-1])
        xf = cross_ln(xf, wq_c[l].astype(BF16), kv, wo_c[l].astype(BF16), ln_g[l, 2:3], ln_b[l, 2:3],
                      alpha=alpha, tm=_row_tile(seq, 256), rows_per_batch=seq)

        xf = ffn(xf, l, 1)
    return xf.reshape(batch, seq, d)
```

```python
import functools
import math

import jax
import jax.numpy as jnp
import numpy as np
from jax import lax
from jax.experimental import pallas as pl
from jax.experimental.pallas import tpu as pltpu

F32 = jnp.float32
BF16 = jnp.bfloat16
I32 = jnp.int32

LANE = 128
VMEM_LIMIT_V7X = 60 * 1024 * 1024

HEAD_DIM = 128
N_ATT_HEADS = 16
N_REC_HEADS = 16
KV_LORA_RANK = 512
N_IDX_HEADS = 16
IDX_DIM = 64
TOPK_MAX = 256
Q_BLOCK = 128
N_BUCKETS = 32
MAX_DISTANCE = 128
N_CROSS_HEADS = 4
LN_EPS = 1e-5
ATT_SCALE = HEAD_DIM ** -0.5
IDX_SCALE = IDX_DIM ** -0.5
IDX_HEAD_SCALE = N_IDX_HEADS ** -0.5

ATT_WIDTH = N_ATT_HEADS * HEAD_DIM
IDX_WIDTH = N_IDX_HEADS * IDX_DIM
REC_WIDTH = N_REC_HEADS * HEAD_DIM
COL_QIDX = ATT_WIDTH
COL_KV = COL_QIDX + IDX_WIDTH
COL_SMALL = COL_KV + KV_LORA_RANK
SMALL_WIDTH = 256
COL_REC = COL_SMALL + SMALL_WIDTH

NEG_BIG = -1e30
INT_MIN = -2 ** 31
KEY_NEG_INF = (0xFF800000 - 2 ** 32) ^ 0x7FFFFFFF

REC_CHUNK = 64
REC_TBLOCK = 512
REC_HEADS_PER_STEP = 4
KEY_GROUP = 256
SCORE_GROUP = 512
IDX_K = 256
COUNT_ROWS = 64
LN_ROWS = 128


def _cparams(sem):
    return pltpu.CompilerParams(dimension_semantics=sem, vmem_limit_bytes=VMEM_LIMIT_V7X)


def _single(shape, imap):
    return pl.BlockSpec(shape, imap, pipeline_mode=pl.Buffered(1))


def _layer_norm_rows(y, g, b):
    mu = jnp.mean(y, axis=-1, keepdims=True)
    d = y - mu
    var = jnp.mean(d * d, axis=-1, keepdims=True)
    return d * lax.rsqrt(var + LN_EPS) * g + b


def _residual_ln_inplace(x_ref, o_ref, g_ref, b_ref, *, alpha, scale):
    g = g_ref[...]
    b = b_ref[...]

    def body(i, carry):
        rows = pl.ds(pl.multiple_of(i * LN_ROWS, LN_ROWS), LN_ROWS)
        y = alpha * x_ref[rows, :] + scale * o_ref[rows, :]
        o_ref[rows, :] = _layer_norm_rows(y, g, b)
        return carry

    lax.fori_loop(0, o_ref.shape[0] // LN_ROWS, body, 0)


def _dot(a, b):
    return jnp.dot(a, b, preferred_element_type=F32)


def _dot_nt(a, b):
    return lax.dot_general(a, b, (((1,), (1,)), ((), ())), preferred_element_type=F32)


def _dot_tn(a, b):
    return lax.dot_general(a, b, (((0,), (0,)), ((), ())), preferred_element_type=F32)


def _ffn_ln_kernel(x_ref, wg_ref, wu_ref, wd_ref, g_ref, b_ref, o_ref, xb_ref, *, alpha):
    f = pl.program_id(1)

    @pl.when(f == 0)
    def _():
        xb_ref[...] = x_ref[...].astype(BF16)
        o_ref[...] = jnp.zeros_like(o_ref)

    xb = xb_ref[...]
    h = _dot(xb, wg_ref[0, 0])
    u = _dot(xb, wu_ref[0, 0])
    a = (h * jax.nn.sigmoid(h) * u).astype(BF16)
    o_ref[...] += _dot(a, wd_ref[0, 0])

    @pl.when(f == pl.num_programs(1) - 1)
    def _():
        _residual_ln_inplace(x_ref, o_ref, g_ref, b_ref, alpha=alpha, scale=0.5)


def ffn_ln(x, wg, wu, wd, g, b, *, layer, which, alpha, tm, tf):
    m, d = x.shape
    nf = wg.shape[-1] // tf
    return pl.pallas_call(
        functools.partial(_ffn_ln_kernel, alpha=alpha),
        out_shape=jax.ShapeDtypeStruct((m, d), F32),
        grid=(m // tm, nf),
        in_specs=[
            pl.BlockSpec((tm, d), lambda i, f: (i, 0)),
            pl.BlockSpec((1, 1, d, tf), lambda i, f: (layer, which, 0, f)),
            pl.BlockSpec((1, 1, d, tf), lambda i, f: (layer, which, 0, f)),
            pl.BlockSpec((1, 1, tf, d), lambda i, f: (layer, which, f, 0)),
            pl.BlockSpec((1, d), lambda i, f: (0, 0)),
            pl.BlockSpec((1, d), lambda i, f: (0, 0)),
        ],
        out_specs=pl.BlockSpec((tm, d), lambda i, f: (i, 0)),
        scratch_shapes=[pltpu.VMEM((tm, d), BF16)],
        compiler_params=_cparams(("parallel", "arbitrary")),
        name="ffn_ln",
    )(x, wg, wu, wd, g, b)


def _matmul_kernel(x_ref, w_ref, o_ref, xb_ref):
    @pl.when(pl.program_id(1) == 0)
    def _():
        xb_ref[...] = x_ref[...].astype(BF16)

    o_ref[...] = _dot(xb_ref[...], w_ref[...]).astype(o_ref.dtype)


def matmul(x, w, *, tm, tn, out_dtype):
    m, k = x.shape
    n = w.shape[1]
    return pl.pallas_call(
        _matmul_kernel,
        out_shape=jax.ShapeDtypeStruct((m, n), out_dtype),
        grid=(m // tm, n // tn),
        in_specs=[
            pl.BlockSpec((tm, k), lambda i, j: (i, 0)),
            pl.BlockSpec((k, tn), lambda i, j: (0, j)),
        ],
        out_specs=pl.BlockSpec((tm, tn), lambda i, j: (i, j)),
        scratch_shapes=[pltpu.VMEM((tm, k), BF16)],
        compiler_params=_cparams(("parallel", "arbitrary")),
        name="matmul",
    )(x, w)


def _proj_ln_kernel(a1_ref, a2_ref, w1_ref, w2_ref, x_ref, g_ref, b_ref, o_ref, *, alpha, tn):
    n = pl.program_id(1)
    c0 = pl.multiple_of(n * tn, tn)
    o_ref[:, pl.ds(c0, tn)] = _dot(a1_ref[...], w1_ref[...]) + _dot(a2_ref[...], w2_ref[...])

    @pl.when(n == pl.num_programs(1) - 1)
    def _():
        _residual_ln_inplace(x_ref, o_ref, g_ref, b_ref, alpha=alpha, scale=1.0)


def proj_ln(a1, a2, w, x, g, b, *, alpha, tm, tn):
    m, kh = a1.shape
    d = w.shape[1]
    return pl.pallas_call(
        functools.partial(_proj_ln_kernel, alpha=alpha, tn=tn),
        out_shape=jax.ShapeDtypeStruct((m, d), F32),
        grid=(m // tm, d // tn),
        in_specs=[
            pl.BlockSpec((tm, kh), lambda i, n: (i, 0)),
            pl.BlockSpec((tm, kh), lambda i, n: (i, 0)),
            pl.BlockSpec((kh, tn), lambda i, n: (0, n)),
            pl.BlockSpec((kh, tn), lambda i, n: (1, n)),
            pl.BlockSpec((tm, d), lambda i, n: (i, 0)),
            pl.BlockSpec((1, d), lambda i, n: (0, 0)),
            pl.BlockSpec((1, d), lambda i, n: (0, 0)),
        ],
        out_specs=pl.BlockSpec((tm, d), lambda i, n: (i, 0)),
        compiler_params=_cparams(("parallel", "arbitrary")),
        name="proj_ln",
    )(a1, a2, w, w, x, g, b)


def _cross_ln_kernel(x_ref, wq_ref, kv_ref, wo_ref, g_ref, b_ref, o_ref, *, alpha, n_heads):
    x = x_ref[...]
    q = _dot(x.astype(BF16), wq_ref[...])
    kv = kv_ref[0]
    cw = n_heads * HEAD_DIM
    outs = []
    for h in range(n_heads):
        qh = q[:, h * HEAD_DIM:(h + 1) * HEAD_DIM].astype(BF16)
        kh = kv[:, h * HEAD_DIM:(h + 1) * HEAD_DIM]
        vh = kv[:, cw + h * HEAD_DIM:cw + (h + 1) * HEAD_DIM]
        logits = _dot_nt(qh, kh) * ATT_SCALE
        mx = jnp.max(logits, axis=-1, keepdims=True)
        e = jnp.exp(logits - mx)
        p = e / jnp.sum(e, axis=-1, keepdims=True)
        outs.append(_dot(p.astype(BF16), vh))
    o = jnp.concatenate(outs, axis=-1).astype(BF16)
    o_ref[...] = _dot(o, wo_ref[...])
    _residual_ln_inplace(x_ref, o_ref, g_ref, b_ref, alpha=alpha, scale=1.0)


def cross_ln(x, wq, kv, wo, g, b, *, alpha, tm, rows_per_batch):
    m, d = x.shape
    cw = wq.shape[1]
    nmem = kv.shape[1]
    steps_per_batch = rows_per_batch // tm
    return pl.pallas_call(
        functools.partial(_cross_ln_kernel, alpha=alpha, n_heads=cw // HEAD_DIM),
        out_shape=jax.ShapeDtypeStruct((m, d), F32),
        grid=(m // tm,),
        in_specs=[
            pl.BlockSpec((tm, d), lambda i: (i, 0)),
            _single((d, cw), lambda i: (0, 0)),
            pl.BlockSpec((1, nmem, 2 * cw), lambda i: (i // steps_per_batch, 0, 0)),
            _single((cw, d), lambda i: (0, 0)),
            pl.BlockSpec((1, d), lambda i: (0, 0)),
            pl.BlockSpec((1, d), lambda i: (0, 0)),
        ],
        out_specs=pl.BlockSpec((tm, d), lambda i: (i, 0)),
        compiler_params=_cparams(("parallel",)),
        name="cross_ln",
    )(x, wq, kv, wo, g, b)


def _split3(x, order):
    hi = x.astype(BF16).astype(F32)
    lo = x - hi
    parts = [hi if c == "h" else lo for c in order]
    parts.append(jnp.zeros((x.shape[0], IDX_K - 3 * x.shape[1]), F32))
    return jnp.concatenate(parts, axis=-1).astype(BF16)


def _dsa_prep_kernel(kv_ref, qi_ref, sm_ref, kvg_ref, kng_ref, knb_ref,
                     ckv_ref, ckvt_ref, k3_ref, qi3_ref, wt_ref):
    kv = kv_ref[...]
    ms = jnp.mean(kv * kv, axis=-1, keepdims=True)
    ckv = kv * lax.rsqrt(ms + LN_EPS) * kvg_ref[...]
    ckv_ref[...] = ckv.astype(BF16)
    ckvt_ref[0] = jnp.transpose(ckv).astype(BF16)

    sm = sm_ref[...]
    kidx = _layer_norm_rows(sm[:, :IDX_DIM], kng_ref[...], knb_ref[...])
    k3_ref[...] = _split3(kidx, "hlh")

    wt = jnp.transpose(sm)
    wt_ref[...] = wt[IDX_DIM:IDX_DIM + N_IDX_HEADS, :] * (IDX_HEAD_SCALE * IDX_SCALE)

    qi = qi_ref[...]
    for h in range(N_IDX_HEADS):
        qi3_ref[h] = _split3(qi[:, h * IDX_DIM:(h + 1) * IDX_DIM], "hhl")


def dsa_prep(proj, kv_g, kn_g, kn_b, *, seq):
    m = proj.shape[0]
    tq = Q_BLOCK
    nq = m // tq
    nqb = seq // tq
    return pl.pallas_call(
        _dsa_prep_kernel,
        out_shape=(
            jax.ShapeDtypeStruct((m, KV_LORA_RANK), BF16),
            jax.ShapeDtypeStruct((m // seq, KV_LORA_RANK, seq), BF16),
            jax.ShapeDtypeStruct((m, IDX_K), BF16),
            jax.ShapeDtypeStruct((nq * N_IDX_HEADS, tq, IDX_K), BF16),
            jax.ShapeDtypeStruct((nq * N_IDX_HEADS, tq), F32),
        ),
        grid=(nq,),
        in_specs=[
            pl.BlockSpec((tq, KV_LORA_RANK), lambda i: (i, COL_KV // KV_LORA_RANK)),
            pl.BlockSpec((tq, IDX_WIDTH), lambda i: (i, COL_QIDX // IDX_WIDTH)),
            pl.BlockSpec((tq, LANE), lambda i: (i, COL_SMALL // LANE)),
            pl.BlockSpec((1, KV_LORA_RANK), lambda i: (0, 0)),
            pl.BlockSpec((1, IDX_DIM), lambda i: (0, 0)),
            pl.BlockSpec((1, IDX_DIM), lambda i: (0, 0)),
        ],
        out_specs=(
            pl.BlockSpec((tq, KV_LORA_RANK), lambda i: (i, 0)),
            pl.BlockSpec((1, KV_LORA_RANK, tq), lambda i: (i // nqb, 0, i % nqb)),
            pl.BlockSpec((tq, IDX_K), lambda i: (i, 0)),
            pl.BlockSpec((N_IDX_HEADS, tq, IDX_K), lambda i: (i, 0, 0)),
            pl.BlockSpec((N_IDX_HEADS, tq), lambda i: (i, 0)),
        ),
        compiler_params=_cparams(("parallel",)),
        name="dsa_prep",
    )(proj, proj, proj, kv_g, kn_g, kn_b)


def _bucket_thresholds():
    max_exact = N_BUCKETS // 2
    rel = np.arange(max_exact, 4 * MAX_DISTANCE, dtype=np.float32)
    large = max_exact + (np.log(rel / np.float32(max_exact)) / np.float32(math.log(MAX_DISTANCE / max_exact))
                         * np.float32(N_BUCKETS - max_exact)).astype(np.int32)
    large = np.minimum(large, N_BUCKETS - 1)
    return [int(rel[np.argmax(large >= bkt)]) for bkt in range(max_exact + 1, N_BUCKETS)]


def _bias_table_kernel(rb_ref, o_ref):
    di = pl.program_id(0)
    h = pl.program_id(1)
    max_exact = N_BUCKETS // 2
    sl = lax.broadcasted_iota(I32, (KEY_GROUP, Q_BLOCK), 0)
    tl = lax.broadcasted_iota(I32, (KEY_GROUP, Q_BLOCK), 1)
    rel = jnp.maximum(di * Q_BLOCK + tl - sl, 0)
    large = jnp.full(rel.shape, max_exact, I32)
    for th in _bucket_thresholds():
        large = large + (rel >= th).astype(I32)
    bucket = jnp.where(rel < max_exact, rel, large)
    far = rb_ref[N_BUCKETS - 1, h]
    val = jnp.zeros(rel.shape, F32)
    for bkt in range(N_BUCKETS - 1):
        val = jnp.where(bucket == bkt, rb_ref[bkt, h] - far, val)
    o_ref[0] = val


def bias_tables(rel_bias):
    nh = rel_bias.shape[1]
    return pl.pallas_call(
        _bias_table_kernel,
        out_shape=jax.ShapeDtypeStruct((3, KEY_GROUP, nh * Q_BLOCK), F32),
        grid=(3, nh),
        in_specs=[pl.BlockSpec(memory_space=pltpu.SMEM)],
        out_specs=pl.BlockSpec((1, KEY_GROUP, Q_BLOCK), lambda d, h: (d, 0, h)),
        compiler_params=_cparams(("arbitrary", "arbitrary")),
        name="bias_tables",
    )(rel_bias)


def _dsa_kernel(q_ref, ckv_ref, ckvt_ref, k3_ref, qi3_ref, wt_ref, wuk_ref, wuv_ref, bias_ref, o_ref,
                keys_ref, qabs_ref, acc_ref, *, topk):
    j = pl.program_id(1)
    nh = N_ATT_HEADS
    tq = Q_BLOCK
    t_lane = j * tq + lax.broadcasted_iota(I32, (1, tq), 1)

    n_sgrp = (j + 1 + (SCORE_GROUP // tq - 1)) // (SCORE_GROUP // tq)
    qi3 = qi3_ref[...].reshape(N_IDX_HEADS * tq, IDX_K)

    def score_body(g, carry):
        r0 = pl.multiple_of(g * SCORE_GROUP, SCORE_GROUP)
        lt = _dot_nt(k3_ref[pl.ds(r0, SCORE_GROUP), :], qi3)
        sc = jnp.zeros((SCORE_GROUP, tq), F32)
        for h in range(N_IDX_HEADS):
            sc = sc + jnp.maximum(lt[:, h * tq:(h + 1) * tq], 0.0) * wt_ref[h:h + 1, :]
        s_row = r0 + lax.broadcasted_iota(I32, (SCORE_GROUP, 1), 0)
        sc = jnp.where(s_row <= t_lane, sc + 0.0, -jnp.inf)
        bits = pltpu.bitcast(sc, I32)
        keys_ref[pl.ds(r0, SCORE_GROUP), :] = jnp.where(bits < 0, bits ^ 0x7FFFFFFF, bits)
        return carry

    lax.fori_loop(0, n_sgrp, score_body, 0)

    def count_ge(cand):
        def body(g, part):
            r0 = pl.multiple_of(g * SCORE_GROUP, SCORE_GROUP)
            ind = jnp.where(keys_ref[pl.ds(r0, SCORE_GROUP), :] >= cand, 1.0, 0.0)
            return part + jnp.sum(ind.reshape(SCORE_GROUP // COUNT_ROWS, COUNT_ROWS, tq), axis=0)
        part = lax.fori_loop(0, n_sgrp, body, jnp.zeros((COUNT_ROWS, tq), F32))
        return jnp.sum(part, axis=0, keepdims=True)

    thr0 = jnp.where(count_ge(jnp.zeros((1, tq), I32)) >= topk, 0, INT_MIN).astype(I32)

    def bit_body(i, thr):
        cand = thr | jnp.left_shift(jnp.int32(1), 30 - i)
        return jnp.where(count_ge(cand) >= topk, cand, thr)

    thr = lax.fori_loop(0, 31, bit_body, thr0)

    n_ge = count_ge(thr)
    tie_lane = (n_ge > topk) & (thr > KEY_NEG_INF)
    has_tie = jnp.max(jnp.where(tie_lane, 1.0, 0.0)) > 0.0

    @pl.when(has_tie)
    def _():
        need = topk - count_ge(thr + 1)
        r_i = lax.broadcasted_iota(I32, (tq, tq), 0)
        c_i = lax.broadcasted_iota(I32, (tq, tq), 1)
        tri = jnp.where(c_i <= r_i, 1.0, 0.0).astype(BF16)

        def tie_body(kb, seen):
            r0 = pl.multiple_of(kb * tq, tq)
            blk = keys_ref[pl.ds(r0, tq), :]
            s_row = r0 + lax.broadcasted_iota(I32, (tq, 1), 0)
            eq = (blk == thr) & (s_row <= t_lane) & tie_lane
            rank = seen + _dot(tri, jnp.where(eq, 1.0, 0.0).astype(BF16))
            keys_ref[pl.ds(r0, tq), :] = jnp.where(eq & (rank > need), thr - 1, blk)
            return rank[tq - 1:tq, :]

        lax.fori_loop(0, j + 1, tie_body, jnp.zeros((1, tq), F32))

    for h in range(nh):
        qh = q_ref[:, h * HEAD_DIM:(h + 1) * HEAD_DIM].astype(BF16)
        qabs_ref[:, h * tq:(h + 1) * tq] = (_dot_nt(wuk_ref[h], qh) * ATT_SCALE).astype(BF16)

    acc_ref[...] = jnp.zeros(acc_ref.shape, F32)

    def attend(g, carry, bias_di):
        m, l = carry
        c0 = pl.multiple_of(g * KEY_GROUP, KEY_GROUP)
        s_all = _dot(ckv_ref[pl.ds(c0, KEY_GROUP), :], qabs_ref[...])
        s_row = c0 + lax.broadcasted_iota(I32, (KEY_GROUP, 1), 0)
        sel = (keys_ref[pl.ds(c0, KEY_GROUP), :] >= thr) & (s_row <= t_lane)
        addm = jnp.where(sel, 0.0, NEG_BIG)
        ps, ms, ls, corrs = [], [], [], []
        for h in range(nh):
            lanes = slice(h * tq, (h + 1) * tq)
            s = s_all[:, lanes] + addm
            if bias_di is not None:
                s = s + bias_ref[bias_di, :, lanes]
            m_prev = m[:, lanes]
            m_new = jnp.maximum(m_prev, jnp.max(s, axis=0, keepdims=True))
            p = jnp.exp(s - m_new)
            corr = jnp.exp(m_prev - m_new)
            ls.append(corr * l[:, lanes] + jnp.sum(p, axis=0, keepdims=True))
            ms.append(m_new)
            corrs.append(corr)
            ps.append(p.astype(BF16))
        pt = jnp.concatenate(ps, axis=1)
        corr_all = jnp.concatenate(corrs, axis=1)
        acc_ref[...] = acc_ref[...] * corr_all + _dot(ckvt_ref[0, :, pl.ds(c0, KEY_GROUP)], pt)
        return jnp.concatenate(ms, axis=1), jnp.concatenate(ls, axis=1)

    carry = (jnp.full((1, nh * tq), NEG_BIG, F32), jnp.zeros((1, nh * tq), F32))
    n_far = jnp.maximum(j // 2 - 1 + j % 2, 0)
    carry = lax.fori_loop(0, n_far, lambda g, cr: attend(g, cr, None), carry)
    carry = lax.cond((j % 2 == 0) & (j >= 2), lambda cr: attend(j // 2 - 1, cr, 2), lambda cr: cr, carry)
    _, l = attend(j // 2, carry, j % 2)

    o_lat = (acc_ref[...] / l).astype(BF16)
    for h in range(nh):
        o_ref[:, h * HEAD_DIM:(h + 1) * HEAD_DIM] = _dot_tn(
            o_lat[:, h * tq:(h + 1) * tq], wuv_ref[h]).astype(o_ref.dtype)


def dsa_attention(proj, ckv, ckvt, k3, qi3, wt, wuk, wuv, bias, *, batch, seq, topk):
    tq = Q_BLOCK
    nq = seq // tq
    nh = N_ATT_HEADS
    r = KV_LORA_RANK
    seq_pad = -(-seq // SCORE_GROUP) * SCORE_GROUP
    return pl.pallas_call(
        functools.partial(_dsa_kernel, topk=topk),
        out_shape=jax.ShapeDtypeStruct((batch * seq, ATT_WIDTH), BF16),
        grid=(batch, nq),
        in_specs=[
            pl.BlockSpec((tq, ATT_WIDTH), lambda b, j: (b * nq + j, 0)),
            _single((seq, r), lambda b, j: (b, 0)),
            _single((1, r, seq), lambda b, j: (b, 0, 0)),
            _single((seq, IDX_K), lambda b, j: (b, 0)),
            pl.BlockSpec((N_IDX_HEADS, tq, IDX_K), lambda b, j: (b * nq + j, 0, 0)),
            pl.BlockSpec((N_IDX_HEADS, tq), lambda b, j: (b * nq + j, 0)),
            _single((nh, r, HEAD_DIM), lambda b, j: (0, 0, 0)),
            _single((nh, r, HEAD_DIM), lambda b, j: (0, 0, 0)),
            _single((3, KEY_GROUP, nh * tq), lambda b, j: (0, 0, 0)),
        ],
        out_specs=pl.BlockSpec((tq, ATT_WIDTH), lambda b, j: (b * nq + j, 0)),
        scratch_shapes=[
            pltpu.VMEM((seq_pad, tq), I32),
            pltpu.VMEM((r, nh * tq), BF16),
            pltpu.VMEM((r, nh * tq), F32),
        ],
        compiler_params=_cparams(("parallel", "arbitrary")),
        name="dsa",
    )(proj, ckv, ckvt, k3, qi3, wt, wuk, wuv, bias)


def _cumsum_rows(x):
    n = x.shape[0]
    row = lax.broadcasted_iota(I32, (n, 1), 0)
    sh = 1
    while sh < n:
        x = x + jnp.where(row >= sh, pltpu.roll(x, sh, 0), 0.0)
        sh *= 2
    return x


def _hgrn2_kernel(q_ref, f_ref, i_ref, g_ref, lb_ref, ng_ref, o_ref, st_ref, *, layer):
    c = REC_CHUNK
    dk = HEAD_DIM

    @pl.when(pl.program_id(2) == 0)
    def _():
        st_ref[...] = jnp.zeros_like(st_ref)

    lbr = lb_ref[...]
    e = jnp.exp(lbr - jnp.max(lbr, axis=0, keepdims=True))
    pr = e / jnp.sum(e, axis=0, keepdims=True)
    lb_all = jnp.zeros((1, lbr.shape[1]), F32)
    for i in range(1, layer + 1):
        lb_all = lb_all + pr[i:i + 1, :]

    row = lax.broadcasted_iota(I32, (c, 1), 0)
    col = lax.broadcasted_iota(I32, (1, c), 1)
    sub = 8
    nsub = c // sub
    tl3 = lax.broadcasted_iota(I32, (1, sub, 1), 1)
    g3 = lax.broadcasted_iota(I32, (nsub, 1, 1), 0)
    col3 = lax.broadcasted_iota(I32, (1, 1, c), 2)
    levels = []
    m = c // 2
    while m >= sub:
        levels.append(m)
        m //= 2

    def head_chunk(r0, hh):
        lanes = slice(hh * dk, (hh + 1) * dk)
        lb = lb_all[:, lanes]
        q = q_ref[pl.ds(r0, c), lanes]
        qf = q * jax.nn.sigmoid(q)
        f = lb + (1.0 - lb) * jax.nn.sigmoid(f_ref[pl.ds(r0, c), lanes])
        k = 1.0 - f
        v = i_ref[pl.ds(r0, c), lanes].astype(BF16)
        cum = _cumsum_rows(jnp.log(f))

        a = jnp.zeros((c, c), F32)
        for m in levels:
            first = (row & (2 * m - 1)) < m
            bnd = jnp.concatenate(
                [jnp.broadcast_to(cum[b * 2 * m + m - 1:b * 2 * m + m, :], (2 * m, dk))
                 for b in range(c // (2 * m))], axis=0)
            qs = jnp.where(first, 0.0, qf * jnp.exp(jnp.minimum(cum - bnd, 0.0))).astype(BF16)
            ks = jnp.where(first, k * jnp.exp(jnp.minimum(bnd - cum, 0.0)), 0.0).astype(BF16)
            same = (row & -(2 * m)) == (col & -(2 * m))
            a = a + jnp.where(same, _dot_nt(qs, ks), 0.0)
        cum3 = cum.reshape(nsub, sub, dk)
        k3 = k.reshape(nsub, sub, dk)
        qf3 = qf.reshape(nsub, sub, dk)
        a3 = jnp.zeros((nsub, sub, c), F32)
        for s in range(sub):
            d = jnp.where(tl3 >= s, cum3 - cum3[:, s:s + 1, :], -jnp.inf)
            prod = qf3 * k3[:, s:s + 1, :] * jnp.exp(d)
            colsum = jnp.sum(prod, axis=-1, keepdims=True)
            a3 = a3 + jnp.where(col3 == g3 * sub + s, colsum, 0.0)
        a = a + a3.reshape(c, c)

        st = st_ref[hh]
        o = _dot(a.astype(BF16), v) + _dot_nt((qf * jnp.exp(cum)).astype(BF16), st.astype(BF16))
        last = cum[c - 1:c, :]
        kl = (k * jnp.exp(last - cum)).astype(BF16)
        st_ref[hh] = st * jnp.exp(last) + _dot_tn(v, kl)

        ms = jnp.mean(o * o, axis=-1, keepdims=True)
        gt = g_ref[pl.ds(r0, c), lanes]
        on = o * lax.rsqrt(ms + LN_EPS) * ng_ref[:, lanes]
        o_ref[pl.ds(r0, c), lanes] = (on * (gt * jax.nn.sigmoid(gt))).astype(o_ref.dtype)

    def chunk_body(ci, carry):
        r0 = pl.multiple_of(ci * c, c)
        for hh in range(REC_HEADS_PER_STEP):
            head_chunk(r0, hh)
        return carry

    lax.fori_loop(0, q_ref.shape[0] // c, chunk_body, 0)


def hgrn2(proj, rec_lb, norm_g, *, layer, batch, seq):
    tb = min(REC_TBLOCK, seq)
    nt = seq // tb
    hw = REC_HEADS_PER_STEP * HEAD_DIM
    ng = N_REC_HEADS // REC_HEADS_PER_STEP

    def seg(k):
        return pl.BlockSpec((tb, hw), lambda b, h, t, k=k: (b * nt + t, k * ng + h))

    return pl.pallas_call(
        functools.partial(_hgrn2_kernel, layer=layer),
        out_shape=jax.ShapeDtypeStruct((batch * seq, REC_WIDTH), BF16),
        grid=(batch, ng, nt),
        in_specs=[
            seg(0), seg(1), seg(2), seg(3),
            pl.BlockSpec((rec_lb.shape[0], hw), lambda b, h, t: (0, h)),
            pl.BlockSpec((1, hw), lambda b, h, t: (0, h)),
        ],
        out_specs=pl.BlockSpec((tb, hw), lambda b, h, t: (b * nt + t, h)),
        scratch_shapes=[pltpu.VMEM((REC_HEADS_PER_STEP, HEAD_DIM, HEAD_DIM), F32)],
        compiler_params=_cparams(("parallel", "parallel", "arbitrary")),
        name="hgrn2",
    )(proj, proj, proj, proj, rec_lb, norm_g)


def _pack_w_in(w):
    kv0 = ATT_WIDTH
    qi0 = kv0 + KV_LORA_RANK
    sm0 = qi0 + IDX_WIDTH
    sm1 = sm0 + IDX_DIM + N_IDX_HEADS
    pad = jnp.zeros((w.shape[0], SMALL_WIDTH - IDX_DIM - N_IDX_HEADS), BF16)
    parts = [w[:, :kv0], w[:, qi0:sm0], w[:, kv0:qi0], w[:, sm0:sm1]]
    w_att = jnp.concatenate([p.astype(BF16) for p in parts] + [pad], axis=1)
    return w_att, w[:, sm1:].astype(BF16)


def _row_tile(m, want):
    t = min(want, m)
    while m % t:
        t //= 2
    return t


def kernel(x, mem, w_in, w_uk, w_uv, kv_norm_g, idx_kn_g, idx_kn_b, rec_lb, rec_norm_g, w_out, rel_bias,
           wq_c, wk_c, wv_c, wo_c, ffn_gate, ffn_up, ffn_down, ln_g, ln_b):
    batch, seq, d = x.shape
    depth = w_in.shape[0]
    m = batch * seq
    alpha = (2 * depth) ** 0.25
    topk = min(TOPK_MAX, seq // 4)
    d_ff = ffn_gate.shape[-1]
    tf = 256 if d_ff % 256 == 0 else 128

    xf = x.reshape(m, d)
    mem_f = mem.reshape(batch * mem.shape[1], d)
    bias = bias_tables(rel_bias)
    rec_lb = rec_lb.astype(F32)
    wg_b, wu_b, wd_b = ffn_gate.astype(BF16), ffn_up.astype(BF16), ffn_down.astype(BF16)

    def ffn(xf, l, i):
        return ffn_ln(xf, wg_b, wu_b, wd_b, ln_g[l, 3 * i:3 * i + 1], ln_b[l, 3 * i:3 * i + 1],
                      layer=l, which=i, alpha=alpha, tm=_row_tile(m, 512), tf=tf)

    for l in range(depth):
        xf = ffn(xf, l, 0)

        w_att, w_rec = _pack_w_in(w_in[l])
        proj_a = matmul(xf, w_att, tm=_row_tile(m, 512), tn=768, out_dtype=F32)
        proj_r = matmul(xf, w_rec, tm=_row_tile(m, 512), tn=1024, out_dtype=F32)
        ckv, ckvt, k3, qi3, wt = dsa_prep(proj_a, kv_norm_g[l][None], idx_kn_g[l][None], idx_kn_b[l][None], seq=seq)
        o_att = dsa_attention(proj_a, ckv, ckvt, k3, qi3, wt, w_uk[l].astype(BF16), w_uv[l].astype(BF16), bias,
                              batch=batch, seq=seq, topk=topk)
        o_rec = hgrn2(proj_r, rec_lb, rec_norm_g[l][None], layer=l, batch=batch, seq=seq)
        xf = proj_ln(o_att, o_rec, w_out[l].astype(BF16), xf, ln_g[l, 1:2], ln_b[l, 1:2],
                     alpha=alpha, tm=_row_tile(m, 512), tn=512)

        kv = matmul(mem_f, jnp.concatenate([wk_c[l], wv_c[l]], axis=1).astype(BF16),
                    tm=_row_tile(mem_f.shape[0], 512), tn=256, out_dtype=BF16)
        kv = kv.reshape(batch, mem.shape[1], kv.shape[-1])
        xf = cross_ln(xf, wq_c[l].astype(BF16), kv, wo_c[l].astype(BF16), ln_g[l, 2:3], ln_b[l, 2:3],
                      alpha=alpha, tm=_row_tile(seq, 256), rows_per_batch=seq)

        xf = ffn(xf, l, 1)
    return xf.reshape(batch, seq, d)
```

```python
import functools
import math

import jax
import jax.numpy as jnp
import numpy as np
from jax import lax
from jax.experimental import pallas as pl
from jax.experimental.pallas import tpu as pltpu

F32 = jnp.float32
BF16 = jnp.bfloat16
I32 = jnp.int32

LANE = 128
VMEM_LIMIT_V7X = 60 * 1024 * 1024

HEAD_DIM = 128
N_ATT_HEADS = 16
N_REC_HEADS = 16
KV_LORA_RANK = 512
N_IDX_HEADS = 16
IDX_DIM = 64
TOPK_MAX = 256
Q_BLOCK = 128
N_BUCKETS = 32
MAX_DISTANCE = 128
N_CROSS_HEADS = 4
LN_EPS = 1e-5
ATT_SCALE = HEAD_DIM ** -0.5
IDX_SCALE = IDX_DIM ** -0.5
IDX_HEAD_SCALE = N_IDX_HEADS ** -0.5

ATT_WIDTH = N_ATT_HEADS * HEAD_DIM
IDX_WIDTH = N_IDX_HEADS * IDX_DIM
REC_WIDTH = N_REC_HEADS * HEAD_DIM
COL_QIDX = ATT_WIDTH
COL_KV = COL_QIDX + IDX_WIDTH
COL_SMALL = COL_KV + KV_LORA_RANK
SMALL_WIDTH = 256
COL_REC = COL_SMALL + SMALL_WIDTH

NEG_BIG = -1e30
INT_MIN = -2 ** 31
KEY_NEG_INF = (0xFF800000 - 2 ** 32) ^ 0x7FFFFFFF

REC_CHUNK = 64
REC_TBLOCK = 512
REC_HEADS_PER_STEP = 4
KEY_GROUP = 256
SCORE_GROUP = 512
IDX_K = 256
COUNT_ROWS = 64
LN_ROWS = 128


def _cparams(sem):
    return pltpu.CompilerParams(dimension_semantics=sem, vmem_limit_bytes=VMEM_LIMIT_V7X)


def _single(shape, imap):
    return pl.BlockSpec(shape, imap, pipeline_mode=pl.Buffered(1))


def _layer_norm_rows(y, g, b):
    mu = jnp.mean(y, axis=-1, keepdims=True)
    d = y - mu
    var = jnp.mean(d * d, axis=-1, keepdims=True)
    return d * lax.rsqrt(var + LN_EPS) * g + b


def _residual_ln_inplace(x_ref, o_ref, g_ref, b_ref, *, alpha, scale):
    g = g_ref[...]
    b = b_ref[...]

    def body(i, carry):
        rows = pl.ds(pl.multiple_of(i * LN_ROWS, LN_ROWS), LN_ROWS)
        y = alpha * x_ref[rows, :] + scale * o_ref[rows, :]
        o_ref[rows, :] = _layer_norm_rows(y, g, b)
        return carry

    lax.fori_loop(0, o_ref.shape[0] // LN_ROWS, body, 0)


def _dot(a, b):
    return jnp.dot(a, b, preferred_element_type=F32)


def _dot_nt(a, b):
    return lax.dot_general(a, b, (((1,), (1,)), ((), ())), preferred_element_type=F32)


def _dot_tn(a, b):
    return lax.dot_general(a, b, (((0,), (0,)), ((), ())), preferred_element_type=F32)


def _ffn_ln_kernel(x_ref, wg_ref, wu_ref, wd_ref, g_ref, b_ref, *rest, alpha, n_cast):
    cast_in = rest[:n_cast]
    o_ref = rest[n_cast]
    cast_out = rest[n_cast + 1:2 * n_cast + 1]
    xb_ref = rest[2 * n_cast + 1]
    f = pl.program_id(1)

    @pl.when(f == 0)
    def _():
        xb_ref[...] = x_ref[...].astype(BF16)
        o_ref[...] = jnp.zeros_like(o_ref)

    xb = xb_ref[...]
    h = _dot(xb, wg_ref[...])
    u = _dot(xb, wu_ref[...])
    a = (h * jax.nn.sigmoid(h) * u).astype(BF16)
    o_ref[...] += _dot(a, wd_ref[...])

    for src, dst in zip(cast_in, cast_out):
        dst[...] = src[0, 0].astype(BF16)

    @pl.when(f == pl.num_programs(1) - 1)
    def _():
        _residual_ln_inplace(x_ref, o_ref, g_ref, b_ref, alpha=alpha, scale=0.5)


def ffn_ln(x, wg, wu, wd, g, b, *, alpha, tm, tf, cast_next=None):
    m, d = x.shape
    d_ff = wg.shape[1]
    ni, nf = m // tm, d_ff // tf
    in_specs = [
        pl.BlockSpec((tm, d), lambda i, f: (i, 0)),
        pl.BlockSpec((d, tf), lambda i, f: (0, f)),
        pl.BlockSpec((d, tf), lambda i, f: (0, f)),
        pl.BlockSpec((tf, d), lambda i, f: (f, 0)),
        pl.BlockSpec((1, d), lambda i, f: (0, 0)),
        pl.BlockSpec((1, d), lambda i, f: (0, 0)),
    ]
    out_shape = [jax.ShapeDtypeStruct((m, d), F32)]
    out_specs = [pl.BlockSpec((tm, d), lambda i, f: (i, 0))]
    operands = [x, wg, wu, wd, g, b]
    n_cast = 0
    if cast_next is not None:
        ng, nu, nd, nl, nw = cast_next
        td = d // ni
        n_cast = 3
        in_specs += [
            pl.BlockSpec((1, 1, td, tf), lambda i, f: (nl, nw, i, f)),
            pl.BlockSpec((1, 1, td, tf), lambda i, f: (nl, nw, i, f)),
            pl.BlockSpec((1, 1, tf, td), lambda i, f: (nl, nw, f, i)),
        ]
        out_shape += [jax.ShapeDtypeStruct((d, d_ff), BF16), jax.ShapeDtypeStruct((d, d_ff), BF16),
                      jax.ShapeDtypeStruct((d_ff, d), BF16)]
        out_specs += [
            pl.BlockSpec((td, tf), lambda i, f: (i, f)),
            pl.BlockSpec((td, tf), lambda i, f: (i, f)),
            pl.BlockSpec((tf, td), lambda i, f: (f, i)),
        ]
        operands += [ng, nu, nd]
    outs = pl.pallas_call(
        functools.partial(_ffn_ln_kernel, alpha=alpha, n_cast=n_cast),
        out_shape=tuple(out_shape),
        grid=(ni, nf),
        in_specs=in_specs,
        out_specs=tuple(out_specs),
        scratch_shapes=[pltpu.VMEM((tm, d), BF16)],
        compiler_params=_cparams(("parallel", "arbitrary")),
        name="ffn_ln",
    )(*operands)
    return outs[0], tuple(outs[1:])


def _matmul_kernel(x_ref, w_ref, o_ref, xb_ref):
    @pl.when(pl.program_id(1) == 0)
    def _():
        xb_ref[...] = x_ref[...].astype(BF16)

    o_ref[...] = _dot(xb_ref[...], w_ref[...]).astype(o_ref.dtype)


def matmul(x, w, *, tm, tn, out_dtype):
    m, k = x.shape
    n = w.shape[1]
    return pl.pallas_call(
        _matmul_kernel,
        out_shape=jax.ShapeDtypeStruct((m, n), out_dtype),
        grid=(m // tm, n // tn),
        in_specs=[
            pl.BlockSpec((tm, k), lambda i, j: (i, 0)),
            pl.BlockSpec((k, tn), lambda i, j: (0, j)),
        ],
        out_specs=pl.BlockSpec((tm, tn), lambda i, j: (i, j)),
        scratch_shapes=[pltpu.VMEM((tm, k), BF16)],
        compiler_params=_cparams(("parallel", "arbitrary")),
        name="matmul",
    )(x, w)


def _proj_ln_kernel(a1_ref, a2_ref, w1_ref, w2_ref, x_ref, g_ref, b_ref, o_ref, *, alpha, tn):
    n = pl.program_id(1)
    c0 = pl.multiple_of(n * tn, tn)
    o_ref[:, pl.ds(c0, tn)] = _dot(a1_ref[...], w1_ref[...]) + _dot(a2_ref[...], w2_ref[...])

    @pl.when(n == pl.num_programs(1) - 1)
    def _():
        _residual_ln_inplace(x_ref, o_ref, g_ref, b_ref, alpha=alpha, scale=1.0)


def proj_ln(a1, a2, w, x, g, b, *, alpha, tm, tn):
    m, kh = a1.shape
    d = w.shape[1]
    return pl.pallas_call(
        functools.partial(_proj_ln_kernel, alpha=alpha, tn=tn),
        out_shape=jax.ShapeDtypeStruct((m, d), F32),
        grid=(m // tm, d // tn),
        in_specs=[
            pl.BlockSpec((tm, kh), lambda i, n: (i, 0)),
            pl.BlockSpec((tm, kh), lambda i, n: (i, 0)),
            pl.BlockSpec((kh, tn), lambda i, n: (0, n)),
            pl.BlockSpec((kh, tn), lambda i, n: (1, n)),
            pl.BlockSpec((tm, d), lambda i, n: (i, 0)),
            pl.BlockSpec((1, d), lambda i, n: (0, 0)),
            pl.BlockSpec((1, d), lambda i, n: (0, 0)),
        ],
        out_specs=pl.BlockSpec((tm, d), lambda i, n: (i, 0)),
        compiler_params=_cparams(("parallel", "arbitrary")),
        name="proj_ln",
    )(a1, a2, w, w, x, g, b)


def _cross_ln_kernel(x_ref, wq_ref, kv_ref, wo_ref, g_ref, b_ref, o_ref, *, alpha, n_heads):
    x = x_ref[...]
    q = _dot(x.astype(BF16), wq_ref[...])
    kv = kv_ref[0]
    cw = n_heads * HEAD_DIM
    outs = []
    for h in range(n_heads):
        qh = q[:, h * HEAD_DIM:(h + 1) * HEAD_DIM].astype(BF16)
        kh = kv[:, h * HEAD_DIM:(h + 1) * HEAD_DIM]
        vh = kv[:, cw + h * HEAD_DIM:cw + (h + 1) * HEAD_DIM]
        logits = _dot_nt(qh, kh) * ATT_SCALE
        mx = jnp.max(logits, axis=-1, keepdims=True)
        e = jnp.exp(logits - mx)
        p = e / jnp.sum(e, axis=-1, keepdims=True)
        outs.append(_dot(p.astype(BF16), vh))
    o = jnp.concatenate(outs, axis=-1).astype(BF16)
    o_ref[...] = _dot(o, wo_ref[...])
    _residual_ln_inplace(x_ref, o_ref, g_ref, b_ref, alpha=alpha, scale=1.0)


def cross_ln(x, wq, kv, wo, g, b, *, alpha, tm, rows_per_batch):
    m, d = x.shape
    cw = wq.shape[1]
    nmem = kv.shape[1]
    steps_per_batch = rows_per_batch // tm
    return pl.pallas_call(
        functools.partial(_cross_ln_kernel, alpha=alpha, n_heads=cw // HEAD_DIM),
        out_shape=jax.ShapeDtypeStruct((m, d), F32),
        grid=(m // tm,),
        in_specs=[
            pl.BlockSpec((tm, d), lambda i: (i, 0)),
            _single((d, cw), lambda i: (0, 0)),
            pl.BlockSpec((1, nmem, 2 * cw), lambda i: (i // steps_per_batch, 0, 0)),
            _single((cw, d), lambda i: (0, 0)),
            pl.BlockSpec((1, d), lambda i: (0, 0)),
            pl.BlockSpec((1, d), lambda i: (0, 0)),
        ],
        out_specs=pl.BlockSpec((tm, d), lambda i: (i, 0)),
        compiler_params=_cparams(("parallel",)),
        name="cross_ln",
    )(x, wq, kv, wo, g, b)


def _split3(x, order):
    hi = x.astype(BF16).astype(F32)
    lo = x - hi
    parts = [hi if c == "h" else lo for c in order]
    parts.append(jnp.zeros((x.shape[0], IDX_K - 3 * x.shape[1]), F32))
    return jnp.concatenate(parts, axis=-1).astype(BF16)


def _dsa_prep_kernel(kv_ref, qi_ref, sm_ref, kvg_ref, kng_ref, knb_ref,
                     ckv_ref, ckvt_ref, k3_ref, qi3_ref, wt_ref):
    kv = kv_ref[...]
    ms = jnp.mean(kv * kv, axis=-1, keepdims=True)
    ckv = kv * lax.rsqrt(ms + LN_EPS) * kvg_ref[...]
    ckv_ref[...] = ckv.astype(BF16)
    ckvt_ref[0] = jnp.transpose(ckv).astype(BF16)

    sm = sm_ref[...]
    kidx = _layer_norm_rows(sm[:, :IDX_DIM], kng_ref[...], knb_ref[...])
    k3_ref[...] = _split3(kidx, "hlh")

    wt = jnp.transpose(sm)
    wt_ref[...] = wt[IDX_DIM:IDX_DIM + N_IDX_HEADS, :] * (IDX_HEAD_SCALE * IDX_SCALE)

    qi = qi_ref[...]
    for h in range(N_IDX_HEADS):
        qi3_ref[h] = _split3(qi[:, h * IDX_DIM:(h + 1) * IDX_DIM], "hhl")


def dsa_prep(proj, kv_g, kn_g, kn_b, *, seq):
    m = proj.shape[0]
    tq = Q_BLOCK
    nq = m // tq
    nqb = seq // tq
    return pl.pallas_call(
        _dsa_prep_kernel,
        out_shape=(
            jax.ShapeDtypeStruct((m, KV_LORA_RANK), BF16),
            jax.ShapeDtypeStruct((m // seq, KV_LORA_RANK, seq), BF16),
            jax.ShapeDtypeStruct((m, IDX_K), BF16),
            jax.ShapeDtypeStruct((nq * N_IDX_HEADS, tq, IDX_K), BF16),
            jax.ShapeDtypeStruct((nq * N_IDX_HEADS, tq), F32),
        ),
        grid=(nq,),
        in_specs=[
            pl.BlockSpec((tq, KV_LORA_RANK), lambda i: (i, COL_KV // KV_LORA_RANK)),
            pl.BlockSpec((tq, IDX_WIDTH), lambda i: (i, COL_QIDX // IDX_WIDTH)),
            pl.BlockSpec((tq, LANE), lambda i: (i, COL_SMALL // LANE)),
            pl.BlockSpec((1, KV_LORA_RANK), lambda i: (0, 0)),
            pl.BlockSpec((1, IDX_DIM), lambda i: (0, 0)),
            pl.BlockSpec((1, IDX_DIM), lambda i: (0, 0)),
        ],
        out_specs=(
            pl.BlockSpec((tq, KV_LORA_RANK), lambda i: (i, 0)),
            pl.BlockSpec((1, KV_LORA_RANK, tq), lambda i: (i // nqb, 0, i % nqb)),
            pl.BlockSpec((tq, IDX_K), lambda i: (i, 0)),
            pl.BlockSpec((N_IDX_HEADS, tq, IDX_K), lambda i: (i, 0, 0)),
            pl.BlockSpec((N_IDX_HEADS, tq), lambda i: (i, 0)),
        ),
        compiler_params=_cparams(("parallel",)),
        name="dsa_prep",
    )(proj, proj, proj, kv_g, kn_g, kn_b)


def _bucket_thresholds():
    max_exact = N_BUCKETS // 2
    rel = np.arange(max_exact, 4 * MAX_DISTANCE, dtype=np.float32)
    large = max_exact + (np.log(rel / np.float32(max_exact)) / np.float32(math.log(MAX_DISTANCE / max_exact))
                         * np.float32(N_BUCKETS - max_exact)).astype(np.int32)
    large = np.minimum(large, N_BUCKETS - 1)
    return [int(rel[np.argmax(large >= bkt)]) for bkt in range(max_exact + 1, N_BUCKETS)]


def _bias_table_kernel(rb_ref, o_ref):
    di = pl.program_id(0)
    h = pl.program_id(1)
    max_exact = N_BUCKETS // 2
    sl = lax.broadcasted_iota(I32, (KEY_GROUP, Q_BLOCK), 0)
    tl = lax.broadcasted_iota(I32, (KEY_GROUP, Q_BLOCK), 1)
    rel = jnp.maximum(di * Q_BLOCK + tl - sl, 0)
    large = jnp.full(rel.shape, max_exact, I32)
    for th in _bucket_thresholds():
        large = large + (rel >= th).astype(I32)
    bucket = jnp.where(rel < max_exact, rel, large)
    far = rb_ref[N_BUCKETS - 1, h]
    val = jnp.zeros(rel.shape, F32)
    for bkt in range(N_BUCKETS - 1):
        val = jnp.where(bucket == bkt, rb_ref[bkt, h] - far, val)
    o_ref[0] = val


def bias_tables(rel_bias):
    nh = rel_bias.shape[1]
    return pl.pallas_call(
        _bias_table_kernel,
        out_shape=jax.ShapeDtypeStruct((3, KEY_GROUP, nh * Q_BLOCK), F32),
        grid=(3, nh),
        in_specs=[pl.BlockSpec(memory_space=pltpu.SMEM)],
        out_specs=pl.BlockSpec((1, KEY_GROUP, Q_BLOCK), lambda d, h: (d, 0, h)),
        compiler_params=_cparams(("arbitrary", "arbitrary")),
        name="bias_tables",
    )(rel_bias)


def _dsa_kernel(q_ref, ckv_ref, ckvt_ref, k3_ref, qi3_ref, wt_ref, wuk_ref, wuv_ref, bias_ref, o_ref,
                keys_ref, qabs_ref, acc_ref, *, topk):
    j = pl.program_id(1)
    nh = N_ATT_HEADS
    tq = Q_BLOCK
    t_lane = j * tq + lax.broadcasted_iota(I32, (1, tq), 1)

    n_sgrp = (j + 1 + (SCORE_GROUP // tq - 1)) // (SCORE_GROUP // tq)
    qi3 = qi3_ref[...].reshape(N_IDX_HEADS * tq, IDX_K)

    def score_body(g, carry):
        r0 = pl.multiple_of(g * SCORE_GROUP, SCORE_GROUP)
        lt = _dot_nt(k3_ref[pl.ds(r0, SCORE_GROUP), :], qi3)
        sc = jnp.zeros((SCORE_GROUP, tq), F32)
        for h in range(N_IDX_HEADS):
            sc = sc + jnp.maximum(lt[:, h * tq:(h + 1) * tq], 0.0) * wt_ref[h:h + 1, :]
        s_row = r0 + lax.broadcasted_iota(I32, (SCORE_GROUP, 1), 0)
        sc = jnp.where(s_row <= t_lane, sc + 0.0, -jnp.inf)
        bits = pltpu.bitcast(sc, I32)
        keys_ref[pl.ds(r0, SCORE_GROUP), :] = jnp.where(bits < 0, bits ^ 0x7FFFFFFF, bits)
        return carry

    lax.fori_loop(0, n_sgrp, score_body, 0)

    def count_ge(cand):
        def body(g, part):
            r0 = pl.multiple_of(g * SCORE_GROUP, SCORE_GROUP)
            ind = jnp.where(keys_ref[pl.ds(r0, SCORE_GROUP), :] >= cand, 1.0, 0.0)
            return part + jnp.sum(ind.reshape(SCORE_GROUP // COUNT_ROWS, COUNT_ROWS, tq), axis=0)
        part = lax.fori_loop(0, n_sgrp, body, jnp.zeros((COUNT_ROWS, tq), F32))
        return jnp.sum(part, axis=0, keepdims=True)

    thr0 = jnp.where(count_ge(jnp.zeros((1, tq), I32)) >= topk, 0, INT_MIN).astype(I32)

    def bit_body(i, thr):
        cand = thr | jnp.left_shift(jnp.int32(1), 30 - i)
        return jnp.where(count_ge(cand) >= topk, cand, thr)

    thr = lax.fori_loop(0, 31, bit_body, thr0)

    n_ge = count_ge(thr)
    tie_lane = (n_ge > topk) & (thr > KEY_NEG_INF)
    has_tie = jnp.max(jnp.where(tie_lane, 1.0, 0.0)) > 0.0

    @pl.when(has_tie)
    def _():
        need = topk - count_ge(thr + 1)
        r_i = lax.broadcasted_iota(I32, (tq, tq), 0)
        c_i = lax.broadcasted_iota(I32, (tq, tq), 1)
        tri = jnp.where(c_i <= r_i, 1.0, 0.0).astype(BF16)

        def tie_body(kb, seen):
            r0 = pl.multiple_of(kb * tq, tq)
            blk = keys_ref[pl.ds(r0, tq), :]
            s_row = r0 + lax.broadcasted_iota(I32, (tq, 1), 0)
            eq = (blk == thr) & (s_row <= t_lane) & tie_lane
            rank = seen + _dot(tri, jnp.where(eq, 1.0, 0.0).astype(BF16))
            keys_ref[pl.ds(r0, tq), :] = jnp.where(eq & (rank > need), thr - 1, blk)
            return rank[tq - 1:tq, :]

        lax.fori_loop(0, j + 1, tie_body, jnp.zeros((1, tq), F32))

    for h in range(nh):
        qh = q_ref[:, h * HEAD_DIM:(h + 1) * HEAD_DIM].astype(BF16)
        qabs_ref[:, h * tq:(h + 1) * tq] = (_dot_nt(wuk_ref[h], qh) * ATT_SCALE).astype(BF16)

    acc_ref[...] = jnp.zeros(acc_ref.shape, F32)

    def attend(g, carry, bias_di):
        m, l = carry
        c0 = pl.multiple_of(g * KEY_GROUP, KEY_GROUP)
        s_all = _dot(ckv_ref[pl.ds(c0, KEY_GROUP), :], qabs_ref[...])
        s_row = c0 + lax.broadcasted_iota(I32, (KEY_GROUP, 1), 0)
        sel = (keys_ref[pl.ds(c0, KEY_GROUP), :] >= thr) & (s_row <= t_lane)
        addm = jnp.where(sel, 0.0, NEG_BIG)
        ps, ms, ls, corrs = [], [], [], []
        for h in range(nh):
            lanes = slice(h * tq, (h + 1) * tq)
            s = s_all[:, lanes] + addm
            if bias_di is not None:
                s = s + bias_ref[bias_di, :, lanes]
            m_prev = m[:, lanes]
            m_new = jnp.maximum(m_prev, jnp.max(s, axis=0, keepdims=True))
            p = jnp.exp(s - m_new)
            corr = jnp.exp(m_prev - m_new)
            ls.append(corr * l[:, lanes] + jnp.sum(p, axis=0, keepdims=True))
            ms.append(m_new)
            corrs.append(corr)
            ps.append(p.astype(BF16))
        pt = jnp.concatenate(ps, axis=1)
        corr_all = jnp.concatenate(corrs, axis=1)
        acc_ref[...] = acc_ref[...] * corr_all + _dot(ckvt_ref[0, :, pl.ds(c0, KEY_GROUP)], pt)
        return jnp.concatenate(ms, axis=1), jnp.concatenate(ls, axis=1)

    carry = (jnp.full((1, nh * tq), NEG_BIG, F32), jnp.zeros((1, nh * tq), F32))
    n_far = jnp.maximum(j // 2 - 1 + j % 2, 0)
    carry = lax.fori_loop(0, n_far, lambda g, cr: attend(g, cr, None), carry)
    carry = lax.cond((j % 2 == 0) & (j >= 2), lambda cr: attend(j // 2 - 1, cr, 2), lambda cr: cr, carry)
    _, l = attend(j // 2, carry, j % 2)

    o_lat = (acc_ref[...] / l).astype(BF16)
    for h in range(nh):
        o_ref[:, h * HEAD_DIM:(h + 1) * HEAD_DIM] = _dot_tn(
            o_lat[:, h * tq:(h + 1) * tq], wuv_ref[h]).astype(o_ref.dtype)


def dsa_attention(proj, ckv, ckvt, k3, qi3, wt, wuk, wuv, bias, *, batch, seq, topk):
    tq = Q_BLOCK
    nq = seq // tq
    nh = N_ATT_HEADS
    r = KV_LORA_RANK
    seq_pad = -(-seq // SCORE_GROUP) * SCORE_GROUP
    return pl.pallas_call(
        functools.partial(_dsa_kernel, topk=topk),
        out_shape=jax.ShapeDtypeStruct((batch * seq, ATT_WIDTH), BF16),
        grid=(batch, nq),
        in_specs=[
            pl.BlockSpec((tq, ATT_WIDTH), lambda b, j: (b * nq + j, 0)),
            _single((seq, r), lambda b, j: (b, 0)),
            _single((1, r, seq), lambda b, j: (b, 0, 0)),
            _single((seq, IDX_K), lambda b, j: (b, 0)),
            pl.BlockSpec((N_IDX_HEADS, tq, IDX_K), lambda b, j: (b * nq + j, 0, 0)),
            pl.BlockSpec((N_IDX_HEADS, tq), lambda b, j: (b * nq + j, 0)),
            _single((nh, r, HEAD_DIM), lambda b, j: (0, 0, 0)),
            _single((nh, r, HEAD_DIM), lambda b, j: (0, 0, 0)),
            _single((3, KEY_GROUP, nh * tq), lambda b, j: (0, 0, 0)),
        ],
        out_specs=pl.BlockSpec((tq, ATT_WIDTH), lambda b, j: (b * nq + j, 0)),
        scratch_shapes=[
            pltpu.VMEM((seq_pad, tq), I32),
            pltpu.VMEM((r, nh * tq), BF16),
            pltpu.VMEM((r, nh * tq), F32),
        ],
        compiler_params=_cparams(("parallel", "arbitrary")),
        name="dsa",
    )(proj, ckv, ckvt, k3, qi3, wt, wuk, wuv, bias)


def _cumsum_rows(x):
    n = x.shape[0]
    row = lax.broadcasted_iota(I32, (n, 1), 0)
    sh = 1
    while sh < n:
        x = x + jnp.where(row >= sh, pltpu.roll(x, sh, 0), 0.0)
        sh *= 2
    return x


def _hgrn2_kernel(q_ref, f_ref, i_ref, g_ref, lb_ref, ng_ref, o_ref, st_ref, *, layer):
    c = REC_CHUNK
    dk = HEAD_DIM

    @pl.when(pl.program_id(2) == 0)
    def _():
        st_ref[...] = jnp.zeros_like(st_ref)

    lbr = lb_ref[...]
    e = jnp.exp(lbr - jnp.max(lbr, axis=0, keepdims=True))
    pr = e / jnp.sum(e, axis=0, keepdims=True)
    lb_all = jnp.zeros((1, lbr.shape[1]), F32)
    for i in range(1, layer + 1):
        lb_all = lb_all + pr[i:i + 1, :]

    row = lax.broadcasted_iota(I32, (c, 1), 0)
    col = lax.broadcasted_iota(I32, (1, c), 1)
    sub = 8
    nsub = c // sub
    tl3 = lax.broadcasted_iota(I32, (1, sub, 1), 1)
    g3 = lax.broadcasted_iota(I32, (nsub, 1, 1), 0)
    col3 = lax.broadcasted_iota(I32, (1, 1, c), 2)
    levels = []
    m = c // 2
    while m >= sub:
        levels.append(m)
        m //= 2

    def head_chunk(r0, hh):
        lanes = slice(hh * dk, (hh + 1) * dk)
        lb = lb_all[:, lanes]
        q = q_ref[pl.ds(r0, c), lanes]
        qf = q * jax.nn.sigmoid(q)
        f = lb + (1.0 - lb) * jax.nn.sigmoid(f_ref[pl.ds(r0, c), lanes])
        k = 1.0 - f
        v = i_ref[pl.ds(r0, c), lanes].astype(BF16)
        cum = _cumsum_rows(jnp.log(f))

        a = jnp.zeros((c, c), F32)
        for m in levels:
            first = (row & (2 * m - 1)) < m
            bnd = jnp.concatenate(
                [jnp.broadcast_to(cum[b * 2 * m + m - 1:b * 2 * m + m, :], (2 * m, dk))
                 for b in range(c // (2 * m))], axis=0)
            qs = jnp.where(first, 0.0, qf * jnp.exp(jnp.minimum(cum - bnd, 0.0))).astype(BF16)
            ks = jnp.where(first, k * jnp.exp(jnp.minimum(bnd - cum, 0.0)), 0.0).astype(BF16)
            same = (row & -(2 * m)) == (col & -(2 * m))
            a = a + jnp.where(same, _dot_nt(qs, ks), 0.0)
        cum3 = cum.reshape(nsub, sub, dk)
        k3 = k.reshape(nsub, sub, dk)
        qf3 = qf.reshape(nsub, sub, dk)
        a3 = jnp.zeros((nsub, sub, c), F32)
        for s in range(sub):
            d = jnp.where(tl3 >= s, cum3 - cum3[:, s:s + 1, :], -jnp.inf)
            prod = qf3 * k3[:, s:s + 1, :] * jnp.exp(d)
            colsum = jnp.sum(prod, axis=-1, keepdims=True)
            a3 = a3 + jnp.where(col3 == g3 * sub + s, colsum, 0.0)
        a = a + a3.reshape(c, c)

        st = st_ref[hh]
        o = _dot(a.astype(BF16), v) + _dot_nt((qf * jnp.exp(cum)).astype(BF16), st.astype(BF16))
        last = cum[c - 1:c, :]
        kl = (k * jnp.exp(last - cum)).astype(BF16)
        st_ref[hh] = st * jnp.exp(last) + _dot_tn(v, kl)

        ms = jnp.mean(o * o, axis=-1, keepdims=True)
        gt = g_ref[pl.ds(r0, c), lanes]
        on = o * lax.rsqrt(ms + LN_EPS) * ng_ref[:, lanes]
        o_ref[pl.ds(r0, c), lanes] = (on * (gt * jax.nn.sigmoid(gt))).astype(o_ref.dtype)

    def chunk_body(ci, carry):
        r0 = pl.multiple_of(ci * c, c)
        for hh in range(REC_HEADS_PER_STEP):
            head_chunk(r0, hh)
        return carry

    lax.fori_loop(0, q_ref.shape[0] // c, chunk_body, 0)


def hgrn2(proj, rec_lb, norm_g, *, layer, batch, seq):
    tb = min(REC_TBLOCK, seq)
    nt = seq // tb
    hw = REC_HEADS_PER_STEP * HEAD_DIM
    ng = N_REC_HEADS // REC_HEADS_PER_STEP

    def seg(k):
        return pl.BlockSpec((tb, hw), lambda b, h, t, k=k: (b * nt + t, k * ng + h))

    return pl.pallas_call(
        functools.partial(_hgrn2_kernel, layer=layer),
        out_shape=jax.ShapeDtypeStruct((batch * seq, REC_WIDTH), BF16),
        grid=(batch, ng, nt),
        in_specs=[
            seg(0), seg(1), seg(2), seg(3),
            pl.BlockSpec((rec_lb.shape[0], hw), lambda b, h, t: (0, h)),
            pl.BlockSpec((1, hw), lambda b, h, t: (0, h)),
        ],
        out_specs=pl.BlockSpec((tb, hw), lambda b, h, t: (b * nt + t, h)),
        scratch_shapes=[pltpu.VMEM((REC_HEADS_PER_STEP, HEAD_DIM, HEAD_DIM), F32)],
        compiler_params=_cparams(("parallel", "parallel", "arbitrary")),
        name="hgrn2",
    )(proj, proj, proj, proj, rec_lb, norm_g)


def _pack_w_in(w):
    kv0 = ATT_WIDTH
    qi0 = kv0 + KV_LORA_RANK
    sm0 = qi0 + IDX_WIDTH
    sm1 = sm0 + IDX_DIM + N_IDX_HEADS
    pad = jnp.zeros((w.shape[0], SMALL_WIDTH - IDX_DIM - N_IDX_HEADS), BF16)
    parts = [w[:, :kv0], w[:, qi0:sm0], w[:, kv0:qi0], w[:, sm0:sm1]]
    w_att = jnp.concatenate([p.astype(BF16) for p in parts] + [pad], axis=1)
    return w_att, w[:, sm1:].astype(BF16)


def _row_tile(m, want):
    t = min(want, m)
    while m % t:
        t //= 2
    return t


def kernel(x, mem, w_in, w_uk, w_uv, kv_norm_g, idx_kn_g, idx_kn_b, rec_lb, rec_norm_g, w_out, rel_bias,
           wq_c, wk_c, wv_c, wo_c, ffn_gate, ffn_up, ffn_down, ln_g, ln_b):
    batch, seq, d = x.shape
    depth = w_in.shape[0]
    m = batch * seq
    alpha = (2 * depth) ** 0.25
    topk = min(TOPK_MAX, seq // 4)
    d_ff = ffn_gate.shape[-1]
    tf = 256 if d_ff % 256 == 0 else 128

    xf = x.reshape(m, d)
    mem_f = mem.reshape(batch * mem.shape[1], d)
    bias = bias_tables(rel_bias)
    rec_lb = rec_lb.astype(F32)

    ffn_order = [(l, i) for l in range(depth) for i in range(2)]
    ffn_w = {ffn_order[0]: tuple(w[0, 0].astype(BF16) for w in (ffn_gate, ffn_up, ffn_down))}

    def ffn(xf, l, i):
        pos = ffn_order.index((l, i))
        nxt = ffn_order[pos + 1] if pos + 1 < len(ffn_order) else None
        out, cast = ffn_ln(xf, *ffn_w.pop((l, i)), ln_g[l, 3 * i:3 * i + 1], ln_b[l, 3 * i:3 * i + 1],
                           alpha=alpha, tm=_row_tile(m, 512), tf=tf,
                           cast_next=None if nxt is None else (ffn_gate, ffn_up, ffn_down) + nxt)
        if nxt is not None:
            ffn_w[nxt] = cast
        return out

    for l in range(depth):
        xf = ffn(xf, l, 0)

        w_att, w_rec = _pack_w_in(w_in[l])
        proj_a = matmul(xf, w_att, tm=_row_tile(m, 512), tn=768, out_dtype=F32)
        proj_r = matmul(xf, w_rec, tm=_row_tile(m, 512), tn=1024, out_dtype=F32)
        ckv, ckvt, k3, qi3, wt = dsa_prep(proj_a, kv_norm_g[l][None], idx_kn_g[l][None], idx_kn_b[l][None], seq=seq)
        o_att = dsa_attention(proj_a, ckv, ckvt, k3, qi3, wt, w_uk[l].astype(BF16), w_uv[l].astype(BF16), bias,
                              batch=batch, seq=seq, topk=topk)
        o_rec = hgrn2(proj_r, rec_lb, rec_norm_g[l][None], layer=l, batch=batch, seq=seq)
        xf = proj_ln(o_att, o_rec, w_out[l].astype(BF16), xf, ln_g[l, 1:2], ln_b[l, 1:2],
                     alpha=alpha, tm=_row_tile(m, 512), tn=512)

        kv = matmul(mem_f, jnp.concatenate([wk_c[l], wv_c[l]], axis=1).astype(BF16),
                    tm=_row_tile(mem_f.shape[0], 512), tn=256, out_dtype=BF16)
        kv = kv.reshape(batch, mem.shape[1], kv.shape[-1])
        xf = cross_ln(xf, wq_c[l].astype(BF16), kv, wo_c[l].astype(BF16), ln_g[l, 2:3], ln_b[l, 2:3],
                      alpha=alpha, tm=_row_tile(seq, 256), rows_per_batch=seq)

        xf = ffn(xf, l, 1)
    return xf.reshape(batch, seq, d)
```

```python
import functools
import math

import jax
import jax.numpy as jnp
import numpy as np
from jax import lax
from jax.experimental import pallas as pl
from jax.experimental.pallas import tpu as pltpu

F32 = jnp.float32
BF16 = jnp.bfloat16
I32 = jnp.int32

LANE = 128
VMEM_LIMIT_V7X = 60 * 1024 * 1024

HEAD_DIM = 128
N_ATT_HEADS = 16
N_REC_HEADS = 16
KV_LORA_RANK = 512
N_IDX_HEADS = 16
IDX_DIM = 64
TOPK_MAX = 256
Q_BLOCK = 128
N_BUCKETS = 32
MAX_DISTANCE = 128
N_CROSS_HEADS = 4
LN_EPS = 1e-5
ATT_SCALE = HEAD_DIM ** -0.5
IDX_SCALE = IDX_DIM ** -0.5
IDX_HEAD_SCALE = N_IDX_HEADS ** -0.5

ATT_WIDTH = N_ATT_HEADS * HEAD_DIM
IDX_WIDTH = N_IDX_HEADS * IDX_DIM
REC_WIDTH = N_REC_HEADS * HEAD_DIM
COL_QIDX = ATT_WIDTH
COL_KV = COL_QIDX + IDX_WIDTH
COL_SMALL = COL_KV + KV_LORA_RANK
SMALL_WIDTH = 256
COL_REC = COL_SMALL + SMALL_WIDTH

NEG_BIG = -1e30
INT_MIN = -2 ** 31
KEY_NEG_INF = (0xFF800000 - 2 ** 32) ^ 0x7FFFFFFF

REC_CHUNK = 64
REC_TBLOCK = 512
REC_HEADS_PER_STEP = 4
KEY_GROUP = 256
N_BIAS_TABLES = 3
SCORE_GROUP = 512
IDX_K = 256
COUNT_ROWS = 64
LN_ROWS = 128
FFN_CAST_EVERY = 4


def _cparams(sem):
    return pltpu.CompilerParams(dimension_semantics=sem, vmem_limit_bytes=VMEM_LIMIT_V7X)


def _single(shape, imap):
    return pl.BlockSpec(shape, imap, pipeline_mode=pl.Buffered(1))


def _layer_norm_rows(y, g, b):
    mu = jnp.mean(y, axis=-1, keepdims=True)
    d = y - mu
    var = jnp.mean(d * d, axis=-1, keepdims=True)
    return d * lax.rsqrt(var + LN_EPS) * g + b


def _residual_ln_inplace(x_ref, o_ref, g_ref, b_ref, *, alpha, scale):
    g = g_ref[...]
    b = b_ref[...]

    def body(i, carry):
        rows = pl.ds(pl.multiple_of(i * LN_ROWS, LN_ROWS), LN_ROWS)
        y = alpha * x_ref[rows, :] + scale * o_ref[rows, :]
        o_ref[rows, :] = _layer_norm_rows(y, g, b)
        return carry

    lax.fori_loop(0, o_ref.shape[0] // LN_ROWS, body, 0)


def _dot(a, b):
    return jnp.dot(a, b, preferred_element_type=F32)


def _dot_nt(a, b):
    return lax.dot_general(a, b, (((1,), (1,)), ((), ())), preferred_element_type=F32)


def _dot_tn(a, b):
    return lax.dot_general(a, b, (((0,), (0,)), ((), ())), preferred_element_type=F32)


def _ffn_ln_kernel(x_ref, wg_ref, wu_ref, wd_ref, g_ref, b_ref, *rest, alpha, n_cast):
    cast_in = rest[:n_cast]
    o_ref = rest[n_cast]
    cast_out = rest[n_cast + 1:2 * n_cast + 1]
    xb_ref = rest[2 * n_cast + 1]
    f = pl.program_id(1)

    @pl.when(f == 0)
    def _():
        xb_ref[...] = x_ref[...].astype(BF16)
        o_ref[...] = jnp.zeros_like(o_ref)

    xb = xb_ref[...]
    h = _dot(xb, wg_ref[...])
    u = _dot(xb, wu_ref[...])
    a = (h * jax.nn.sigmoid(h) * u).astype(BF16)
    o_ref[...] += _dot(a, wd_ref[...])

    if cast_in:
        @pl.when(f % FFN_CAST_EVERY == 0)
        def _():
            for src, dst in zip(cast_in, cast_out):
                dst[...] = src[0, 0].astype(BF16)

    @pl.when(f == pl.num_programs(1) - 1)
    def _():
        _residual_ln_inplace(x_ref, o_ref, g_ref, b_ref, alpha=alpha, scale=0.5)


def ffn_ln(x, wg, wu, wd, g, b, *, alpha, tm, tf, cast_next=None):
    m, d = x.shape
    d_ff = wg.shape[1]
    ni, nf = m // tm, d_ff // tf
    in_specs = [
        pl.BlockSpec((tm, d), lambda i, f: (i, 0)),
        pl.BlockSpec((d, tf), lambda i, f: (0, f)),
        pl.BlockSpec((d, tf), lambda i, f: (0, f)),
        pl.BlockSpec((tf, d), lambda i, f: (f, 0)),
        pl.BlockSpec((1, d), lambda i, f: (0, 0)),
        pl.BlockSpec((1, d), lambda i, f: (0, 0)),
    ]
    out_shape = [jax.ShapeDtypeStruct((m, d), F32)]
    out_specs = [pl.BlockSpec((tm, d), lambda i, f: (i, 0))]
    operands = [x, wg, wu, wd, g, b]
    n_cast = 0
    if cast_next is not None:
        ng, nu, nd, nl, nw = cast_next
        td = d // ni
        tc = FFN_CAST_EVERY * tf
        n_cast = 3
        in_specs += [
            pl.BlockSpec((1, 1, td, tc), lambda i, f: (nl, nw, i, f // FFN_CAST_EVERY)),
            pl.BlockSpec((1, 1, td, tc), lambda i, f: (nl, nw, i, f // FFN_CAST_EVERY)),
            pl.BlockSpec((1, 1, tc, td), lambda i, f: (nl, nw, f // FFN_CAST_EVERY, i)),
        ]
        out_shape += [jax.ShapeDtypeStruct((d, d_ff), BF16), jax.ShapeDtypeStruct((d, d_ff), BF16),
                      jax.ShapeDtypeStruct((d_ff, d), BF16)]
        out_specs += [
            pl.BlockSpec((td, tc), lambda i, f: (i, f // FFN_CAST_EVERY)),
            pl.BlockSpec((td, tc), lambda i, f: (i, f // FFN_CAST_EVERY)),
            pl.BlockSpec((tc, td), lambda i, f: (f // FFN_CAST_EVERY, i)),
        ]
        operands += [ng, nu, nd]
    outs = pl.pallas_call(
        functools.partial(_ffn_ln_kernel, alpha=alpha, n_cast=n_cast),
        out_shape=tuple(out_shape),
        grid=(ni, nf),
        in_specs=in_specs,
        out_specs=tuple(out_specs),
        scratch_shapes=[pltpu.VMEM((tm, d), BF16)],
        compiler_params=_cparams(("parallel", "arbitrary")),
        name="ffn_ln",
    )(*operands)
    return outs[0], tuple(outs[1:])


def _matmul_kernel(x_ref, w_ref, o_ref, xb_ref):
    @pl.when(pl.program_id(1) == 0)
    def _():
        xb_ref[...] = x_ref[...].astype(BF16)

    o_ref[...] = _dot(xb_ref[...], w_ref[...]).astype(o_ref.dtype)


def matmul(x, w, *, tm, tn, out_dtype):
    m, k = x.shape
    n = w.shape[1]
    return pl.pallas_call(
        _matmul_kernel,
        out_shape=jax.ShapeDtypeStruct((m, n), out_dtype),
        grid=(m // tm, n // tn),
        in_specs=[
            pl.BlockSpec((tm, k), lambda i, j: (i, 0)),
            pl.BlockSpec((k, tn), lambda i, j: (0, j)),
        ],
        out_specs=pl.BlockSpec((tm, tn), lambda i, j: (i, j)),
        scratch_shapes=[pltpu.VMEM((tm, k), BF16)],
        compiler_params=_cparams(("parallel", "arbitrary")),
        name="matmul",
    )(x, w)


def _proj_ln_kernel(a1_ref, a2_ref, w1_ref, w2_ref, x_ref, g_ref, b_ref, o_ref, *, alpha, tn):
    n = pl.program_id(1)
    c0 = pl.multiple_of(n * tn, tn)
    o_ref[:, pl.ds(c0, tn)] = _dot(a1_ref[...], w1_ref[...]) + _dot(a2_ref[...], w2_ref[...])

    @pl.when(n == pl.num_programs(1) - 1)
    def _():
        _residual_ln_inplace(x_ref, o_ref, g_ref, b_ref, alpha=alpha, scale=1.0)


def proj_ln(a1, a2, w, x, g, b, *, alpha, tm, tn):
    m, kh = a1.shape
    d = w.shape[1]
    return pl.pallas_call(
        functools.partial(_proj_ln_kernel, alpha=alpha, tn=tn),
        out_shape=jax.ShapeDtypeStruct((m, d), F32),
        grid=(m // tm, d // tn),
        in_specs=[
            pl.BlockSpec((tm, kh), lambda i, n: (i, 0)),
            pl.BlockSpec((tm, kh), lambda i, n: (i, 0)),
            pl.BlockSpec((kh, tn), lambda i, n: (0, n)),
            pl.BlockSpec((kh, tn), lambda i, n: (1, n)),
            pl.BlockSpec((tm, d), lambda i, n: (i, 0)),
            pl.BlockSpec((1, d), lambda i, n: (0, 0)),
            pl.BlockSpec((1, d), lambda i, n: (0, 0)),
        ],
        out_specs=pl.BlockSpec((tm, d), lambda i, n: (i, 0)),
        compiler_params=_cparams(("parallel", "arbitrary")),
        name="proj_ln",
    )(a1, a2, w, w, x, g, b)


def _cross_ln_kernel(x_ref, wq_ref, kv_ref, wo_ref, g_ref, b_ref, o_ref, *, alpha, n_heads):
    x = x_ref[...]
    q = _dot(x.astype(BF16), wq_ref[...])
    kv = kv_ref[0]
    cw = n_heads * HEAD_DIM
    outs = []
    for h in range(n_heads):
        qh = q[:, h * HEAD_DIM:(h + 1) * HEAD_DIM].astype(BF16)
        kh = kv[:, h * HEAD_DIM:(h + 1) * HEAD_DIM]
        vh = kv[:, cw + h * HEAD_DIM:cw + (h + 1) * HEAD_DIM]
        logits = _dot_nt(qh, kh) * ATT_SCALE
        mx = jnp.max(logits, axis=-1, keepdims=True)
        e = jnp.exp(logits - mx)
        p = e / jnp.sum(e, axis=-1, keepdims=True)
        outs.append(_dot(p.astype(BF16), vh))
    o = jnp.concatenate(outs, axis=-1).astype(BF16)
    o_ref[...] = _dot(o, wo_ref[...])
    _residual_ln_inplace(x_ref, o_ref, g_ref, b_ref, alpha=alpha, scale=1.0)


def cross_ln(x, wq, kv, wo, g, b, *, alpha, tm, rows_per_batch):
    m, d = x.shape
    cw = wq.shape[1]
    nmem = kv.shape[1]
    steps_per_batch = rows_per_batch // tm
    return pl.pallas_call(
        functools.partial(_cross_ln_kernel, alpha=alpha, n_heads=cw // HEAD_DIM),
        out_shape=jax.ShapeDtypeStruct((m, d), F32),
        grid=(m // tm,),
        in_specs=[
            pl.BlockSpec((tm, d), lambda i: (i, 0)),
            _single((d, cw), lambda i: (0, 0)),
            pl.BlockSpec((1, nmem, 2 * cw), lambda i: (i // steps_per_batch, 0, 0)),
            _single((cw, d), lambda i: (0, 0)),
            pl.BlockSpec((1, d), lambda i: (0, 0)),
            pl.BlockSpec((1, d), lambda i: (0, 0)),
        ],
        out_specs=pl.BlockSpec((tm, d), lambda i: (i, 0)),
        compiler_params=_cparams(("parallel",)),
        name="cross_ln",
    )(x, wq, kv, wo, g, b)


def _split3(x, order):
    hi = x.astype(BF16).astype(F32)
    lo = x - hi
    parts = [hi if c == "h" else lo for c in order]
    parts.append(jnp.zeros((x.shape[0], IDX_K - 3 * x.shape[1]), F32))
    return jnp.concatenate(parts, axis=-1).astype(BF16)


def _dsa_prep_kernel(kv_ref, qi_ref, sm_ref, kvg_ref, kng_ref, knb_ref,
                     ckv_ref, ckvt_ref, k3_ref, qi3_ref, wt_ref):
    kv = kv_ref[...]
    ms = jnp.mean(kv * kv, axis=-1, keepdims=True)
    ckv = kv * lax.rsqrt(ms + LN_EPS) * kvg_ref[...]
    ckv_ref[...] = ckv.astype(BF16)
    ckvt_ref[0] = jnp.transpose(ckv).astype(BF16)

    sm = sm_ref[...]
    kidx = _layer_norm_rows(sm[:, :IDX_DIM], kng_ref[...], knb_ref[...])
    k3_ref[...] = _split3(kidx, "hlh")

    wt = jnp.transpose(sm)
    wt_ref[...] = wt[IDX_DIM:IDX_DIM + N_IDX_HEADS, :] * (IDX_HEAD_SCALE * IDX_SCALE)

    qi = qi_ref[...]
    for h in range(N_IDX_HEADS):
        qi3_ref[h] = _split3(qi[:, h * IDX_DIM:(h + 1) * IDX_DIM], "hhl")


def dsa_prep(proj, kv_g, kn_g, kn_b, *, seq):
    m = proj.shape[0]
    tq = Q_BLOCK
    nq = m // tq
    nqb = seq // tq
    return pl.pallas_call(
        _dsa_prep_kernel,
        out_shape=(
            jax.ShapeDtypeStruct((m, KV_LORA_RANK), BF16),
            jax.ShapeDtypeStruct((m // seq, KV_LORA_RANK, seq), BF16),
            jax.ShapeDtypeStruct((m, IDX_K), BF16),
            jax.ShapeDtypeStruct((nq * N_IDX_HEADS, tq, IDX_K), BF16),
            jax.ShapeDtypeStruct((nq * N_IDX_HEADS, tq), F32),
        ),
        grid=(nq,),
        in_specs=[
            pl.BlockSpec((tq, KV_LORA_RANK), lambda i: (i, COL_KV // KV_LORA_RANK)),
            pl.BlockSpec((tq, IDX_WIDTH), lambda i: (i, COL_QIDX // IDX_WIDTH)),
            pl.BlockSpec((tq, LANE), lambda i: (i, COL_SMALL // LANE)),
            pl.BlockSpec((1, KV_LORA_RANK), lambda i: (0, 0)),
            pl.BlockSpec((1, IDX_DIM), lambda i: (0, 0)),
            pl.BlockSpec((1, IDX_DIM), lambda i: (0, 0)),
        ],
        out_specs=(
            pl.BlockSpec((tq, KV_LORA_RANK), lambda i: (i, 0)),
            pl.BlockSpec((1, KV_LORA_RANK, tq), lambda i: (i // nqb, 0, i % nqb)),
            pl.BlockSpec((tq, IDX_K), lambda i: (i, 0)),
            pl.BlockSpec((N_IDX_HEADS, tq, IDX_K), lambda i: (i, 0, 0)),
            pl.BlockSpec((N_IDX_HEADS, tq), lambda i: (i, 0)),
        ),
        compiler_params=_cparams(("parallel",)),
        name="dsa_prep",
    )(proj, proj, proj, kv_g, kn_g, kn_b)


def _bucket_thresholds():
    max_exact = N_BUCKETS // 2
    rel = np.arange(max_exact, 4 * MAX_DISTANCE, dtype=np.float32)
    large = max_exact + (np.log(rel / np.float32(max_exact)) / np.float32(math.log(MAX_DISTANCE / max_exact))
                         * np.float32(N_BUCKETS - max_exact)).astype(np.int32)
    large = np.minimum(large, N_BUCKETS - 1)
    return [int(rel[np.argmax(large >= bkt)]) for bkt in range(max_exact + 1, N_BUCKETS)]


def _bias_table_kernel(rb_ref, o_ref):
    di = pl.program_id(0)
    max_exact = N_BUCKETS // 2
    sl = lax.broadcasted_iota(I32, (KEY_GROUP, Q_BLOCK), 0)
    tl = lax.broadcasted_iota(I32, (KEY_GROUP, Q_BLOCK), 1)
    rel = jnp.maximum(di * Q_BLOCK + tl - sl, 0)
    large = jnp.full(rel.shape, max_exact, I32)
    for th in _bucket_thresholds():
        large = large + (rel >= th).astype(I32)
    bucket = jnp.where(rel < max_exact, rel, large)
    for h in range(N_ATT_HEADS):
        far = rb_ref[N_BUCKETS - 1, h]
        val = jnp.zeros(rel.shape, F32)
        for bkt in range(N_BUCKETS - 1):
            val = jnp.where(bucket == bkt, rb_ref[bkt, h] - far, val)
        o_ref[0, :, h * Q_BLOCK:(h + 1) * Q_BLOCK] = val


def bias_tables(rel_bias):
    nh = rel_bias.shape[1]
    return pl.pallas_call(
        _bias_table_kernel,
        out_shape=jax.ShapeDtypeStruct((N_BIAS_TABLES, KEY_GROUP, nh * Q_BLOCK), F32),
        grid=(N_BIAS_TABLES,),
        in_specs=[pl.BlockSpec(memory_space=pltpu.SMEM)],
        out_specs=pl.BlockSpec((1, KEY_GROUP, nh * Q_BLOCK), lambda d: (d, 0, 0)),
        compiler_params=_cparams(("arbitrary",)),
        name="bias_tables",
    )(rel_bias)


def _dsa_kernel(q_ref, ckv_ref, ckvt_ref, k3_ref, qi3_ref, wt_ref, wuk_ref, wuv_ref, bias_ref, o_ref,
                keys_ref, qabs_ref, acc_ref, *, topk):
    j = pl.program_id(1)
    nh = N_ATT_HEADS
    tq = Q_BLOCK
    t_lane = j * tq + lax.broadcasted_iota(I32, (1, tq), 1)

    n_sgrp = (j + 1 + (SCORE_GROUP // tq - 1)) // (SCORE_GROUP // tq)
    qi3 = qi3_ref[...].reshape(N_IDX_HEADS * tq, IDX_K)

    def score_body(g, carry):
        r0 = pl.multiple_of(g * SCORE_GROUP, SCORE_GROUP)
        lt = _dot_nt(k3_ref[pl.ds(r0, SCORE_GROUP), :], qi3)
        sc = jnp.zeros((SCORE_GROUP, tq), F32)
        for h in range(N_IDX_HEADS):
            sc = sc + jnp.maximum(lt[:, h * tq:(h + 1) * tq], 0.0) * wt_ref[h:h + 1, :]
        s_row = r0 + lax.broadcasted_iota(I32, (SCORE_GROUP, 1), 0)
        sc = jnp.where(s_row <= t_lane, sc + 0.0, -jnp.inf)
        bits = pltpu.bitcast(sc, I32)
        keys_ref[pl.ds(r0, SCORE_GROUP), :] = jnp.where(bits < 0, bits ^ 0x7FFFFFFF, bits)
        return carry

    lax.fori_loop(0, n_sgrp, score_body, 0)

    def count_ge(cand):
        def body(g, part):
            r0 = pl.multiple_of(g * SCORE_GROUP, SCORE_GROUP)
            ind = jnp.where(keys_ref[pl.ds(r0, SCORE_GROUP), :] >= cand, 1.0, 0.0)
            return part + jnp.sum(ind.reshape(SCORE_GROUP // COUNT_ROWS, COUNT_ROWS, tq), axis=0)
        part = lax.fori_loop(0, n_sgrp, body, jnp.zeros((COUNT_ROWS, tq), F32))
        return jnp.sum(part, axis=0, keepdims=True)

    thr0 = jnp.where(count_ge(jnp.zeros((1, tq), I32)) >= topk, 0, INT_MIN).astype(I32)

    def bit_body(i, thr):
        cand = thr | jnp.left_shift(jnp.int32(1), 30 - i)
        return jnp.where(count_ge(cand) >= topk, cand, thr)

    thr = lax.fori_loop(0, 31, bit_body, thr0)

    n_ge = count_ge(thr)
    tie_lane = (n_ge > topk) & (thr > KEY_NEG_INF)
    has_tie = jnp.max(jnp.where(tie_lane, 1.0, 0.0)) > 0.0

    @pl.when(has_tie)
    def _():
        need = topk - count_ge(thr + 1)
        r_i = lax.broadcasted_iota(I32, (tq, tq), 0)
        c_i = lax.broadcasted_iota(I32, (tq, tq), 1)
        tri = jnp.where(c_i <= r_i, 1.0, 0.0).astype(BF16)

        def tie_body(kb, seen):
            r0 = pl.multiple_of(kb * tq, tq)
            blk = keys_ref[pl.ds(r0, tq), :]
            s_row = r0 + lax.broadcasted_iota(I32, (tq, 1), 0)
            eq = (blk == thr) & (s_row <= t_lane) & tie_lane
            rank = seen + _dot(tri, jnp.where(eq, 1.0, 0.0).astype(BF16))
            keys_ref[pl.ds(r0, tq), :] = jnp.where(eq & (rank > need), thr - 1, blk)
            return rank[tq - 1:tq, :]

        lax.fori_loop(0, j + 1, tie_body, jnp.zeros((1, tq), F32))

    for h in range(nh):
        qh = q_ref[:, h * HEAD_DIM:(h + 1) * HEAD_DIM].astype(BF16)
        qabs_ref[:, h * tq:(h + 1) * tq] = (_dot_nt(wuk_ref[h], qh) * ATT_SCALE).astype(BF16)

    acc_ref[...] = jnp.zeros(acc_ref.shape, F32)

    def attend(g, carry, bias_di):
        m, l = carry
        c0 = pl.multiple_of(g * KEY_GROUP, KEY_GROUP)
        s_all = _dot(ckv_ref[pl.ds(c0, KEY_GROUP), :], qabs_ref[...])
        s_row = c0 + lax.broadcasted_iota(I32, (KEY_GROUP, 1), 0)
        sel = (keys_ref[pl.ds(c0, KEY_GROUP), :] >= thr) & (s_row <= t_lane)
        addm = jnp.where(sel, 0.0, NEG_BIG)
        ps, ms, ls, corrs = [], [], [], []
        for h in range(nh):
            lanes = slice(h * tq, (h + 1) * tq)
            s = s_all[:, lanes] + addm
            if bias_di is not None:
                s = s + bias_ref[bias_di, :, lanes]
            m_prev = m[:, lanes]
            m_new = jnp.maximum(m_prev, jnp.max(s, axis=0, keepdims=True))
            p = jnp.exp(s - m_new)
            corr = jnp.exp(m_prev - m_new)
            ls.append(corr * l[:, lanes] + jnp.sum(p, axis=0, keepdims=True))
            ms.append(m_new)
            corrs.append(corr)
            ps.append(p.astype(BF16))
        pt = jnp.concatenate(ps, axis=1)
        corr_all = jnp.concatenate(corrs, axis=1)
        acc_ref[...] = acc_ref[...] * corr_all + _dot(ckvt_ref[0, :, pl.ds(c0, KEY_GROUP)], pt)
        return jnp.concatenate(ms, axis=1), jnp.concatenate(ls, axis=1)

    carry = (jnp.full((1, nh * tq), NEG_BIG, F32), jnp.zeros((1, nh * tq), F32))
    n_far = jnp.maximum(j // 2 - 1 + j % 2, 0)
    carry = lax.fori_loop(0, n_far, lambda g, cr: attend(g, cr, None), carry)
    carry = lax.cond((j % 2 == 0) & (j >= 2), lambda cr: attend(j // 2 - 1, cr, 2), lambda cr: cr, carry)
    _, l = attend(j // 2, carry, j % 2)

    o_lat = (acc_ref[...] / l).astype(BF16)
    for h in range(nh):
        o_ref[:, h * HEAD_DIM:(h + 1) * HEAD_DIM] = _dot_tn(
            o_lat[:, h * tq:(h + 1) * tq], wuv_ref[h]).astype(o_ref.dtype)


def dsa_attention(proj, ckv, ckvt, k3, qi3, wt, wuk, wuv, bias, *, batch, seq, topk):
    tq = Q_BLOCK
    nq = seq // tq
    nh = N_ATT_HEADS
    r = KV_LORA_RANK
    seq_pad = -(-seq // SCORE_GROUP) * SCORE_GROUP
    return pl.pallas_call(
        functools.partial(_dsa_kernel, topk=topk),
        out_shape=jax.ShapeDtypeStruct((batch * seq, ATT_WIDTH), BF16),
        grid=(batch, nq),
        in_specs=[
            pl.BlockSpec((tq, ATT_WIDTH), lambda b, j: (b * nq + j, 0)),
            _single((seq, r), lambda b, j: (b, 0)),
            _single((1, r, seq), lambda b, j: (b, 0, 0)),
            _single((seq, IDX_K), lambda b, j: (b, 0)),
            pl.BlockSpec((N_IDX_HEADS, tq, IDX_K), lambda b, j: (b * nq + j, 0, 0)),
            pl.BlockSpec((N_IDX_HEADS, tq), lambda b, j: (b * nq + j, 0)),
            _single((nh, r, HEAD_DIM), lambda b, j: (0, 0, 0)),
            _single((nh, r, HEAD_DIM), lambda b, j: (0, 0, 0)),
            _single((N_BIAS_TABLES, KEY_GROUP, nh * tq), lambda b, j: (0, 0, 0)),
        ],
        out_specs=pl.BlockSpec((tq, ATT_WIDTH), lambda b, j: (b * nq + j, 0)),
        scratch_shapes=[
            pltpu.VMEM((seq_pad, tq), I32),
            pltpu.VMEM((r, nh * tq), BF16),
            pltpu.VMEM((r, nh * tq), F32),
        ],
        compiler_params=_cparams(("parallel", "arbitrary")),
        name="dsa",
    )(proj, ckv, ckvt, k3, qi3, wt, wuk, wuv, bias)


def _cumsum_rows(x):
    n = x.shape[0]
    row = lax.broadcasted_iota(I32, (n, 1), 0)
    sh = 1
    while sh < n:
        x = x + jnp.where(row >= sh, pltpu.roll(x, sh, 0), 0.0)
        sh *= 2
    return x


def _hgrn2_kernel(q_ref, f_ref, i_ref, g_ref, lb_ref, ng_ref, o_ref, st_ref, *, layer):
    c = REC_CHUNK
    dk = HEAD_DIM

    @pl.when(pl.program_id(2) == 0)
    def _():
        st_ref[...] = jnp.zeros_like(st_ref)

    lbr = lb_ref[...]
    e = jnp.exp(lbr - jnp.max(lbr, axis=0, keepdims=True))
    pr = e / jnp.sum(e, axis=0, keepdims=True)
    lb_all = jnp.zeros((1, lbr.shape[1]), F32)
    for i in range(1, layer + 1):
        lb_all = lb_all + pr[i:i + 1, :]

    row = lax.broadcasted_iota(I32, (c, 1), 0)
    col = lax.broadcasted_iota(I32, (1, c), 1)
    sub = 8
    nsub = c // sub
    tl3 = lax.broadcasted_iota(I32, (1, sub, 1), 1)
    g3 = lax.broadcasted_iota(I32, (nsub, 1, 1), 0)
    col3 = lax.broadcasted_iota(I32, (1, 1, c), 2)
    levels = []
    m = c // 2
    while m >= sub:
        levels.append(m)
        m //= 2

    def head_chunk(r0, hh):
        lanes = slice(hh * dk, (hh + 1) * dk)
        lb = lb_all[:, lanes]
        q = q_ref[pl.ds(r0, c), lanes]
        qf = q * jax.nn.sigmoid(q)
        f = lb + (1.0 - lb) * jax.nn.sigmoid(f_ref[pl.ds(r0, c), lanes])
        k = 1.0 - f
        v = i_ref[pl.ds(r0, c), lanes].astype(BF16)
        cum = _cumsum_rows(jnp.log(f))

        a = jnp.zeros((c, c), F32)
        for m in levels:
            first = (row & (2 * m - 1)) < m
            bnd = jnp.concatenate(
                [jnp.broadcast_to(cum[b * 2 * m + m - 1:b * 2 * m + m, :], (2 * m, dk))
                 for b in range(c // (2 * m))], axis=0)
            qs = jnp.where(first, 0.0, qf * jnp.exp(jnp.minimum(cum - bnd, 0.0))).astype(BF16)
            ks = jnp.where(first, k * jnp.exp(jnp.minimum(bnd - cum, 0.0)), 0.0).astype(BF16)
            same = (row & -(2 * m)) == (col & -(2 * m))
            a = a + jnp.where(same, _dot_nt(qs, ks), 0.0)
        cum3 = cum.reshape(nsub, sub, dk)
        k3 = k.reshape(nsub, sub, dk)
        qf3 = qf.reshape(nsub, sub, dk)
        a3 = jnp.zeros((nsub, sub, c), F32)
        for s in range(sub):
            d = jnp.where(tl3 >= s, cum3 - cum3[:, s:s + 1, :], -jnp.inf)
            prod = qf3 * k3[:, s:s + 1, :] * jnp.exp(d)
            colsum = jnp.sum(prod, axis=-1, keepdims=True)
            a3 = a3 + jnp.where(col3 == g3 * sub + s, colsum, 0.0)
        a = a + a3.reshape(c, c)

        st = st_ref[hh]
        o = _dot(a.astype(BF16), v) + _dot_nt((qf * jnp.exp(cum)).astype(BF16), st.astype(BF16))
        last = cum[c - 1:c, :]
        kl = (k * jnp.exp(last - cum)).astype(BF16)
        st_ref[hh] = st * jnp.exp(last) + _dot_tn(v, kl)

        ms = jnp.mean(o * o, axis=-1, keepdims=True)
        gt = g_ref[pl.ds(r0, c), lanes]
        on = o * lax.rsqrt(ms + LN_EPS) * ng_ref[:, lanes]
        o_ref[pl.ds(r0, c), lanes] = (on * (gt * jax.nn.sigmoid(gt))).astype(o_ref.dtype)

    def chunk_body(ci, carry):
        r0 = pl.multiple_of(ci * c, c)
        for hh in range(REC_HEADS_PER_STEP):
            head_chunk(r0, hh)
        return carry

    lax.fori_loop(0, q_ref.shape[0] // c, chunk_body, 0)


def hgrn2(proj, rec_lb, norm_g, *, layer, batch, seq):
    tb = min(REC_TBLOCK, seq)
    nt = seq // tb
    hw = REC_HEADS_PER_STEP * HEAD_DIM
    ng = N_REC_HEADS // REC_HEADS_PER_STEP

    def seg(k):
        return pl.BlockSpec((tb, hw), lambda b, h, t, k=k: (b * nt + t, k * ng + h))

    return pl.pallas_call(
        functools.partial(_hgrn2_kernel, layer=layer),
        out_shape=jax.ShapeDtypeStruct((batch * seq, REC_WIDTH), BF16),
        grid=(batch, ng, nt),
        in_specs=[
            seg(0), seg(1), seg(2), seg(3),
            pl.BlockSpec((rec_lb.shape[0], hw), lambda b, h, t: (0, h)),
            pl.BlockSpec((1, hw), lambda b, h, t: (0, h)),
        ],
        out_specs=pl.BlockSpec((tb, hw), lambda b, h, t: (b * nt + t, h)),
        scratch_shapes=[pltpu.VMEM((REC_HEADS_PER_STEP, HEAD_DIM, HEAD_DIM), F32)],
        compiler_params=_cparams(("parallel", "parallel", "arbitrary")),
        name="hgrn2",
    )(proj, proj, proj, proj, rec_lb, norm_g)


def _pack_w_in(w):
    kv0 = ATT_WIDTH
    qi0 = kv0 + KV_LORA_RANK
    sm0 = qi0 + IDX_WIDTH
    sm1 = sm0 + IDX_DIM + N_IDX_HEADS
    pad = jnp.zeros((w.shape[0], SMALL_WIDTH - IDX_DIM - N_IDX_HEADS), BF16)
    parts = [w[:, :kv0], w[:, qi0:sm0], w[:, kv0:qi0], w[:, sm0:sm1]]
    w_att = jnp.concatenate([p.astype(BF16) for p in parts] + [pad], axis=1)
    return w_att, w[:, sm1:].astype(BF16)


def _row_tile(m, want):
    t = min(want, m)
    while m % t:
        t //= 2
    return t


def kernel(x, mem, w_in, w_uk, w_uv, kv_norm_g, idx_kn_g, idx_kn_b, rec_lb, rec_norm_g, w_out, rel_bias,
           wq_c, wk_c, wv_c, wo_c, ffn_gate, ffn_up, ffn_down, ln_g, ln_b):
    batch, seq, d = x.shape
    depth = w_in.shape[0]
    m = batch * seq
    alpha = (2 * depth) ** 0.25
    topk = min(TOPK_MAX, seq // 4)
    d_ff = ffn_gate.shape[-1]
    tf = 256 if d_ff % 256 == 0 else 128

    xf = x.reshape(m, d)
    mem_f = mem.reshape(batch * mem.shape[1], d)
    bias = bias_tables(rel_bias)
    rec_lb = rec_lb.astype(F32)

    ffn_order = [(l, i) for l in range(depth) for i in range(2)]
    ffn_w = {ffn_order[0]: tuple(w[0, 0].astype(BF16) for w in (ffn_gate, ffn_up, ffn_down))}

    def ffn(xf, l, i):
        pos = ffn_order.index((l, i))
        nxt = ffn_order[pos + 1] if pos + 1 < len(ffn_order) else None
        out, cast = ffn_ln(xf, *ffn_w.pop((l, i)), ln_g[l, 3 * i:3 * i + 1], ln_b[l, 3 * i:3 * i + 1],
                           alpha=alpha, tm=_row_tile(m, 512), tf=tf,
                           cast_next=None if nxt is None else (ffn_gate, ffn_up, ffn_down) + nxt)
        if nxt is not None:
            ffn_w[nxt] = cast
        return out

    for l in range(depth):
        xf = ffn(xf, l, 0)

        w_att, w_rec = _pack_w_in(w_in[l])
        proj_a = matmul(xf, w_att, tm=_row_tile(m, 512), tn=768, out_dtype=F32)
        proj_r = matmul(xf, w_rec, tm=_row_tile(m, 512), tn=1024, out_dtype=F32)
        ckv, ckvt, k3, qi3, wt = dsa_prep(proj_a, kv_norm_g[l][None], idx_kn_g[l][None], idx_kn_b[l][None], seq=seq)
        o_att = dsa_attention(proj_a, ckv, ckvt, k3, qi3, wt, w_uk[l].astype(BF16), w_uv[l].astype(BF16), bias,
                              batch=batch, seq=seq, topk=topk)
        o_rec = hgrn2(proj_r, rec_lb, rec_norm_g[l][None], layer=l, batch=batch, seq=seq)
        xf = proj_ln(o_att, o_rec, w_out[l].astype(BF16), xf, ln_g[l, 1:2], ln_b[l, 1:2],
                     alpha=alpha, tm=_row_tile(m, 512), tn=512)

        kv = matmul(mem_f, jnp.concatenate([wk_c[l], wv_c[l]], axis=1).astype(BF16),
                    tm=_row_tile(mem_f.shape[0], 512), tn=256, out_dtype=BF16)
        kv = kv.reshape(batch, mem.shape[1], kv.shape[-1])
        xf = cross_ln(xf, wq_c[l].astype(BF16), kv, wo_c[l].astype(BF16), ln_g[l, 2:3], ln_b[l, 2:3],
                      alpha=alpha, tm=_row_tile(seq, 256), rows_per_batch=seq)

        xf = ffn(xf, l, 1)
    return xf.reshape(batch, seq, d)
```

```python
import functools
import math

import jax
import jax.numpy as jnp
import numpy as np
from jax import lax
from jax.experimental import pallas as pl
from jax.experimental.pallas import tpu as pltpu

F32 = jnp.float32
BF16 = jnp.bfloat16
I32 = jnp.int32

LANE = 128
VMEM_LIMIT_V7X = 60 * 1024 * 1024

HEAD_DIM = 128
N_ATT_HEADS = 16
N_REC_HEADS = 16
KV_LORA_RANK = 512
N_IDX_HEADS = 16
IDX_DIM = 64
TOPK_MAX = 256
Q_BLOCK = 128
N_BUCKETS = 32
MAX_DISTANCE = 128
N_CROSS_HEADS = 4
LN_EPS = 1e-5
ATT_SCALE = HEAD_DIM ** -0.5
IDX_SCALE = IDX_DIM ** -0.5
IDX_HEAD_SCALE = N_IDX_HEADS ** -0.5

ATT_WIDTH = N_ATT_HEADS * HEAD_DIM
IDX_WIDTH = N_IDX_HEADS * IDX_DIM
REC_WIDTH = N_REC_HEADS * HEAD_DIM
COL_QIDX = ATT_WIDTH
COL_KV = COL_QIDX + IDX_WIDTH
COL_SMALL = COL_KV + KV_LORA_RANK
SMALL_WIDTH = 256
COL_REC = COL_SMALL + SMALL_WIDTH

NEG_BIG = -1e30
INT_MIN = -2 ** 31
KEY_NEG_INF = (0xFF800000 - 2 ** 32) ^ 0x7FFFFFFF

REC_CHUNK = 64
REC_TBLOCK = 512
REC_HEADS_PER_STEP = 8
KEY_GROUP = 256
N_BIAS_TABLES = 3
SCORE_GROUP = 512
IDX_K = 256
COUNT_ROWS = 64
LN_ROWS = 128


def _cparams(sem):
    return pltpu.CompilerParams(dimension_semantics=sem, vmem_limit_bytes=VMEM_LIMIT_V7X)


def _single(shape, imap):
    return pl.BlockSpec(shape, imap, pipeline_mode=pl.Buffered(1))


def _layer_norm_rows(y, g, b):
    mu = jnp.mean(y, axis=-1, keepdims=True)
    d = y - mu
    var = jnp.mean(d * d, axis=-1, keepdims=True)
    return d * lax.rsqrt(var + LN_EPS) * g + b


def _residual_ln_inplace(x_ref, o_ref, g_ref, b_ref, *, alpha, scale):
    g = g_ref[...]
    b = b_ref[...]

    def body(i, carry):
        rows = pl.ds(pl.multiple_of(i * LN_ROWS, LN_ROWS), LN_ROWS)
        y = alpha * x_ref[rows, :] + scale * o_ref[rows, :]
        o_ref[rows, :] = _layer_norm_rows(y, g, b)
        return carry

    lax.fori_loop(0, o_ref.shape[0] // LN_ROWS, body, 0)


def _dot(a, b):
    return jnp.dot(a, b, preferred_element_type=F32)


def _dot_nt(a, b):
    return lax.dot_general(a, b, (((1,), (1,)), ((), ())), preferred_element_type=F32)


def _dot_tn(a, b):
    return lax.dot_general(a, b, (((0,), (0,)), ((), ())), preferred_element_type=F32)


def _ffn_ln_kernel(x_ref, wg_ref, wu_ref, wd_ref, g_ref, b_ref, *rest, alpha, n_cast):
    cast_in = rest[:n_cast]
    o_ref = rest[n_cast]
    cast_out = rest[n_cast + 1:2 * n_cast + 1]
    xb_ref = rest[2 * n_cast + 1]
    f = pl.program_id(1)

    @pl.when(f == 0)
    def _():
        xb_ref[...] = x_ref[...].astype(BF16)
        o_ref[...] = jnp.zeros_like(o_ref)

    xb = xb_ref[...]
    h = _dot(xb, wg_ref[...])
    u = _dot(xb, wu_ref[...])
    a = (h * jax.nn.sigmoid(h) * u).astype(BF16)
    o_ref[...] += _dot(a, wd_ref[...])

    for src, dst in zip(cast_in, cast_out):
        dst[...] = src[0, 0].astype(BF16)

    @pl.when(f == pl.num_programs(1) - 1)
    def _():
        _residual_ln_inplace(x_ref, o_ref, g_ref, b_ref, alpha=alpha, scale=0.5)


def ffn_ln(x, wg, wu, wd, g, b, *, alpha, tm, tf, cast_next=None):
    m, d = x.shape
    d_ff = wg.shape[1]
    ni, nf = m // tm, d_ff // tf
    in_specs = [
        pl.BlockSpec((tm, d), lambda i, f: (i, 0)),
        pl.BlockSpec((d, tf), lambda i, f: (0, f)),
        pl.BlockSpec((d, tf), lambda i, f: (0, f)),
        pl.BlockSpec((tf, d), lambda i, f: (f, 0)),
        pl.BlockSpec((1, d), lambda i, f: (0, 0)),
        pl.BlockSpec((1, d), lambda i, f: (0, 0)),
    ]
    out_shape = [jax.ShapeDtypeStruct((m, d), F32)]
    out_specs = [pl.BlockSpec((tm, d), lambda i, f: (i, 0))]
    operands = [x, wg, wu, wd, g, b]
    n_cast = 0
    if cast_next is not None:
        ng, nu, nd, nl, nw = cast_next
        td = d // ni
        n_cast = 3
        in_specs += [
            pl.BlockSpec((1, 1, td, tf), lambda i, f: (nl, nw, i, f)),
            pl.BlockSpec((1, 1, td, tf), lambda i, f: (nl, nw, i, f)),
            pl.BlockSpec((1, 1, tf, td), lambda i, f: (nl, nw, f, i)),
        ]
        out_shape += [jax.ShapeDtypeStruct((d, d_ff), BF16), jax.ShapeDtypeStruct((d, d_ff), BF16),
                      jax.ShapeDtypeStruct((d_ff, d), BF16)]
        out_specs += [
            pl.BlockSpec((td, tf), lambda i, f: (i, f)),
            pl.BlockSpec((td, tf), lambda i, f: (i, f)),
            pl.BlockSpec((tf, td), lambda i, f: (f, i)),
        ]
        operands += [ng, nu, nd]
    outs = pl.pallas_call(
        functools.partial(_ffn_ln_kernel, alpha=alpha, n_cast=n_cast),
        out_shape=tuple(out_shape),
        grid=(ni, nf),
        in_specs=in_specs,
        out_specs=tuple(out_specs),
        scratch_shapes=[pltpu.VMEM((tm, d), BF16)],
        compiler_params=_cparams(("parallel", "arbitrary")),
        name="ffn_ln",
    )(*operands)
    return outs[0], tuple(outs[1:])


def _matmul_kernel(x_ref, w_ref, o_ref, xb_ref):
    @pl.when(pl.program_id(1) == 0)
    def _():
        xb_ref[...] = x_ref[...].astype(BF16)

    o_ref[...] = _dot(xb_ref[...], w_ref[...]).astype(o_ref.dtype)


def matmul(x, w, *, tm, tn, out_dtype):
    m, k = x.shape
    n = w.shape[1]
    return pl.pallas_call(
        _matmul_kernel,
        out_shape=jax.ShapeDtypeStruct((m, n), out_dtype),
        grid=(m // tm, n // tn),
        in_specs=[
            pl.BlockSpec((tm, k), lambda i, j: (i, 0)),
            pl.BlockSpec((k, tn), lambda i, j: (0, j)),
        ],
        out_specs=pl.BlockSpec((tm, tn), lambda i, j: (i, j)),
        scratch_shapes=[pltpu.VMEM((tm, k), BF16)],
        compiler_params=_cparams(("parallel", "arbitrary")),
        name="matmul",
    )(x, w)


def _matmul2_kernel(x_ref, wa_ref, wr_ref, oa_ref, or_ref, xb_ref, *, na):
    c = pl.program_id(1)

    @pl.when(c == 0)
    def _():
        xb_ref[...] = x_ref[...].astype(BF16)

    @pl.when(c < na)
    def _():
        oa_ref[...] = _dot(xb_ref[...], wa_ref[...])

    @pl.when(c >= na)
    def _():
        or_ref[...] = _dot(xb_ref[...], wr_ref[...])


def matmul2(x, wa, wr, *, tm, tna, tnr):
    m, k = x.shape
    na, nr = wa.shape[1] // tna, wr.shape[1] // tnr
    return pl.pallas_call(
        functools.partial(_matmul2_kernel, na=na),
        out_shape=(jax.ShapeDtypeStruct((m, wa.shape[1]), F32), jax.ShapeDtypeStruct((m, wr.shape[1]), F32)),
        grid=(m // tm, na + nr),
        in_specs=[
            pl.BlockSpec((tm, k), lambda i, c: (i, 0)),
            pl.BlockSpec((k, tna), lambda i, c: (0, jnp.minimum(c, na - 1))),
            pl.BlockSpec((k, tnr), lambda i, c: (0, jnp.maximum(c - na, 0))),
        ],
        out_specs=(
            pl.BlockSpec((tm, tna), lambda i, c: (i, jnp.minimum(c, na - 1))),
            pl.BlockSpec((tm, tnr), lambda i, c: (i, jnp.maximum(c - na, 0))),
        ),
        scratch_shapes=[pltpu.VMEM((tm, k), BF16)],
        compiler_params=_cparams(("parallel", "arbitrary")),
        name="matmul2",
    )(x, wa, wr)


def _proj_ln_kernel(a1_ref, a2_ref, w1_ref, w2_ref, x_ref, g_ref, b_ref, o_ref, *, alpha, tn):
    n = pl.program_id(1)
    c0 = pl.multiple_of(n * tn, tn)
    o_ref[:, pl.ds(c0, tn)] = _dot(a1_ref[...], w1_ref[...]) + _dot(a2_ref[...], w2_ref[...])

    @pl.when(n == pl.num_programs(1) - 1)
    def _():
        _residual_ln_inplace(x_ref, o_ref, g_ref, b_ref, alpha=alpha, scale=1.0)


def proj_ln(a1, a2, w, x, g, b, *, alpha, tm, tn):
    m, kh = a1.shape
    d = w.shape[1]
    return pl.pallas_call(
        functools.partial(_proj_ln_kernel, alpha=alpha, tn=tn),
        out_shape=jax.ShapeDtypeStruct((m, d), F32),
        grid=(m // tm, d // tn),
        in_specs=[
            pl.BlockSpec((tm, kh), lambda i, n: (i, 0)),
            pl.BlockSpec((tm, kh), lambda i, n: (i, 0)),
            pl.BlockSpec((kh, tn), lambda i, n: (0, n)),
            pl.BlockSpec((kh, tn), lambda i, n: (1, n)),
            pl.BlockSpec((tm, d), lambda i, n: (i, 0)),
            pl.BlockSpec((1, d), lambda i, n: (0, 0)),
            pl.BlockSpec((1, d), lambda i, n: (0, 0)),
        ],
        out_specs=pl.BlockSpec((tm, d), lambda i, n: (i, 0)),
        compiler_params=_cparams(("parallel", "arbitrary")),
        name="proj_ln",
    )(a1, a2, w, w, x, g, b)


def _cross_ln_kernel(x_ref, wq_ref, kv_ref, wo_ref, g_ref, b_ref, o_ref, *, alpha, n_heads):
    x = x_ref[...]
    q = _dot(x.astype(BF16), wq_ref[...])
    kv = kv_ref[0]
    cw = n_heads * HEAD_DIM
    outs = []
    for h in range(n_heads):
        qh = q[:, h * HEAD_DIM:(h + 1) * HEAD_DIM].astype(BF16)
        kh = kv[:, h * HEAD_DIM:(h + 1) * HEAD_DIM]
        vh = kv[:, cw + h * HEAD_DIM:cw + (h + 1) * HEAD_DIM]
        logits = _dot_nt(qh, kh) * ATT_SCALE
        mx = jnp.max(logits, axis=-1, keepdims=True)
        e = jnp.exp(logits - mx)
        p = e / jnp.sum(e, axis=-1, keepdims=True)
        outs.append(_dot(p.astype(BF16), vh))
    o = jnp.concatenate(outs, axis=-1).astype(BF16)
    o_ref[...] = _dot(o, wo_ref[...])
    _residual_ln_inplace(x_ref, o_ref, g_ref, b_ref, alpha=alpha, scale=1.0)


def cross_ln(x, wq, kv, wo, g, b, *, alpha, tm, rows_per_batch):
    m, d = x.shape
    cw = wq.shape[1]
    nmem = kv.shape[1]
    steps_per_batch = rows_per_batch // tm
    return pl.pallas_call(
        functools.partial(_cross_ln_kernel, alpha=alpha, n_heads=cw // HEAD_DIM),
        out_shape=jax.ShapeDtypeStruct((m, d), F32),
        grid=(m // tm,),
        in_specs=[
            pl.BlockSpec((tm, d), lambda i: (i, 0)),
            _single((d, cw), lambda i: (0, 0)),
            pl.BlockSpec((1, nmem, 2 * cw), lambda i: (i // steps_per_batch, 0, 0)),
            _single((cw, d), lambda i: (0, 0)),
            pl.BlockSpec((1, d), lambda i: (0, 0)),
            pl.BlockSpec((1, d), lambda i: (0, 0)),
        ],
        out_specs=pl.BlockSpec((tm, d), lambda i: (i, 0)),
        compiler_params=_cparams(("parallel",)),
        name="cross_ln",
    )(x, wq, kv, wo, g, b)


def _split3(x, order):
    hi = x.astype(BF16).astype(F32)
    lo = x - hi
    parts = [hi if c == "h" else lo for c in order]
    parts.append(jnp.zeros((x.shape[0], IDX_K - 3 * x.shape[1]), F32))
    return jnp.concatenate(parts, axis=-1).astype(BF16)


def _dsa_prep_kernel(kv_ref, qi_ref, sm_ref, kvg_ref, kng_ref, knb_ref,
                     ckv_ref, ckvt_ref, k3_ref, qi3_ref, wt_ref):
    kv = kv_ref[...]
    ms = jnp.mean(kv * kv, axis=-1, keepdims=True)
    ckv = kv * lax.rsqrt(ms + LN_EPS) * kvg_ref[...]
    ckv_ref[...] = ckv.astype(BF16)
    ckvt_ref[0] = jnp.transpose(ckv).astype(BF16)

    sm = sm_ref[...]
    kidx = _layer_norm_rows(sm[:, :IDX_DIM], kng_ref[...], knb_ref[...])
    k3_ref[...] = _split3(kidx, "hlh")

    wt = jnp.transpose(sm)
    wt_ref[...] = wt[IDX_DIM:IDX_DIM + N_IDX_HEADS, :] * (IDX_HEAD_SCALE * IDX_SCALE)

    qi = qi_ref[...]
    for h in range(N_IDX_HEADS):
        qi3_ref[h] = _split3(qi[:, h * IDX_DIM:(h + 1) * IDX_DIM], "hhl")


def dsa_prep(proj, kv_g, kn_g, kn_b, *, seq):
    m = proj.shape[0]
    tq = Q_BLOCK
    nq = m // tq
    nqb = seq // tq
    return pl.pallas_call(
        _dsa_prep_kernel,
        out_shape=(
            jax.ShapeDtypeStruct((m, KV_LORA_RANK), BF16),
            jax.ShapeDtypeStruct((m // seq, KV_LORA_RANK, seq), BF16),
            jax.ShapeDtypeStruct((m, IDX_K), BF16),
            jax.ShapeDtypeStruct((nq * N_IDX_HEADS, tq, IDX_K), BF16),
            jax.ShapeDtypeStruct((nq * N_IDX_HEADS, tq), F32),
        ),
        grid=(nq,),
        in_specs=[
            pl.BlockSpec((tq, KV_LORA_RANK), lambda i: (i, COL_KV // KV_LORA_RANK)),
            pl.BlockSpec((tq, IDX_WIDTH), lambda i: (i, COL_QIDX // IDX_WIDTH)),
            pl.BlockSpec((tq, LANE), lambda i: (i, COL_SMALL // LANE)),
            pl.BlockSpec((1, KV_LORA_RANK), lambda i: (0, 0)),
            pl.BlockSpec((1, IDX_DIM), lambda i: (0, 0)),
            pl.BlockSpec((1, IDX_DIM), lambda i: (0, 0)),
        ],
        out_specs=(
            pl.BlockSpec((tq, KV_LORA_RANK), lambda i: (i, 0)),
            pl.BlockSpec((1, KV_LORA_RANK, tq), lambda i: (i // nqb, 0, i % nqb)),
            pl.BlockSpec((tq, IDX_K), lambda i: (i, 0)),
            pl.BlockSpec((N_IDX_HEADS, tq, IDX_K), lambda i: (i, 0, 0)),
            pl.BlockSpec((N_IDX_HEADS, tq), lambda i: (i, 0)),
        ),
        compiler_params=_cparams(("parallel",)),
        name="dsa_prep",
    )(proj, proj, proj, kv_g, kn_g, kn_b)


def _bucket_thresholds():
    max_exact = N_BUCKETS // 2
    rel = np.arange(max_exact, 4 * MAX_DISTANCE, dtype=np.float32)
    large = max_exact + (np.log(rel / np.float32(max_exact)) / np.float32(math.log(MAX_DISTANCE / max_exact))
                         * np.float32(N_BUCKETS - max_exact)).astype(np.int32)
    large = np.minimum(large, N_BUCKETS - 1)
    return [int(rel[np.argmax(large >= bkt)]) for bkt in range(max_exact + 1, N_BUCKETS)]


def _bias_table_kernel(rb_ref, o_ref):
    di = pl.program_id(0)
    max_exact = N_BUCKETS // 2
    sl = lax.broadcasted_iota(I32, (KEY_GROUP, Q_BLOCK), 0)
    tl = lax.broadcasted_iota(I32, (KEY_GROUP, Q_BLOCK), 1)
    rel = jnp.maximum(di * Q_BLOCK + tl - sl, 0)
    large = jnp.full(rel.shape, max_exact, I32)
    for th in _bucket_thresholds():
        large = large + (rel >= th).astype(I32)
    bucket = jnp.where(rel < max_exact, rel, large)
    for h in range(N_ATT_HEADS):
        far = rb_ref[N_BUCKETS - 1, h]
        val = jnp.zeros(rel.shape, F32)
        for bkt in range(N_BUCKETS - 1):
            val = jnp.where(bucket == bkt, rb_ref[bkt, h] - far, val)
        o_ref[0, :, h * Q_BLOCK:(h + 1) * Q_BLOCK] = val


def bias_tables(rel_bias):
    nh = rel_bias.shape[1]
    return pl.pallas_call(
        _bias_table_kernel,
        out_shape=jax.ShapeDtypeStruct((N_BIAS_TABLES, KEY_GROUP, nh * Q_BLOCK), F32),
        grid=(N_BIAS_TABLES,),
        in_specs=[pl.BlockSpec(memory_space=pltpu.SMEM)],
        out_specs=pl.BlockSpec((1, KEY_GROUP, nh * Q_BLOCK), lambda d: (d, 0, 0)),
        compiler_params=_cparams(("arbitrary",)),
        name="bias_tables",
    )(rel_bias)


def _dsa_kernel(q_ref, ckv_ref, ckvt_ref, k3_ref, qi3_ref, wt_ref, wuk_ref, wuv_ref, bias_ref, o_ref,
                keys_ref, qabs_ref, acc_ref, *, topk):
    j = pl.program_id(1)
    nh = N_ATT_HEADS
    tq = Q_BLOCK
    t_lane = j * tq + lax.broadcasted_iota(I32, (1, tq), 1)

    n_sgrp = (j + 1 + (SCORE_GROUP // tq - 1)) // (SCORE_GROUP // tq)
    qi3 = qi3_ref[...].reshape(N_IDX_HEADS * tq, IDX_K)

    def score_body(g, carry):
        r0 = pl.multiple_of(g * SCORE_GROUP, SCORE_GROUP)
        lt = _dot_nt(k3_ref[pl.ds(r0, SCORE_GROUP), :], qi3)
        sc = jnp.zeros((SCORE_GROUP, tq), F32)
        for h in range(N_IDX_HEADS):
            sc = sc + jnp.maximum(lt[:, h * tq:(h + 1) * tq], 0.0) * wt_ref[h:h + 1, :]
        s_row = r0 + lax.broadcasted_iota(I32, (SCORE_GROUP, 1), 0)
        sc = jnp.where(s_row <= t_lane, sc + 0.0, -jnp.inf)
        bits = pltpu.bitcast(sc, I32)
        keys_ref[pl.ds(r0, SCORE_GROUP), :] = jnp.where(bits < 0, bits ^ 0x7FFFFFFF, bits)
        return carry

    lax.fori_loop(0, n_sgrp, score_body, 0)

    def count_ge(cand):
        def body(g, part):
            r0 = pl.multiple_of(g * SCORE_GROUP, SCORE_GROUP)
            ind = jnp.where(keys_ref[pl.ds(r0, SCORE_GROUP), :] >= cand, 1.0, 0.0)
            return part + jnp.sum(ind.reshape(SCORE_GROUP // COUNT_ROWS, COUNT_ROWS, tq), axis=0)
        part = lax.fori_loop(0, n_sgrp, body, jnp.zeros((COUNT_ROWS, tq), F32))
        return jnp.sum(part, axis=0, keepdims=True)

    thr0 = jnp.where(count_ge(jnp.zeros((1, tq), I32)) >= topk, 0, INT_MIN).astype(I32)

    def bit_body(i, thr):
        cand = thr | jnp.left_shift(jnp.int32(1), 30 - i)
        return jnp.where(count_ge(cand) >= topk, cand, thr)

    thr = lax.fori_loop(0, 31, bit_body, thr0)

    n_ge = count_ge(thr)
    tie_lane = (n_ge > topk) & (thr > KEY_NEG_INF)
    has_tie = jnp.max(jnp.where(tie_lane, 1.0, 0.0)) > 0.0

    @pl.when(has_tie)
    def _():
        need = topk - count_ge(thr + 1)
        r_i = lax.broadcasted_iota(I32, (tq, tq), 0)
        c_i = lax.broadcasted_iota(I32, (tq, tq), 1)
        tri = jnp.where(c_i <= r_i, 1.0, 0.0).astype(BF16)

        def tie_body(kb, seen):
            r0 = pl.multiple_of(kb * tq, tq)
            blk = keys_ref[pl.ds(r0, tq), :]
            s_row = r0 + lax.broadcasted_iota(I32, (tq, 1), 0)
            eq = (blk == thr) & (s_row <= t_lane) & tie_lane
            rank = seen + _dot(tri, jnp.where(eq, 1.0, 0.0).astype(BF16))
            keys_ref[pl.ds(r0, tq), :] = jnp.where(eq & (rank > need), thr - 1, blk)
            return rank[tq - 1:tq, :]

        lax.fori_loop(0, j + 1, tie_body, jnp.zeros((1, tq), F32))

    for h in range(nh):
        qh = q_ref[:, h * HEAD_DIM:(h + 1) * HEAD_DIM].astype(BF16)
        qabs_ref[:, h * tq:(h + 1) * tq] = (_dot_nt(wuk_ref[h], qh) * ATT_SCALE).astype(BF16)

    acc_ref[...] = jnp.zeros(acc_ref.shape, F32)

    def attend(c0, carry, bias_di):
        kg = KEY_GROUP
        m, l = carry
        s_all = _dot(ckv_ref[pl.ds(c0, kg), :], qabs_ref[...])
        s_row = c0 + lax.broadcasted_iota(I32, (kg, 1), 0)
        sel = (keys_ref[pl.ds(c0, kg), :] >= thr) & (s_row <= t_lane)
        addm = jnp.where(sel, 0.0, NEG_BIG)
        ps, ms, ls, corrs = [], [], [], []
        for h in range(nh):
            lanes = slice(h * tq, (h + 1) * tq)
            s = s_all[:, lanes] + addm
            if bias_di is not None:
                s = s + bias_ref[bias_di, :, lanes]
            m_prev = m[:, lanes]
            m_new = jnp.maximum(m_prev, jnp.max(s, axis=0, keepdims=True))
            p = jnp.exp(s - m_new)
            corr = jnp.exp(m_prev - m_new)
            ls.append(corr * l[:, lanes] + jnp.sum(p, axis=0, keepdims=True))
            ms.append(m_new)
            corrs.append(corr)
            ps.append(p.astype(BF16))
        pt = jnp.concatenate(ps, axis=1)
        corr_all = jnp.concatenate(corrs, axis=1)
        acc_ref[...] = acc_ref[...] * corr_all + _dot(ckvt_ref[0, :, pl.ds(c0, kg)], pt)
        return jnp.concatenate(ms, axis=1), jnp.concatenate(ls, axis=1)

    def at_group(g):
        return pl.multiple_of(g * KEY_GROUP, KEY_GROUP)

    carry = (jnp.full((1, nh * tq), NEG_BIG, F32), jnp.zeros((1, nh * tq), F32))
    n_far = jnp.maximum(j // 2 - 1 + j % 2, 0)
    carry = lax.fori_loop(0, n_far, lambda g, cr: attend(at_group(g), cr, None), carry)
    carry = lax.cond((j % 2 == 0) & (j >= 2), lambda cr: attend(at_group(j // 2 - 1), cr, 2), lambda cr: cr, carry)
    _, l = attend(at_group(j // 2), carry, j % 2)

    o_lat = (acc_ref[...] / l).astype(BF16)
    for h in range(nh):
        o_ref[:, h * HEAD_DIM:(h + 1) * HEAD_DIM] = _dot_tn(
            o_lat[:, h * tq:(h + 1) * tq], wuv_ref[h]).astype(o_ref.dtype)


def dsa_attention(proj, ckv, ckvt, k3, qi3, wt, wuk, wuv, bias, *, batch, seq, topk):
    tq = Q_BLOCK
    nq = seq // tq
    nh = N_ATT_HEADS
    r = KV_LORA_RANK
    seq_pad = -(-seq // SCORE_GROUP) * SCORE_GROUP
    return pl.pallas_call(
        functools.partial(_dsa_kernel, topk=topk),
        out_shape=jax.ShapeDtypeStruct((batch * seq, ATT_WIDTH), BF16),
        grid=(batch, nq),
        in_specs=[
            pl.BlockSpec((tq, ATT_WIDTH), lambda b, j: (b * nq + j, 0)),
            _single((seq, r), lambda b, j: (b, 0)),
            _single((1, r, seq), lambda b, j: (b, 0, 0)),
            _single((seq, IDX_K), lambda b, j: (b, 0)),
            pl.BlockSpec((N_IDX_HEADS, tq, IDX_K), lambda b, j: (b * nq + j, 0, 0)),
            pl.BlockSpec((N_IDX_HEADS, tq), lambda b, j: (b * nq + j, 0)),
            _single((nh, r, HEAD_DIM), lambda b, j: (0, 0, 0)),
            _single((nh, r, HEAD_DIM), lambda b, j: (0, 0, 0)),
            _single((N_BIAS_TABLES, KEY_GROUP, nh * tq), lambda b, j: (0, 0, 0)),
        ],
        out_specs=pl.BlockSpec((tq, ATT_WIDTH), lambda b, j: (b * nq + j, 0)),
        scratch_shapes=[
            pltpu.VMEM((seq_pad, tq), I32),
            pltpu.VMEM((r, nh * tq), BF16),
            pltpu.VMEM((r, nh * tq), F32),
        ],
        compiler_params=_cparams(("parallel", "arbitrary")),
        name="dsa",
    )(proj, ckv, ckvt, k3, qi3, wt, wuk, wuv, bias)


def _cumsum_rows(x):
    n = x.shape[0]
    row = lax.broadcasted_iota(I32, (n, 1), 0)
    sh = 1
    while sh < n:
        x = x + jnp.where(row >= sh, pltpu.roll(x, sh, 0), 0.0)
        sh *= 2
    return x


def _hgrn2_kernel(q_ref, f_ref, i_ref, g_ref, lb_ref, ng_ref, o_ref, st_ref, *, layer):
    c = REC_CHUNK
    dk = HEAD_DIM

    @pl.when(pl.program_id(2) == 0)
    def _():
        st_ref[...] = jnp.zeros_like(st_ref)

    lbr = lb_ref[...]
    e = jnp.exp(lbr - jnp.max(lbr, axis=0, keepdims=True))
    pr = e / jnp.sum(e, axis=0, keepdims=True)
    lb_all = jnp.zeros((1, lbr.shape[1]), F32)
    for i in range(1, layer + 1):
        lb_all = lb_all + pr[i:i + 1, :]

    row = lax.broadcasted_iota(I32, (c, 1), 0)
    col = lax.broadcasted_iota(I32, (1, c), 1)
    sub = 8
    nsub = c // sub
    tl3 = lax.broadcasted_iota(I32, (1, sub, 1), 1)
    g3 = lax.broadcasted_iota(I32, (nsub, 1, 1), 0)
    col3 = lax.broadcasted_iota(I32, (1, 1, c), 2)
    levels = []
    m = c // 2
    while m >= sub:
        levels.append(m)
        m //= 2

    def head_chunk(r0, hh):
        lanes = slice(hh * dk, (hh + 1) * dk)
        lb = lb_all[:, lanes]
        q = q_ref[pl.ds(r0, c), lanes]
        qf = q * jax.nn.sigmoid(q)
        f = lb + (1.0 - lb) * jax.nn.sigmoid(f_ref[pl.ds(r0, c), lanes])
        k = 1.0 - f
        v = i_ref[pl.ds(r0, c), lanes].astype(BF16)
        cum = _cumsum_rows(jnp.log(f))

        a = jnp.zeros((c, c), F32)
        for m in levels:
            first = (row & (2 * m - 1)) < m
            bnd = jnp.concatenate(
                [jnp.broadcast_to(cum[b * 2 * m + m - 1:b * 2 * m + m, :], (2 * m, dk))
                 for b in range(c // (2 * m))], axis=0)
            qs = jnp.where(first, 0.0, qf * jnp.exp(jnp.minimum(cum - bnd, 0.0))).astype(BF16)
            ks = jnp.where(first, k * jnp.exp(jnp.minimum(bnd - cum, 0.0)), 0.0).astype(BF16)
            same = (row & -(2 * m)) == (col & -(2 * m))
            a = a + jnp.where(same, _dot_nt(qs, ks), 0.0)
        cum3 = cum.reshape(nsub, sub, dk)
        k3 = k.reshape(nsub, sub, dk)
        qf3 = qf.reshape(nsub, sub, dk)
        a3 = jnp.zeros((nsub, sub, c), F32)
        for s in range(sub):
            d = jnp.where(tl3 >= s, cum3 - cum3[:, s:s + 1, :], -jnp.inf)
            prod = qf3 * k3[:, s:s + 1, :] * jnp.exp(d)
            colsum = jnp.sum(prod, axis=-1, keepdims=True)
            a3 = a3 + jnp.where(col3 == g3 * sub + s, colsum, 0.0)
        a = a + a3.reshape(c, c)

        st = st_ref[hh]
        o = _dot(a.astype(BF16), v) + _dot_nt((qf * jnp.exp(cum)).astype(BF16), st.astype(BF16))
        last = cum[c - 1:c, :]
        kl = (k * jnp.exp(last - cum)).astype(BF16)
        st_ref[hh] = st * jnp.exp(last) + _dot_tn(v, kl)

        ms = jnp.mean(o * o, axis=-1, keepdims=True)
        gt = g_ref[pl.ds(r0, c), lanes]
        on = o * lax.rsqrt(ms + LN_EPS) * ng_ref[:, lanes]
        o_ref[pl.ds(r0, c), lanes] = (on * (gt * jax.nn.sigmoid(gt))).astype(o_ref.dtype)

    def chunk_body(ci, carry):
        r0 = pl.multiple_of(ci * c, c)
        for hh in range(REC_HEADS_PER_STEP):
            head_chunk(r0, hh)
        return carry

    lax.fori_loop(0, q_ref.shape[0] // c, chunk_body, 0)


def hgrn2(proj, rec_lb, norm_g, *, layer, batch, seq):
    tb = min(REC_TBLOCK, seq)
    nt = seq // tb
    hw = REC_HEADS_PER_STEP * HEAD_DIM
    ng = N_REC_HEADS // REC_HEADS_PER_STEP

    def seg(k):
        return pl.BlockSpec((tb, hw), lambda b, h, t, k=k: (b * nt + t, k * ng + h))

    return pl.pallas_call(
        functools.partial(_hgrn2_kernel, layer=layer),
        out_shape=jax.ShapeDtypeStruct((batch * seq, REC_WIDTH), BF16),
        grid=(batch, ng, nt),
        in_specs=[
            seg(0), seg(1), seg(2), seg(3),
            pl.BlockSpec((rec_lb.shape[0], hw), lambda b, h, t: (0, h)),
            pl.BlockSpec((1, hw), lambda b, h, t: (0, h)),
        ],
        out_specs=pl.BlockSpec((tb, hw), lambda b, h, t: (b * nt + t, h)),
        scratch_shapes=[pltpu.VMEM((REC_HEADS_PER_STEP, HEAD_DIM, HEAD_DIM), F32)],
        compiler_params=_cparams(("parallel", "parallel", "arbitrary")),
        name="hgrn2",
    )(proj, proj, proj, proj, rec_lb, norm_g)


def _pack_w_in(w):
    kv0 = ATT_WIDTH
    qi0 = kv0 + KV_LORA_RANK
    sm0 = qi0 + IDX_WIDTH
    sm1 = sm0 + IDX_DIM + N_IDX_HEADS
    pad = jnp.zeros((w.shape[0], SMALL_WIDTH - IDX_DIM - N_IDX_HEADS), BF16)
    parts = [w[:, :kv0], w[:, qi0:sm0], w[:, kv0:qi0], w[:, sm0:sm1]]
    w_att = jnp.concatenate([p.astype(BF16) for p in parts] + [pad], axis=1)
    return w_att, w[:, sm1:].astype(BF16)


def _row_tile(m, want):
    t = min(want, m)
    while m % t:
        t //= 2
    return t


def kernel(x, mem, w_in, w_uk, w_uv, kv_norm_g, idx_kn_g, idx_kn_b, rec_lb, rec_norm_g, w_out, rel_bias,
           wq_c, wk_c, wv_c, wo_c, ffn_gate, ffn_up, ffn_down, ln_g, ln_b):
    batch, seq, d = x.shape
    depth = w_in.shape[0]
    m = batch * seq
    alpha = (2 * depth) ** 0.25
    topk = min(TOPK_MAX, seq // 4)
    d_ff = ffn_gate.shape[-1]
    tf = 256 if d_ff % 256 == 0 else 128

    xf = x.reshape(m, d)
    mem_f = mem.reshape(batch * mem.shape[1], d)
    bias = bias_tables(rel_bias)
    rec_lb = rec_lb.astype(F32)

    ffn_order = [(l, i) for l in range(depth) for i in range(2)]
    ffn_w = {ffn_order[0]: tuple(w[0, 0].astype(BF16) for w in (ffn_gate, ffn_up, ffn_down))}

    def ffn(xf, l, i):
        pos = ffn_order.index((l, i))
        nxt = ffn_order[pos + 1] if pos + 1 < len(ffn_order) else None
        out, cast = ffn_ln(xf, *ffn_w.pop((l, i)), ln_g[l, 3 * i:3 * i + 1], ln_b[l, 3 * i:3 * i + 1],
                           alpha=alpha, tm=_row_tile(m, 512), tf=tf,
                           cast_next=None if nxt is None else (ffn_gate, ffn_up, ffn_down) + nxt)
        if nxt is not None:
            ffn_w[nxt] = cast
        return out

    for l in range(depth):
        xf = ffn(xf, l, 0)

        w_att, w_rec = _pack_w_in(w_in[l])
        proj_a, proj_r = matmul2(xf, w_att, w_rec, tm=_row_tile(m, 512), tna=768, tnr=512)
        ckv, ckvt, k3, qi3, wt = dsa_prep(proj_a, kv_norm_g[l][None], idx_kn_g[l][None], idx_kn_b[l][None], seq=seq)
        o_att = dsa_attention(proj_a, ckv, ckvt, k3, qi3, wt, w_uk[l].astype(BF16), w_uv[l].astype(BF16), bias,
                              batch=batch, seq=seq, topk=topk)
        o_rec = hgrn2(proj_r, rec_lb, rec_norm_g[l][None], layer=l, batch=batch, seq=seq)
        xf = proj_ln(o_att, o_rec, w_out[l].astype(BF16), xf, ln_g[l, 1:2], ln_b[l, 1:2],
                     alpha=alpha, tm=_row_tile(m, 512), tn=512)

        kv = matmul(mem_f, jnp.concatenate([wk_c[l], wv_c[l]], axis=1).astype(BF16),
                    tm=_row_tile(mem_f.shape[0], 512), tn=256, out_dtype=BF16)
        kv = kv.reshape(batch, mem.shape[1], kv.shape[-1])
        xf = cross_ln(xf, wq_c[l].astype(BF16), kv, wo_c[l].astype(BF16), ln_g[l, 2:3], ln_b[l, 2:3],
                      alpha=alpha, tm=_row_tile(seq, 256), rows_per_batch=seq)

        xf = ffn(xf, l, 1)
    return xf.reshape(batch, seq, d)
```

```python
import functools
import math

import jax
import jax.numpy as jnp
import numpy as np
from jax import lax
from jax.experimental import pallas as pl
from jax.experimental.pallas import tpu as pltpu

F32 = jnp.float32
BF16 = jnp.bfloat16
I32 = jnp.int32

LANE = 128
VMEM_LIMIT_V7X = 60 * 1024 * 1024

HEAD_DIM = 128
N_ATT_HEADS = 16
N_REC_HEADS = 16
KV_LORA_RANK = 512
N_IDX_HEADS = 16
IDX_DIM = 64
TOPK_MAX = 256
Q_BLOCK = 128
N_BUCKETS = 32
MAX_DISTANCE = 128
N_CROSS_HEADS = 4
LN_EPS = 1e-5
ATT_SCALE = HEAD_DIM ** -0.5
IDX_SCALE = IDX_DIM ** -0.5
IDX_HEAD_SCALE = N_IDX_HEADS ** -0.5

ATT_WIDTH = N_ATT_HEADS * HEAD_DIM
IDX_WIDTH = N_IDX_HEADS * IDX_DIM
REC_WIDTH = N_REC_HEADS * HEAD_DIM
COL_QIDX = ATT_WIDTH
COL_KV = COL_QIDX + IDX_WIDTH
COL_SMALL = COL_KV + KV_LORA_RANK
SMALL_WIDTH = 256
COL_REC = COL_SMALL + SMALL_WIDTH

NEG_BIG = -1e30
INT_MIN = -2 ** 31
KEY_NEG_INF = (0xFF800000 - 2 ** 32) ^ 0x7FFFFFFF

REC_CHUNK = 64
REC_TBLOCK = 512
REC_HEADS_PER_STEP = 8
KEY_GROUP = 256
N_BIAS_TABLES = 3
SCORE_GROUP = 512
IDX_K = 256
COUNT_ROWS = 64
LN_ROWS = 128
PROJ_CAST_COLS = 384


def _cparams(sem):
    return pltpu.CompilerParams(dimension_semantics=sem, vmem_limit_bytes=VMEM_LIMIT_V7X)


def _single(shape, imap):
    return pl.BlockSpec(shape, imap, pipeline_mode=pl.Buffered(1))


def _layer_norm_rows(y, g, b):
    mu = jnp.mean(y, axis=-1, keepdims=True)
    d = y - mu
    var = jnp.mean(d * d, axis=-1, keepdims=True)
    return d * lax.rsqrt(var + LN_EPS) * g + b


def _residual_ln_inplace(x_ref, o_ref, g_ref, b_ref, *, alpha, scale):
    g = g_ref[...]
    b = b_ref[...]

    def body(i, carry):
        rows = pl.ds(pl.multiple_of(i * LN_ROWS, LN_ROWS), LN_ROWS)
        y = alpha * x_ref[rows, :] + scale * o_ref[rows, :]
        o_ref[rows, :] = _layer_norm_rows(y, g, b)
        return carry

    lax.fori_loop(0, o_ref.shape[0] // LN_ROWS, body, 0)


def _dot(a, b):
    return jnp.dot(a, b, preferred_element_type=F32)


def _dot_nt(a, b):
    return lax.dot_general(a, b, (((1,), (1,)), ((), ())), preferred_element_type=F32)


def _dot_tn(a, b):
    return lax.dot_general(a, b, (((0,), (0,)), ((), ())), preferred_element_type=F32)


def _ffn_ln_kernel(x_ref, wg_ref, wu_ref, wd_ref, g_ref, b_ref, *rest, alpha, n_cast):
    cast_in = rest[:n_cast]
    o_ref = rest[n_cast]
    cast_out = rest[n_cast + 1:2 * n_cast + 1]
    xb_ref = rest[2 * n_cast + 1]
    f = pl.program_id(1)

    @pl.when(f == 0)
    def _():
        xb_ref[...] = x_ref[...].astype(BF16)
        o_ref[...] = jnp.zeros_like(o_ref)

    xb = xb_ref[...]
    h = _dot(xb, wg_ref[...])
    u = _dot(xb, wu_ref[...])
    a = (h * jax.nn.sigmoid(h) * u).astype(BF16)
    o_ref[...] += _dot(a, wd_ref[...])

    for src, dst in zip(cast_in, cast_out):
        dst[...] = src[...].reshape(dst.shape).astype(BF16)

    @pl.when(f == pl.num_programs(1) - 1)
    def _():
        _residual_ln_inplace(x_ref, o_ref, g_ref, b_ref, alpha=alpha, scale=0.5)


def ffn_ln(x, wg, wu, wd, g, b, *, alpha, tm, tf, cast_next=None, cast_proj=None):
    m, d = x.shape
    d_ff = wg.shape[1]
    ni, nf = m // tm, d_ff // tf
    in_specs = [
        pl.BlockSpec((tm, d), lambda i, f: (i, 0)),
        pl.BlockSpec((d, tf), lambda i, f: (0, f)),
        pl.BlockSpec((d, tf), lambda i, f: (0, f)),
        pl.BlockSpec((tf, d), lambda i, f: (f, 0)),
        pl.BlockSpec((1, d), lambda i, f: (0, 0)),
        pl.BlockSpec((1, d), lambda i, f: (0, 0)),
    ]
    out_shape = [jax.ShapeDtypeStruct((m, d), F32)]
    out_specs = [pl.BlockSpec((tm, d), lambda i, f: (i, 0))]
    operands = [x, wg, wu, wd, g, b]
    n_cast = 0
    if cast_next is not None:
        ng, nu, nd, nl, nw = cast_next
        td = d // ni
        n_cast = 3
        in_specs += [
            pl.BlockSpec((1, 1, td, tf), lambda i, f: (nl, nw, i, f)),
            pl.BlockSpec((1, 1, td, tf), lambda i, f: (nl, nw, i, f)),
            pl.BlockSpec((1, 1, tf, td), lambda i, f: (nl, nw, f, i)),
        ]
        out_shape += [jax.ShapeDtypeStruct((d, d_ff), BF16), jax.ShapeDtypeStruct((d, d_ff), BF16),
                      jax.ShapeDtypeStruct((d_ff, d), BF16)]
        out_specs += [
            pl.BlockSpec((td, tf), lambda i, f: (i, f)),
            pl.BlockSpec((td, tf), lambda i, f: (i, f)),
            pl.BlockSpec((tf, td), lambda i, f: (f, i)),
        ]
        operands += [ng, nu, nd]
    if cast_proj is not None:
        w_in, pl_layer = cast_proj
        td = d // ni
        n_in = w_in.shape[-1]
        ncb = pl.cdiv(n_in, PROJ_CAST_COLS)
        assert ncb <= nf
        n_cast += 1
        in_specs.append(pl.BlockSpec((1, td, PROJ_CAST_COLS), lambda i, f: (pl_layer, i, jnp.minimum(f, ncb - 1))))
        operands.append(w_in)
        out_shape.append(jax.ShapeDtypeStruct((d, n_in), BF16))
        out_specs.append(pl.BlockSpec((td, PROJ_CAST_COLS), lambda i, f: (i, jnp.minimum(f, ncb - 1))))
    outs = pl.pallas_call(
        functools.partial(_ffn_ln_kernel, alpha=alpha, n_cast=n_cast),
        out_shape=tuple(out_shape),
        grid=(ni, nf),
        in_specs=in_specs,
        out_specs=tuple(out_specs),
        scratch_shapes=[pltpu.VMEM((tm, d), BF16)],
        compiler_params=_cparams(("parallel", "arbitrary")),
        name="ffn_ln",
    )(*operands)
    return outs[0], tuple(outs[1:])


def _matmul_kernel(x_ref, w_ref, o_ref, xb_ref):
    @pl.when(pl.program_id(1) == 0)
    def _():
        xb_ref[...] = x_ref[...].astype(BF16)

    o_ref[...] = _dot(xb_ref[...], w_ref[...]).astype(o_ref.dtype)


def matmul(x, w, *, tm, tn, out_dtype):
    m, k = x.shape
    n = w.shape[1]
    return pl.pallas_call(
        _matmul_kernel,
        out_shape=jax.ShapeDtypeStruct((m, n), out_dtype),
        grid=(m // tm, n // tn),
        in_specs=[
            pl.BlockSpec((tm, k), lambda i, j: (i, 0)),
            pl.BlockSpec((k, tn), lambda i, j: (0, j)),
        ],
        out_specs=pl.BlockSpec((tm, tn), lambda i, j: (i, j)),
        scratch_shapes=[pltpu.VMEM((tm, k), BF16)],
        compiler_params=_cparams(("parallel", "arbitrary")),
        name="matmul",
    )(x, w)


def _proj_ln_kernel(a1_ref, a2_ref, w1_ref, w2_ref, x_ref, g_ref, b_ref, o_ref, *, alpha, tn):
    n = pl.program_id(1)
    c0 = pl.multiple_of(n * tn, tn)
    o_ref[:, pl.ds(c0, tn)] = _dot(a1_ref[...], w1_ref[...]) + _dot(a2_ref[...], w2_ref[...])

    @pl.when(n == pl.num_programs(1) - 1)
    def _():
        _residual_ln_inplace(x_ref, o_ref, g_ref, b_ref, alpha=alpha, scale=1.0)


def proj_ln(a1, a2, w, x, g, b, *, alpha, tm, tn):
    m, kh = a1.shape
    d = w.shape[1]
    return pl.pallas_call(
        functools.partial(_proj_ln_kernel, alpha=alpha, tn=tn),
        out_shape=jax.ShapeDtypeStruct((m, d), F32),
        grid=(m // tm, d // tn),
        in_specs=[
            pl.BlockSpec((tm, kh), lambda i, n: (i, 0)),
            pl.BlockSpec((tm, kh), lambda i, n: (i, 0)),
            pl.BlockSpec((kh, tn), lambda i, n: (0, n)),
            pl.BlockSpec((kh, tn), lambda i, n: (1, n)),
            pl.BlockSpec((tm, d), lambda i, n: (i, 0)),
            pl.BlockSpec((1, d), lambda i, n: (0, 0)),
            pl.BlockSpec((1, d), lambda i, n: (0, 0)),
        ],
        out_specs=pl.BlockSpec((tm, d), lambda i, n: (i, 0)),
        compiler_params=_cparams(("parallel", "arbitrary")),
        name="proj_ln",
    )(a1, a2, w, w, x, g, b)


def _cross_ln_kernel(x_ref, wq_ref, kv_ref, wo_ref, g_ref, b_ref, o_ref, *, alpha, n_heads):
    x = x_ref[...]
    q = _dot(x.astype(BF16), wq_ref[...])
    kv = kv_ref[0]
    cw = n_heads * HEAD_DIM
    outs = []
    for h in range(n_heads):
        qh = q[:, h * HEAD_DIM:(h + 1) * HEAD_DIM].astype(BF16)
        kh = kv[:, h * HEAD_DIM:(h + 1) * HEAD_DIM]
        vh = kv[:, cw + h * HEAD_DIM:cw + (h + 1) * HEAD_DIM]
        logits = _dot_nt(qh, kh) * ATT_SCALE
        mx = jnp.max(logits, axis=-1, keepdims=True)
        e = jnp.exp(logits - mx)
        p = e / jnp.sum(e, axis=-1, keepdims=True)
        outs.append(_dot(p.astype(BF16), vh))
    o = jnp.concatenate(outs, axis=-1).astype(BF16)
    o_ref[...] = _dot(o, wo_ref[...])
    _residual_ln_inplace(x_ref, o_ref, g_ref, b_ref, alpha=alpha, scale=1.0)


def cross_ln(x, wq, kv, wo, g, b, *, alpha, tm, rows_per_batch):
    m, d = x.shape
    cw = wq.shape[1]
    nmem = kv.shape[1]
    steps_per_batch = rows_per_batch // tm
    return pl.pallas_call(
        functools.partial(_cross_ln_kernel, alpha=alpha, n_heads=cw // HEAD_DIM),
        out_shape=jax.ShapeDtypeStruct((m, d), F32),
        grid=(m // tm,),
        in_specs=[
            pl.BlockSpec((tm, d), lambda i: (i, 0)),
            _single((d, cw), lambda i: (0, 0)),
            pl.BlockSpec((1, nmem, 2 * cw), lambda i: (i // steps_per_batch, 0, 0)),
            _single((cw, d), lambda i: (0, 0)),
            pl.BlockSpec((1, d), lambda i: (0, 0)),
            pl.BlockSpec((1, d), lambda i: (0, 0)),
        ],
        out_specs=pl.BlockSpec((tm, d), lambda i: (i, 0)),
        compiler_params=_cparams(("parallel",)),
        name="cross_ln",
    )(x, wq, kv, wo, g, b)


def _split3(x, order):
    hi = x.astype(BF16).astype(F32)
    lo = x - hi
    parts = [hi if c == "h" else lo for c in order]
    parts.append(jnp.zeros((x.shape[0], IDX_K - 3 * x.shape[1]), F32))
    return jnp.concatenate(parts, axis=-1).astype(BF16)


def _dsa_prep_kernel(kv_ref, qi_ref, sm_ref, kvg_ref, kng_ref, knb_ref,
                     ckv_ref, ckvt_ref, k3_ref, qi3_ref, wt_ref):
    kv = kv_ref[...]
    ms = jnp.mean(kv * kv, axis=-1, keepdims=True)
    ckv = kv * lax.rsqrt(ms + LN_EPS) * kvg_ref[...]
    ckv_ref[...] = ckv.astype(BF16)
    ckvt_ref[0] = jnp.transpose(ckv).astype(BF16)

    sm = sm_ref[...]
    kidx = _layer_norm_rows(sm[:, :IDX_DIM], kng_ref[...], knb_ref[...])
    k3_ref[...] = _split3(kidx, "hlh")

    wt = jnp.transpose(sm)
    wt_ref[...] = wt[IDX_DIM:IDX_DIM + N_IDX_HEADS, :] * (IDX_HEAD_SCALE * IDX_SCALE)

    qi = qi_ref[...]
    for h in range(N_IDX_HEADS):
        qi3_ref[h] = _split3(qi[:, h * IDX_DIM:(h + 1) * IDX_DIM], "hhl")


def dsa_prep(proj, kv_g, kn_g, kn_b, *, seq):
    m = proj.shape[0]
    tq = Q_BLOCK
    nq = m // tq
    nqb = seq // tq
    return pl.pallas_call(
        _dsa_prep_kernel,
        out_shape=(
            jax.ShapeDtypeStruct((m, KV_LORA_RANK), BF16),
            jax.ShapeDtypeStruct((m // seq, KV_LORA_RANK, seq), BF16),
            jax.ShapeDtypeStruct((m, IDX_K), BF16),
            jax.ShapeDtypeStruct((nq * N_IDX_HEADS, tq, IDX_K), BF16),
            jax.ShapeDtypeStruct((nq * N_IDX_HEADS, tq), F32),
        ),
        grid=(nq,),
        in_specs=[
            pl.BlockSpec((tq, KV_LORA_RANK), lambda i: (i, COL_KV // KV_LORA_RANK)),
            pl.BlockSpec((tq, IDX_WIDTH), lambda i: (i, COL_QIDX // IDX_WIDTH)),
            pl.BlockSpec((tq, LANE), lambda i: (i, COL_SMALL // LANE)),
            pl.BlockSpec((1, KV_LORA_RANK), lambda i: (0, 0)),
            pl.BlockSpec((1, IDX_DIM), lambda i: (0, 0)),
            pl.BlockSpec((1, IDX_DIM), lambda i: (0, 0)),
        ],
        out_specs=(
            pl.BlockSpec((tq, KV_LORA_RANK), lambda i: (i, 0)),
            pl.BlockSpec((1, KV_LORA_RANK, tq), lambda i: (i // nqb, 0, i % nqb)),
            pl.BlockSpec((tq, IDX_K), lambda i: (i, 0)),
            pl.BlockSpec((N_IDX_HEADS, tq, IDX_K), lambda i: (i, 0, 0)),
            pl.BlockSpec((N_IDX_HEADS, tq), lambda i: (i, 0)),
        ),
        compiler_params=_cparams(("parallel",)),
        name="dsa_prep",
    )(proj, proj, proj, kv_g, kn_g, kn_b)


def _bucket_thresholds():
    max_exact = N_BUCKETS // 2
    rel = np.arange(max_exact, 4 * MAX_DISTANCE, dtype=np.float32)
    large = max_exact + (np.log(rel / np.float32(max_exact)) / np.float32(math.log(MAX_DISTANCE / max_exact))
                         * np.float32(N_BUCKETS - max_exact)).astype(np.int32)
    large = np.minimum(large, N_BUCKETS - 1)
    return [int(rel[np.argmax(large >= bkt)]) for bkt in range(max_exact + 1, N_BUCKETS)]


def _bias_table_kernel(rb_ref, o_ref):
    di = pl.program_id(0)
    max_exact = N_BUCKETS // 2
    sl = lax.broadcasted_iota(I32, (KEY_GROUP, Q_BLOCK), 0)
    tl = lax.broadcasted_iota(I32, (KEY_GROUP, Q_BLOCK), 1)
    rel = jnp.maximum(di * Q_BLOCK + tl - sl, 0)
    large = jnp.full(rel.shape, max_exact, I32)
    for th in _bucket_thresholds():
        large = large + (rel >= th).astype(I32)
    bucket = jnp.where(rel < max_exact, rel, large)
    for h in range(N_ATT_HEADS):
        far = rb_ref[N_BUCKETS - 1, h]
        val = jnp.zeros(rel.shape, F32)
        for bkt in range(N_BUCKETS - 1):
            val = jnp.where(bucket == bkt, rb_ref[bkt, h] - far, val)
        o_ref[0, :, h * Q_BLOCK:(h + 1) * Q_BLOCK] = val


def bias_tables(rel_bias):
    nh = rel_bias.shape[1]
    return pl.pallas_call(
        _bias_table_kernel,
        out_shape=jax.ShapeDtypeStruct((N_BIAS_TABLES, KEY_GROUP, nh * Q_BLOCK), F32),
        grid=(N_BIAS_TABLES,),
        in_specs=[pl.BlockSpec(memory_space=pltpu.SMEM)],
        out_specs=pl.BlockSpec((1, KEY_GROUP, nh * Q_BLOCK), lambda d: (d, 0, 0)),
        compiler_params=_cparams(("arbitrary",)),
        name="bias_tables",
    )(rel_bias)


def _dsa_kernel(q_ref, ckv_ref, ckvt_ref, k3_ref, qi3_ref, wt_ref, wuk_ref, wuv_ref, bias_ref, o_ref,
                keys_ref, qabs_ref, acc_ref, *, topk):
    j = pl.program_id(1)
    nh = N_ATT_HEADS
    tq = Q_BLOCK
    t_lane = j * tq + lax.broadcasted_iota(I32, (1, tq), 1)

    n_sgrp = (j + 1 + (SCORE_GROUP // tq - 1)) // (SCORE_GROUP // tq)
    qi3 = qi3_ref[...].reshape(N_IDX_HEADS * tq, IDX_K)

    def score_body(g, carry):
        r0 = pl.multiple_of(g * SCORE_GROUP, SCORE_GROUP)
        lt = _dot_nt(k3_ref[pl.ds(r0, SCORE_GROUP), :], qi3)
        sc = jnp.zeros((SCORE_GROUP, tq), F32)
        for h in range(N_IDX_HEADS):
            sc = sc + jnp.maximum(lt[:, h * tq:(h + 1) * tq], 0.0) * wt_ref[h:h + 1, :]
        s_row = r0 + lax.broadcasted_iota(I32, (SCORE_GROUP, 1), 0)
        sc = jnp.where(s_row <= t_lane, sc + 0.0, -jnp.inf)
        bits = pltpu.bitcast(sc, I32)
        keys_ref[pl.ds(r0, SCORE_GROUP), :] = jnp.where(bits < 0, bits ^ 0x7FFFFFFF, bits)
        return carry

    lax.fori_loop(0, n_sgrp, score_body, 0)

    def count_ge(cand):
        def body(g, part):
            r0 = pl.multiple_of(g * SCORE_GROUP, SCORE_GROUP)
            ind = jnp.where(keys_ref[pl.ds(r0, SCORE_GROUP), :] >= cand, 1.0, 0.0)
            return part + jnp.sum(ind.reshape(SCORE_GROUP // COUNT_ROWS, COUNT_ROWS, tq), axis=0)
        part = lax.fori_loop(0, n_sgrp, body, jnp.zeros((COUNT_ROWS, tq), F32))
        return jnp.sum(part, axis=0, keepdims=True)

    thr0 = jnp.where(count_ge(jnp.zeros((1, tq), I32)) >= topk, 0, INT_MIN).astype(I32)

    def bit_body(i, thr):
        cand = thr | jnp.left_shift(jnp.int32(1), 30 - i)
        return jnp.where(count_ge(cand) >= topk, cand, thr)

    thr = lax.fori_loop(0, 31, bit_body, thr0)

    n_ge = count_ge(thr)
    tie_lane = (n_ge > topk) & (thr > KEY_NEG_INF)
    has_tie = jnp.max(jnp.where(tie_lane, 1.0, 0.0)) > 0.0

    @pl.when(has_tie)
    def _():
        need = topk - count_ge(thr + 1)
        r_i = lax.broadcasted_iota(I32, (tq, tq), 0)
        c_i = lax.broadcasted_iota(I32, (tq, tq), 1)
        tri = jnp.where(c_i <= r_i, 1.0, 0.0).astype(BF16)

        def tie_body(kb, seen):
            r0 = pl.multiple_of(kb * tq, tq)
            blk = keys_ref[pl.ds(r0, tq), :]
            s_row = r0 + lax.broadcasted_iota(I32, (tq, 1), 0)
            eq = (blk == thr) & (s_row <= t_lane) & tie_lane
            rank = seen + _dot(tri, jnp.where(eq, 1.0, 0.0).astype(BF16))
            keys_ref[pl.ds(r0, tq), :] = jnp.where(eq & (rank > need), thr - 1, blk)
            return rank[tq - 1:tq, :]

        lax.fori_loop(0, j + 1, tie_body, jnp.zeros((1, tq), F32))

    for h in range(nh):
        qh = q_ref[:, h * HEAD_DIM:(h + 1) * HEAD_DIM].astype(BF16)
        qabs_ref[:, h * tq:(h + 1) * tq] = (_dot_nt(wuk_ref[h], qh) * ATT_SCALE).astype(BF16)

    @pl.when(j == 0)
    def _():
        acc_ref[...] = jnp.zeros(acc_ref.shape, F32)

    def attend(c0, carry, bias_di):
        kg = KEY_GROUP
        m, l = carry
        s_all = _dot(ckv_ref[pl.ds(c0, kg), :], qabs_ref[...])
        s_row = c0 + lax.broadcasted_iota(I32, (kg, 1), 0)
        sel = (keys_ref[pl.ds(c0, kg), :] >= thr) & (s_row <= t_lane)
        addm = jnp.where(sel, 0.0, NEG_BIG)
        ps, ms, ls, corrs = [], [], [], []
        for h in range(nh):
            lanes = slice(h * tq, (h + 1) * tq)
            s = s_all[:, lanes] + addm
            if bias_di is not None:
                s = s + bias_ref[bias_di, :, lanes]
            m_prev = m[:, lanes]
            m_new = jnp.maximum(m_prev, jnp.max(s, axis=0, keepdims=True))
            p = jnp.exp(s - m_new)
            corr = jnp.exp(m_prev - m_new)
            ls.append(corr * l[:, lanes] + jnp.sum(p, axis=0, keepdims=True))
            ms.append(m_new)
            corrs.append(corr)
            ps.append(p.astype(BF16))
        pt = jnp.concatenate(ps, axis=1)
        corr_all = jnp.concatenate(corrs, axis=1)
        prev = jnp.where(c0 == 0, 0.0, acc_ref[...] * corr_all)
        acc_ref[...] = prev + _dot(ckvt_ref[0, :, pl.ds(c0, kg)], pt)
        return jnp.concatenate(ms, axis=1), jnp.concatenate(ls, axis=1)

    def at_group(g):
        return pl.multiple_of(g * KEY_GROUP, KEY_GROUP)

    carry = (jnp.full((1, nh * tq), NEG_BIG, F32), jnp.zeros((1, nh * tq), F32))
    n_far = jnp.maximum(j // 2 - 1 + j % 2, 0)
    carry = lax.fori_loop(0, n_far, lambda g, cr: attend(at_group(g), cr, None), carry)
    carry = lax.cond((j % 2 == 0) & (j >= 2), lambda cr: attend(at_group(j // 2 - 1), cr, 2), lambda cr: cr, carry)
    _, l = attend(at_group(j // 2), carry, j % 2)

    o_lat = (acc_ref[...] * (1.0 / l)).astype(BF16)
    for h in range(nh):
        o_ref[:, h * HEAD_DIM:(h + 1) * HEAD_DIM] = _dot_tn(
            o_lat[:, h * tq:(h + 1) * tq], wuv_ref[h]).astype(o_ref.dtype)


def dsa_attention(proj, ckv, ckvt, k3, qi3, wt, wuk, wuv, bias, *, batch, seq, topk):
    tq = Q_BLOCK
    nq = seq // tq
    nh = N_ATT_HEADS
    r = KV_LORA_RANK
    seq_pad = -(-seq // SCORE_GROUP) * SCORE_GROUP
    return pl.pallas_call(
        functools.partial(_dsa_kernel, topk=topk),
        out_shape=jax.ShapeDtypeStruct((batch * seq, ATT_WIDTH), BF16),
        grid=(batch, nq),
        in_specs=[
            pl.BlockSpec((tq, ATT_WIDTH), lambda b, j: (b * nq + j, 0)),
            _single((seq, r), lambda b, j: (b, 0)),
            _single((1, r, seq), lambda b, j: (b, 0, 0)),
            _single((seq, IDX_K), lambda b, j: (b, 0)),
            pl.BlockSpec((N_IDX_HEADS, tq, IDX_K), lambda b, j: (b * nq + j, 0, 0)),
            pl.BlockSpec((N_IDX_HEADS, tq), lambda b, j: (b * nq + j, 0)),
            _single((nh, r, HEAD_DIM), lambda b, j: (0, 0, 0)),
            _single((nh, r, HEAD_DIM), lambda b, j: (0, 0, 0)),
            _single((N_BIAS_TABLES, KEY_GROUP, nh * tq), lambda b, j: (0, 0, 0)),
        ],
        out_specs=pl.BlockSpec((tq, ATT_WIDTH), lambda b, j: (b * nq + j, 0)),
        scratch_shapes=[
            pltpu.VMEM((seq_pad, tq), I32),
            pltpu.VMEM((r, nh * tq), BF16),
            pltpu.VMEM((r, nh * tq), F32),
        ],
        compiler_params=_cparams(("parallel", "arbitrary")),
        name="dsa",
    )(proj, ckv, ckvt, k3, qi3, wt, wuk, wuv, bias)


def _cumsum_rows(x):
    n = x.shape[0]
    row = lax.broadcasted_iota(I32, (n, 1), 0)
    sh = 1
    while sh < n:
        x = x + jnp.where(row >= sh, pltpu.roll(x, sh, 0), 0.0)
        sh *= 2
    return x


def _hgrn2_kernel(q_ref, f_ref, i_ref, g_ref, lb_ref, ng_ref, o_ref, st_ref, *, layer):
    c = REC_CHUNK
    dk = HEAD_DIM

    @pl.when(pl.program_id(2) == 0)
    def _():
        st_ref[...] = jnp.zeros_like(st_ref)

    lbr = lb_ref[...]
    e = jnp.exp(lbr - jnp.max(lbr, axis=0, keepdims=True))
    pr = e / jnp.sum(e, axis=0, keepdims=True)
    lb_all = jnp.zeros((1, lbr.shape[1]), F32)
    for i in range(1, layer + 1):
        lb_all = lb_all + pr[i:i + 1, :]

    row = lax.broadcasted_iota(I32, (c, 1), 0)
    col = lax.broadcasted_iota(I32, (1, c), 1)
    sub = 8
    nsub = c // sub
    tl3 = lax.broadcasted_iota(I32, (1, sub, 1), 1)
    g3 = lax.broadcasted_iota(I32, (nsub, 1, 1), 0)
    col3 = lax.broadcasted_iota(I32, (1, 1, c), 2)
    levels = []
    m = c // 2
    while m >= sub:
        levels.append(m)
        m //= 2

    def head_chunk(r0, hh):
        lanes = slice(hh * dk, (hh + 1) * dk)
        lb = lb_all[:, lanes]
        q = q_ref[pl.ds(r0, c), lanes]
        qf = q * jax.nn.sigmoid(q)
        f = lb + (1.0 - lb) * jax.nn.sigmoid(f_ref[pl.ds(r0, c), lanes])
        k = 1.0 - f
        v = i_ref[pl.ds(r0, c), lanes].astype(BF16)
        cum = _cumsum_rows(jnp.log(f))

        a = jnp.zeros((c, c), F32)
        for m in levels:
            first = (row & (2 * m - 1)) < m
            bnd = jnp.concatenate(
                [jnp.broadcast_to(cum[b * 2 * m + m - 1:b * 2 * m + m, :], (2 * m, dk))
                 for b in range(c // (2 * m))], axis=0)
            qs = jnp.where(first, 0.0, qf * jnp.exp(cum - bnd)).astype(BF16)
            ks = jnp.where(first, k * jnp.exp(bnd - cum), 0.0).astype(BF16)
            same = (row & -(2 * m)) == (col & -(2 * m))
            a = a + jnp.where(same, _dot_nt(qs, ks), 0.0)
        cum3 = cum.reshape(nsub, sub, dk)
        k3 = k.reshape(nsub, sub, dk)
        qf3 = qf.reshape(nsub, sub, dk)
        a3 = jnp.zeros((nsub, sub, c), F32)
        for s in range(sub):
            d = jnp.where(tl3 >= s, cum3 - cum3[:, s:s + 1, :], -jnp.inf)
            prod = qf3 * k3[:, s:s + 1, :] * jnp.exp(d)
            colsum = jnp.sum(prod, axis=-1, keepdims=True)
            a3 = a3 + jnp.where(col3 == g3 * sub + s, colsum, 0.0)
        a = a + a3.reshape(c, c)

        st = st_ref[hh]
        o = _dot(a.astype(BF16), v) + _dot_nt((qf * jnp.exp(cum)).astype(BF16), st.astype(BF16))
        last = cum[c - 1:c, :]
        kl = (k * jnp.exp(last - cum)).astype(BF16)
        st_ref[hh] = st * jnp.exp(last) + _dot_tn(v, kl)

        ms = jnp.mean(o * o, axis=-1, keepdims=True)
        gt = g_ref[pl.ds(r0, c), lanes]
        on = o * lax.rsqrt(ms + LN_EPS) * ng_ref[:, lanes]
        o_ref[pl.ds(r0, c), lanes] = (on * (gt * jax.nn.sigmoid(gt))).astype(o_ref.dtype)

    def chunk_body(ci, carry):
        r0 = pl.multiple_of(ci * c, c)
        for hh in range(REC_HEADS_PER_STEP):
            head_chunk(r0, hh)
        return carry

    lax.fori_loop(0, q_ref.shape[0] // c, chunk_body, 0)


def hgrn2(proj, rec_lb, norm_g, *, layer, batch, seq):
    tb = min(REC_TBLOCK, seq)
    nt = seq // tb
    hw = REC_HEADS_PER_STEP * HEAD_DIM
    ng = N_REC_HEADS // REC_HEADS_PER_STEP

    def seg(k):
        return pl.BlockSpec((tb, hw), lambda b, h, t, k=k: (b * nt + t, k * ng + h))

    return pl.pallas_call(
        functools.partial(_hgrn2_kernel, layer=layer),
        out_shape=jax.ShapeDtypeStruct((batch * seq, REC_WIDTH), BF16),
        grid=(batch, ng, nt),
        in_specs=[
            seg(0), seg(1), seg(2), seg(3),
            pl.BlockSpec((rec_lb.shape[0], hw), lambda b, h, t: (0, h)),
            pl.BlockSpec((1, hw), lambda b, h, t: (0, h)),
        ],
        out_specs=pl.BlockSpec((tb, hw), lambda b, h, t: (b * nt + t, h)),
        scratch_shapes=[pltpu.VMEM((REC_HEADS_PER_STEP, HEAD_DIM, HEAD_DIM), F32)],
        compiler_params=_cparams(("parallel", "parallel", "arbitrary")),
        name="hgrn2",
    )(proj, proj, proj, proj, rec_lb, norm_g)


def _pack_w_in(w):
    kv0 = ATT_WIDTH
    qi0 = kv0 + KV_LORA_RANK
    sm0 = qi0 + IDX_WIDTH
    sm1 = sm0 + IDX_DIM + N_IDX_HEADS
    pad = jnp.zeros((w.shape[0], SMALL_WIDTH - IDX_DIM - N_IDX_HEADS), BF16)
    parts = [w[:, :kv0], w[:, qi0:sm0], w[:, kv0:qi0], w[:, sm0:sm1]]
    w_att = jnp.concatenate([p.astype(BF16) for p in parts] + [pad], axis=1)
    return w_att, w[:, sm1:].astype(BF16)


def _row_tile(m, want):
    t = min(want, m)
    while m % t:
        t //= 2
    return t


def kernel(x, mem, w_in, w_uk, w_uv, kv_norm_g, idx_kn_g, idx_kn_b, rec_lb, rec_norm_g, w_out, rel_bias,
           wq_c, wk_c, wv_c, wo_c, ffn_gate, ffn_up, ffn_down, ln_g, ln_b):
    batch, seq, d = x.shape
    depth = w_in.shape[0]
    m = batch * seq
    alpha = (2 * depth) ** 0.25
    topk = min(TOPK_MAX, seq // 4)
    d_ff = ffn_gate.shape[-1]
    tf = 256 if d_ff % 256 == 0 else 128

    xf = x.reshape(m, d)
    mem_f = mem.reshape(batch * mem.shape[1], d)
    bias = bias_tables(rel_bias)
    rec_lb = rec_lb.astype(F32)

    ffn_order = [(l, i) for l in range(depth) for i in range(2)]
    ffn_w = {ffn_order[0]: tuple(w[0, 0].astype(BF16) for w in (ffn_gate, ffn_up, ffn_down))}

    proj_host = {(0, 0): 0, **{(l - 1, 1): l for l in range(1, depth)}}
    proj_w = {}

    def ffn(xf, l, i):
        pos = ffn_order.index((l, i))
        nxt = ffn_order[pos + 1] if pos + 1 < len(ffn_order) else None
        pl_layer = proj_host.get((l, i))
        out, cast = ffn_ln(xf, *ffn_w.pop((l, i)), ln_g[l, 3 * i:3 * i + 1], ln_b[l, 3 * i:3 * i + 1],
                           alpha=alpha, tm=_row_tile(m, 512), tf=tf,
                           cast_next=None if nxt is None else (ffn_gate, ffn_up, ffn_down) + nxt,
                           cast_proj=None if pl_layer is None else (w_in, pl_layer))
        if nxt is not None:
            ffn_w[nxt] = cast[:3]
        if pl_layer is not None:
            proj_w[pl_layer] = cast[-1]
        return out

    for l in range(depth):
        xf = ffn(xf, l, 0)

        w_att, w_rec = _pack_w_in(proj_w.pop(l))
        proj_a = matmul(xf, w_att, tm=_row_tile(m, 512), tn=768, out_dtype=F32)
        proj_r = matmul(xf, w_rec, tm=_row_tile(m, 512), tn=1024, out_dtype=F32)
        ckv, ckvt, k3, qi3, wt = dsa_prep(proj_a, kv_norm_g[l][None], idx_kn_g[l][None], idx_kn_b[l][None], seq=seq)
        o_att = dsa_attention(proj_a, ckv, ckvt, k3, qi3, wt, w_uk[l].astype(BF16), w_uv[l].astype(BF16), bias,
                              batch=batch, seq=seq, topk=topk)
        o_rec = hgrn2(proj_r, rec_lb, rec_norm_g[l][None], layer=l, batch=batch, seq=seq)
        xf = proj_ln(o_att, o_rec, w_out[l].astype(BF16), xf, ln_g[l, 1:2], ln_b[l, 1:2],
                     alpha=alpha, tm=_row_tile(m, 512), tn=512)

        kv = matmul(mem_f, jnp.concatenate([wk_c[l], wv_c[l]], axis=1).astype(BF16),
                    tm=_row_tile(mem_f.shape[0], 512), tn=256, out_dtype=BF16)
        kv = kv.reshape(batch, mem.shape[1], kv.shape[-1])
        xf = cross_ln(xf, wq_c[l].astype(BF16), kv, wo_c[l].astype(BF16), ln_g[l, 2:3], ln_b[l, 2:3],
                      alpha=alpha, tm=_row_tile(seq, 256), rows_per_batch=seq)

        xf = ffn(xf, l, 1)
    return xf.reshape(batch, seq, d)
```

```python
import functools
import math

import jax
import jax.numpy as jnp
import numpy as np
from jax import lax
from jax.experimental import pallas as pl
from jax.experimental.pallas import tpu as pltpu

F32 = jnp.float32
BF16 = jnp.bfloat16
I32 = jnp.int32

LANE = 128
VMEM_LIMIT_V7X = 60 * 1024 * 1024

HEAD_DIM = 128
N_ATT_HEADS = 16
N_REC_HEADS = 16
KV_LORA_RANK = 512
N_IDX_HEADS = 16
IDX_DIM = 64
TOPK_MAX = 256
Q_BLOCK = 128
N_BUCKETS = 32
MAX_DISTANCE = 128
N_CROSS_HEADS = 4
LN_EPS = 1e-5
ATT_SCALE = HEAD_DIM ** -0.5
IDX_SCALE = IDX_DIM ** -0.5
IDX_HEAD_SCALE = N_IDX_HEADS ** -0.5

ATT_WIDTH = N_ATT_HEADS * HEAD_DIM
IDX_WIDTH = N_IDX_HEADS * IDX_DIM
REC_WIDTH = N_REC_HEADS * HEAD_DIM
COL_QIDX = ATT_WIDTH
COL_KV = COL_QIDX + IDX_WIDTH
COL_SMALL = COL_KV + KV_LORA_RANK
SMALL_WIDTH = 256
COL_REC = COL_SMALL + SMALL_WIDTH

NEG_BIG = -1e30
INT_MIN = -2 ** 31
KEY_NEG_INF = (0xFF800000 - 2 ** 32) ^ 0x7FFFFFFF

REC_CHUNK = 64
REC_TBLOCK = 512
REC_HEADS_PER_STEP = 16
KEY_GROUP = 256
N_BIAS_TABLES = 3
SCORE_GROUP = 512
IDX_K = 256
COUNT_ROWS = 64
LN_ROWS = 128


def _cparams(sem):
    return pltpu.CompilerParams(dimension_semantics=sem, vmem_limit_bytes=VMEM_LIMIT_V7X)


def _single(shape, imap):
    return pl.BlockSpec(shape, imap, pipeline_mode=pl.Buffered(1))


def _layer_norm_rows(y, g, b):
    mu = jnp.mean(y, axis=-1, keepdims=True)
    d = y - mu
    var = jnp.mean(d * d, axis=-1, keepdims=True)
    return d * lax.rsqrt(var + LN_EPS) * g + b


def _residual_ln_inplace(x_ref, o_ref, g_ref, b_ref, *, alpha, scale):
    g = g_ref[...]
    b = b_ref[...]

    def body(i, carry):
        rows = pl.ds(pl.multiple_of(i * LN_ROWS, LN_ROWS), LN_ROWS)
        y = alpha * x_ref[rows, :] + scale * o_ref[rows, :]
        o_ref[rows, :] = _layer_norm_rows(y, g, b)
        return carry

    lax.fori_loop(0, o_ref.shape[0] // LN_ROWS, body, 0)


def _dot(a, b):
    return jnp.dot(a, b, preferred_element_type=F32)


def _dot_nt(a, b):
    return lax.dot_general(a, b, (((1,), (1,)), ((), ())), preferred_element_type=F32)


def _dot_tn(a, b):
    return lax.dot_general(a, b, (((0,), (0,)), ((), ())), preferred_element_type=F32)


def _ffn_ln_kernel(x_ref, wg_ref, wu_ref, wd_ref, g_ref, b_ref, *rest, alpha, n_cast):
    cast_in = rest[:n_cast]
    o_ref = rest[n_cast]
    cast_out = rest[n_cast + 1:2 * n_cast + 1]
    xb_ref = rest[2 * n_cast + 1]
    f = pl.program_id(1)

    @pl.when(f == 0)
    def _():
        xb_ref[...] = x_ref[...].astype(BF16)
        o_ref[...] = jnp.zeros_like(o_ref)

    xb = xb_ref[...]
    h = _dot(xb, wg_ref[...])
    u = _dot(xb, wu_ref[...])
    a = (h * jax.nn.sigmoid(h) * u).astype(BF16)
    o_ref[...] += _dot(a, wd_ref[...])

    for src, dst in zip(cast_in, cast_out):
        dst[...] = src[0, 0].astype(BF16)

    @pl.when(f == pl.num_programs(1) - 1)
    def _():
        _residual_ln_inplace(x_ref, o_ref, g_ref, b_ref, alpha=alpha, scale=0.5)


def ffn_ln(x, wg, wu, wd, g, b, *, alpha, tm, tf, cast_next=None):
    m, d = x.shape
    d_ff = wg.shape[1]
    ni, nf = m // tm, d_ff // tf
    in_specs = [
        pl.BlockSpec((tm, d), lambda i, f: (i, 0)),
        pl.BlockSpec((d, tf), lambda i, f: (0, f)),
        pl.BlockSpec((d, tf), lambda i, f: (0, f)),
        pl.BlockSpec((tf, d), lambda i, f: (f, 0)),
        pl.BlockSpec((1, d), lambda i, f: (0, 0)),
        pl.BlockSpec((1, d), lambda i, f: (0, 0)),
    ]
    out_shape = [jax.ShapeDtypeStruct((m, d), F32)]
    out_specs = [pl.BlockSpec((tm, d), lambda i, f: (i, 0))]
    operands = [x, wg, wu, wd, g, b]
    n_cast = 0
    if cast_next is not None:
        ng, nu, nd, nl, nw = cast_next
        td = d // ni
        n_cast = 3
        in_specs += [
            pl.BlockSpec((1, 1, td, tf), lambda i, f: (nl, nw, i, f)),
            pl.BlockSpec((1, 1, td, tf), lambda i, f: (nl, nw, i, f)),
            pl.BlockSpec((1, 1, tf, td), lambda i, f: (nl, nw, f, i)),
        ]
        out_shape += [jax.ShapeDtypeStruct((d, d_ff), BF16), jax.ShapeDtypeStruct((d, d_ff), BF16),
                      jax.ShapeDtypeStruct((d_ff, d), BF16)]
        out_specs += [
            pl.BlockSpec((td, tf), lambda i, f: (i, f)),
            pl.BlockSpec((td, tf), lambda i, f: (i, f)),
            pl.BlockSpec((tf, td), lambda i, f: (f, i)),
        ]
        operands += [ng, nu, nd]
    outs = pl.pallas_call(
        functools.partial(_ffn_ln_kernel, alpha=alpha, n_cast=n_cast),
        out_shape=tuple(out_shape),
        grid=(ni, nf),
        in_specs=in_specs,
        out_specs=tuple(out_specs),
        scratch_shapes=[pltpu.VMEM((tm, d), BF16)],
        compiler_params=_cparams(("parallel", "arbitrary")),
        name="ffn_ln",
    )(*operands)
    return outs[0], tuple(outs[1:])


def _matmul_kernel(x_ref, w_ref, o_ref, xb_ref):
    @pl.when(pl.program_id(1) == 0)
    def _():
        xb_ref[...] = x_ref[...].astype(BF16)

    o_ref[...] = _dot(xb_ref[...], w_ref[...]).astype(o_ref.dtype)


def matmul(x, w, *, tm, tn, out_dtype):
    m, k = x.shape
    n = w.shape[1]
    return pl.pallas_call(
        _matmul_kernel,
        out_shape=jax.ShapeDtypeStruct((m, n), out_dtype),
        grid=(m // tm, n // tn),
        in_specs=[
            pl.BlockSpec((tm, k), lambda i, j: (i, 0)),
            pl.BlockSpec((k, tn), lambda i, j: (0, j)),
        ],
        out_specs=pl.BlockSpec((tm, tn), lambda i, j: (i, j)),
        scratch_shapes=[pltpu.VMEM((tm, k), BF16)],
        compiler_params=_cparams(("parallel", "arbitrary")),
        name="matmul",
    )(x, w)


def _proj_ln_kernel(a1_ref, a2_ref, w1_ref, w2_ref, x_ref, g_ref, b_ref, o_ref, *, alpha, tn):
    n = pl.program_id(1)
    c0 = pl.multiple_of(n * tn, tn)
    o_ref[:, pl.ds(c0, tn)] = _dot(a1_ref[...], w1_ref[...]) + _dot(a2_ref[...], w2_ref[...])

    @pl.when(n == pl.num_programs(1) - 1)
    def _():
        _residual_ln_inplace(x_ref, o_ref, g_ref, b_ref, alpha=alpha, scale=1.0)


def proj_ln(a1, a2, w, x, g, b, *, alpha, tm, tn):
    m, kh = a1.shape
    d = w.shape[1]
    return pl.pallas_call(
        functools.partial(_proj_ln_kernel, alpha=alpha, tn=tn),
        out_shape=jax.ShapeDtypeStruct((m, d), F32),
        grid=(m // tm, d // tn),
        in_specs=[
            pl.BlockSpec((tm, kh), lambda i, n: (i, 0)),
            pl.BlockSpec((tm, kh), lambda i, n: (i, 0)),
            pl.BlockSpec((kh, tn), lambda i, n: (0, n)),
            pl.BlockSpec((kh, tn), lambda i, n: (1, n)),
            pl.BlockSpec((tm, d), lambda i, n: (i, 0)),
            pl.BlockSpec((1, d), lambda i, n: (0, 0)),
            pl.BlockSpec((1, d), lambda i, n: (0, 0)),
        ],
        out_specs=pl.BlockSpec((tm, d), lambda i, n: (i, 0)),
        compiler_params=_cparams(("parallel", "arbitrary")),
        name="proj_ln",
    )(a1, a2, w, w, x, g, b)


def _cross_ln_kernel(x_ref, wq_ref, kv_ref, wo_ref, g_ref, b_ref, o_ref, *, alpha, n_heads):
    x = x_ref[...]
    q = _dot(x.astype(BF16), wq_ref[...])
    kv = kv_ref[0]
    cw = n_heads * HEAD_DIM
    outs = []
    for h in range(n_heads):
        qh = q[:, h * HEAD_DIM:(h + 1) * HEAD_DIM].astype(BF16)
        kh = kv[:, h * HEAD_DIM:(h + 1) * HEAD_DIM]
        vh = kv[:, cw + h * HEAD_DIM:cw + (h + 1) * HEAD_DIM]
        logits = _dot_nt(qh, kh) * ATT_SCALE
        mx = jnp.max(logits, axis=-1, keepdims=True)
        e = jnp.exp(logits - mx)
        p = e / jnp.sum(e, axis=-1, keepdims=True)
        outs.append(_dot(p.astype(BF16), vh))
    o = jnp.concatenate(outs, axis=-1).astype(BF16)
    o_ref[...] = _dot(o, wo_ref[...])
    _residual_ln_inplace(x_ref, o_ref, g_ref, b_ref, alpha=alpha, scale=1.0)


def cross_ln(x, wq, kv, wo, g, b, *, alpha, tm, rows_per_batch):
    m, d = x.shape
    cw = wq.shape[1]
    nmem = kv.shape[1]
    steps_per_batch = rows_per_batch // tm
    return pl.pallas_call(
        functools.partial(_cross_ln_kernel, alpha=alpha, n_heads=cw // HEAD_DIM),
        out_shape=jax.ShapeDtypeStruct((m, d), F32),
        grid=(m // tm,),
        in_specs=[
            pl.BlockSpec((tm, d), lambda i: (i, 0)),
            _single((d, cw), lambda i: (0, 0)),
            pl.BlockSpec((1, nmem, 2 * cw), lambda i: (i // steps_per_batch, 0, 0)),
            _single((cw, d), lambda i: (0, 0)),
            pl.BlockSpec((1, d), lambda i: (0, 0)),
            pl.BlockSpec((1, d), lambda i: (0, 0)),
        ],
        out_specs=pl.BlockSpec((tm, d), lambda i: (i, 0)),
        compiler_params=_cparams(("parallel",)),
        name="cross_ln",
    )(x, wq, kv, wo, g, b)


def _split3(x, order):
    hi = x.astype(BF16).astype(F32)
    lo = x - hi
    parts = [hi if c == "h" else lo for c in order]
    parts.append(jnp.zeros((x.shape[0], IDX_K - 3 * x.shape[1]), F32))
    return jnp.concatenate(parts, axis=-1).astype(BF16)


def _dsa_prep_kernel(kv_ref, qi_ref, sm_ref, kvg_ref, kng_ref, knb_ref,
                     ckv_ref, ckvt_ref, k3_ref, qi3_ref, wt_ref):
    kv = kv_ref[...]
    ms = jnp.mean(kv * kv, axis=-1, keepdims=True)
    ckv = kv * lax.rsqrt(ms + LN_EPS) * kvg_ref[...]
    ckv_ref[...] = ckv.astype(BF16)
    ckvt_ref[0] = jnp.transpose(ckv).astype(BF16)

    sm = sm_ref[...]
    kidx = _layer_norm_rows(sm[:, :IDX_DIM], kng_ref[...], knb_ref[...])
    k3_ref[...] = _split3(kidx, "hlh")

    wt = jnp.transpose(sm)
    wt_ref[...] = wt[IDX_DIM:IDX_DIM + N_IDX_HEADS, :] * (IDX_HEAD_SCALE * IDX_SCALE)

    qi = qi_ref[...]
    for h in range(N_IDX_HEADS):
        qi3_ref[h] = _split3(qi[:, h * IDX_DIM:(h + 1) * IDX_DIM], "hhl")


def dsa_prep(proj, kv_g, kn_g, kn_b, *, seq):
    m = proj.shape[0]
    tq = Q_BLOCK
    nq = m // tq
    nqb = seq // tq
    return pl.pallas_call(
        _dsa_prep_kernel,
        out_shape=(
            jax.ShapeDtypeStruct((m, KV_LORA_RANK), BF16),
            jax.ShapeDtypeStruct((m // seq, KV_LORA_RANK, seq), BF16),
            jax.ShapeDtypeStruct((m, IDX_K), BF16),
            jax.ShapeDtypeStruct((nq * N_IDX_HEADS, tq, IDX_K), BF16),
            jax.ShapeDtypeStruct((nq * N_IDX_HEADS, tq), F32),
        ),
        grid=(nq,),
        in_specs=[
            pl.BlockSpec((tq, KV_LORA_RANK), lambda i: (i, COL_KV // KV_LORA_RANK)),
            pl.BlockSpec((tq, IDX_WIDTH), lambda i: (i, COL_QIDX // IDX_WIDTH)),
            pl.BlockSpec((tq, LANE), lambda i: (i, COL_SMALL // LANE)),
            pl.BlockSpec((1, KV_LORA_RANK), lambda i: (0, 0)),
            pl.BlockSpec((1, IDX_DIM), lambda i: (0, 0)),
            pl.BlockSpec((1, IDX_DIM), lambda i: (0, 0)),
        ],
        out_specs=(
            pl.BlockSpec((tq, KV_LORA_RANK), lambda i: (i, 0)),
            pl.BlockSpec((1, KV_LORA_RANK, tq), lambda i: (i // nqb, 0, i % nqb)),
            pl.BlockSpec((tq, IDX_K), lambda i: (i, 0)),
            pl.BlockSpec((N_IDX_HEADS, tq, IDX_K), lambda i: (i, 0, 0)),
            pl.BlockSpec((N_IDX_HEADS, tq), lambda i: (i, 0)),
        ),
        compiler_params=_cparams(("parallel",)),
        name="dsa_prep",
    )(proj, proj, proj, kv_g, kn_g, kn_b)


def _bucket_thresholds():
    max_exact = N_BUCKETS // 2
    rel = np.arange(max_exact, 4 * MAX_DISTANCE, dtype=np.float32)
    large = max_exact + (np.log(rel / np.float32(max_exact)) / np.float32(math.log(MAX_DISTANCE / max_exact))
                         * np.float32(N_BUCKETS - max_exact)).astype(np.int32)
    large = np.minimum(large, N_BUCKETS - 1)
    return [int(rel[np.argmax(large >= bkt)]) for bkt in range(max_exact + 1, N_BUCKETS)]


def _bias_table_kernel(rb_ref, o_ref):
    di = pl.program_id(0)
    max_exact = N_BUCKETS // 2
    sl = lax.broadcasted_iota(I32, (KEY_GROUP, Q_BLOCK), 0)
    tl = lax.broadcasted_iota(I32, (KEY_GROUP, Q_BLOCK), 1)
    rel = jnp.maximum(di * Q_BLOCK + tl - sl, 0)
    large = jnp.full(rel.shape, max_exact, I32)
    for th in _bucket_thresholds():
        large = large + (rel >= th).astype(I32)
    bucket = jnp.where(rel < max_exact, rel, large)
    for h in range(N_ATT_HEADS):
        far = rb_ref[N_BUCKETS - 1, h]
        val = jnp.zeros(rel.shape, F32)
        for bkt in range(N_BUCKETS - 1):
            val = jnp.where(bucket == bkt, rb_ref[bkt, h] - far, val)
        o_ref[0, :, h * Q_BLOCK:(h + 1) * Q_BLOCK] = val


def bias_tables(rel_bias):
    nh = rel_bias.shape[1]
    return pl.pallas_call(
        _bias_table_kernel,
        out_shape=jax.ShapeDtypeStruct((N_BIAS_TABLES, KEY_GROUP, nh * Q_BLOCK), F32),
        grid=(N_BIAS_TABLES,),
        in_specs=[pl.BlockSpec(memory_space=pltpu.SMEM)],
        out_specs=pl.BlockSpec((1, KEY_GROUP, nh * Q_BLOCK), lambda d: (d, 0, 0)),
        compiler_params=_cparams(("arbitrary",)),
        name="bias_tables",
    )(rel_bias)


def _dsa_kernel(q_ref, ckv_ref, ckvt_ref, k3_ref, qi3_ref, wt_ref, wuk_ref, wuv_ref, bias_ref, o_ref,
                keys_ref, qabs_ref, acc_ref, *, topk):
    j = pl.program_id(1)
    nh = N_ATT_HEADS
    tq = Q_BLOCK
    t_lane = j * tq + lax.broadcasted_iota(I32, (1, tq), 1)

    n_sgrp = (j + 1 + (SCORE_GROUP // tq - 1)) // (SCORE_GROUP // tq)
    qi3 = qi3_ref[...].reshape(N_IDX_HEADS * tq, IDX_K)

    def score_body(g, carry):
        r0 = pl.multiple_of(g * SCORE_GROUP, SCORE_GROUP)
        lt = _dot_nt(k3_ref[pl.ds(r0, SCORE_GROUP), :], qi3)
        sc = jnp.zeros((SCORE_GROUP, tq), F32)
        for h in range(N_IDX_HEADS):
            sc = sc + jnp.maximum(lt[:, h * tq:(h + 1) * tq], 0.0) * wt_ref[h:h + 1, :]
        s_row = r0 + lax.broadcasted_iota(I32, (SCORE_GROUP, 1), 0)
        sc = jnp.where(s_row <= t_lane, sc + 0.0, -jnp.inf)
        bits = pltpu.bitcast(sc, I32)
        keys_ref[pl.ds(r0, SCORE_GROUP), :] = jnp.where(bits < 0, bits ^ 0x7FFFFFFF, bits)
        return carry

    lax.fori_loop(0, n_sgrp, score_body, 0)

    def count_ge(cand):
        def body(g, part):
            r0 = pl.multiple_of(g * SCORE_GROUP, SCORE_GROUP)
            ind = jnp.where(keys_ref[pl.ds(r0, SCORE_GROUP), :] >= cand, 1.0, 0.0)
            return part + jnp.sum(ind.reshape(SCORE_GROUP // COUNT_ROWS, COUNT_ROWS, tq), axis=0)
        part = lax.fori_loop(0, n_sgrp, body, jnp.zeros((COUNT_ROWS, tq), F32))
        return jnp.sum(part, axis=0, keepdims=True)

    thr0 = jnp.where(count_ge(jnp.zeros((1, tq), I32)) >= topk, 0, INT_MIN).astype(I32)

    def bit_body(i, thr):
        cand = thr | jnp.left_shift(jnp.int32(1), 30 - i)
        return jnp.where(count_ge(cand) >= topk, cand, thr)

    thr = lax.fori_loop(0, 31, bit_body, thr0)

    n_ge = count_ge(thr)
    tie_lane = (n_ge > topk) & (thr > KEY_NEG_INF)
    has_tie = jnp.max(jnp.where(tie_lane, 1.0, 0.0)) > 0.0

    @pl.when(has_tie)
    def _():
        need = topk - count_ge(thr + 1)
        r_i = lax.broadcasted_iota(I32, (tq, tq), 0)
        c_i = lax.broadcasted_iota(I32, (tq, tq), 1)
        tri = jnp.where(c_i <= r_i, 1.0, 0.0).astype(BF16)

        def tie_body(kb, seen):
            r0 = pl.multiple_of(kb * tq, tq)
            blk = keys_ref[pl.ds(r0, tq), :]
            s_row = r0 + lax.broadcasted_iota(I32, (tq, 1), 0)
            eq = (blk == thr) & (s_row <= t_lane) & tie_lane
            rank = seen + _dot(tri, jnp.where(eq, 1.0, 0.0).astype(BF16))
            keys_ref[pl.ds(r0, tq), :] = jnp.where(eq & (rank > need), thr - 1, blk)
            return rank[tq - 1:tq, :]

        lax.fori_loop(0, j + 1, tie_body, jnp.zeros((1, tq), F32))

    for h in range(nh):
        qh = q_ref[:, h * HEAD_DIM:(h + 1) * HEAD_DIM].astype(BF16)
        qabs_ref[:, h * tq:(h + 1) * tq] = (_dot_nt(wuk_ref[h], qh) * ATT_SCALE).astype(BF16)

    acc_ref[...] = jnp.zeros(acc_ref.shape, F32)

    def attend(c0, carry, bias_di):
        kg = KEY_GROUP
        m, l = carry
        s_all = _dot(ckv_ref[pl.ds(c0, kg), :], qabs_ref[...])
        s_row = c0 + lax.broadcasted_iota(I32, (kg, 1), 0)
        sel = (keys_ref[pl.ds(c0, kg), :] >= thr) & (s_row <= t_lane)
        addm = jnp.where(sel, 0.0, NEG_BIG)
        ps, ms, ls, corrs = [], [], [], []
        for h in range(nh):
            lanes = slice(h * tq, (h + 1) * tq)
            s = s_all[:, lanes] + addm
            if bias_di is not None:
                s = s + bias_ref[bias_di, :, lanes]
            m_prev = m[:, lanes]
            m_new = jnp.maximum(m_prev, jnp.max(s, axis=0, keepdims=True))
            p = jnp.exp(s - m_new)
            corr = jnp.exp(m_prev - m_new)
            ls.append(corr * l[:, lanes] + jnp.sum(p, axis=0, keepdims=True))
            ms.append(m_new)
            corrs.append(corr)
            ps.append(p.astype(BF16))
        pt = jnp.concatenate(ps, axis=1)
        corr_all = jnp.concatenate(corrs, axis=1)
        acc_ref[...] = acc_ref[...] * corr_all + _dot(ckvt_ref[0, :, pl.ds(c0, kg)], pt)
        return jnp.concatenate(ms, axis=1), jnp.concatenate(ls, axis=1)

    def at_group(g):
        return pl.multiple_of(g * KEY_GROUP, KEY_GROUP)

    carry = (jnp.full((1, nh * tq), NEG_BIG, F32), jnp.zeros((1, nh * tq), F32))
    n_far = jnp.maximum(j // 2 - 1 + j % 2, 0)
    carry = lax.fori_loop(0, n_far, lambda g, cr: attend(at_group(g), cr, None), carry)
    carry = lax.cond((j % 2 == 0) & (j >= 2), lambda cr: attend(at_group(j // 2 - 1), cr, 2), lambda cr: cr, carry)
    _, l = attend(at_group(j // 2), carry, j % 2)

    o_lat = (acc_ref[...] / l).astype(BF16)
    for h in range(nh):
        o_ref[:, h * HEAD_DIM:(h + 1) * HEAD_DIM] = _dot_tn(
            o_lat[:, h * tq:(h + 1) * tq], wuv_ref[h]).astype(o_ref.dtype)


def dsa_attention(proj, ckv, ckvt, k3, qi3, wt, wuk, wuv, bias, *, batch, seq, topk):
    tq = Q_BLOCK
    nq = seq // tq
    nh = N_ATT_HEADS
    r = KV_LORA_RANK
    seq_pad = -(-seq // SCORE_GROUP) * SCORE_GROUP
    return pl.pallas_call(
        functools.partial(_dsa_kernel, topk=topk),
        out_shape=jax.ShapeDtypeStruct((batch * seq, ATT_WIDTH), BF16),
        grid=(batch, nq),
        in_specs=[
            pl.BlockSpec((tq, ATT_WIDTH), lambda b, j: (b * nq + j, 0)),
            _single((seq, r), lambda b, j: (b, 0)),
            _single((1, r, seq), lambda b, j: (b, 0, 0)),
            _single((seq, IDX_K), lambda b, j: (b, 0)),
            pl.BlockSpec((N_IDX_HEADS, tq, IDX_K), lambda b, j: (b * nq + j, 0, 0)),
            pl.BlockSpec((N_IDX_HEADS, tq), lambda b, j: (b * nq + j, 0)),
            _single((nh, r, HEAD_DIM), lambda b, j: (0, 0, 0)),
            _single((nh, r, HEAD_DIM), lambda b, j: (0, 0, 0)),
            _single((N_BIAS_TABLES, KEY_GROUP, nh * tq), lambda b, j: (0, 0, 0)),
        ],
        out_specs=pl.BlockSpec((tq, ATT_WIDTH), lambda b, j: (b * nq + j, 0)),
        scratch_shapes=[
            pltpu.VMEM((seq_pad, tq), I32),
            pltpu.VMEM((r, nh * tq), BF16),
            pltpu.VMEM((r, nh * tq), F32),
        ],
        compiler_params=_cparams(("parallel", "arbitrary")),
        name="dsa",
    )(proj, ckv, ckvt, k3, qi3, wt, wuk, wuv, bias)


def _cumsum_rows(x):
    n = x.shape[0]
    row = lax.broadcasted_iota(I32, (n, 1), 0)
    sh = 1
    while sh < n:
        x = x + jnp.where(row >= sh, pltpu.roll(x, sh, 0), 0.0)
        sh *= 2
    return x


def _hgrn2_kernel(q_ref, f_ref, i_ref, g_ref, lb_ref, ng_ref, o_ref, st_ref, *, layer):
    c = REC_CHUNK
    dk = HEAD_DIM

    @pl.when(pl.program_id(2) == 0)
    def _():
        st_ref[...] = jnp.zeros_like(st_ref)

    lbr = lb_ref[...]
    e = jnp.exp(lbr - jnp.max(lbr, axis=0, keepdims=True))
    pr = e / jnp.sum(e, axis=0, keepdims=True)
    lb_all = jnp.zeros((1, lbr.shape[1]), F32)
    for i in range(1, layer + 1):
        lb_all = lb_all + pr[i:i + 1, :]

    row = lax.broadcasted_iota(I32, (c, 1), 0)
    col = lax.broadcasted_iota(I32, (1, c), 1)
    sub = 8
    nsub = c // sub
    tl3 = lax.broadcasted_iota(I32, (1, sub, 1), 1)
    g3 = lax.broadcasted_iota(I32, (nsub, 1, 1), 0)
    col3 = lax.broadcasted_iota(I32, (1, 1, c), 2)
    levels = []
    m = c // 2
    while m >= sub:
        levels.append(m)
        m //= 2

    def head_chunk(r0, hh):
        lanes = slice(hh * dk, (hh + 1) * dk)
        lb = lb_all[:, lanes]
        q = q_ref[pl.ds(r0, c), lanes]
        qf = q * jax.nn.sigmoid(q)
        f = lb + (1.0 - lb) * jax.nn.sigmoid(f_ref[pl.ds(r0, c), lanes])
        k = 1.0 - f
        v = i_ref[pl.ds(r0, c), lanes].astype(BF16)
        cum = _cumsum_rows(jnp.log(f))

        a = jnp.zeros((c, c), F32)
        for m in levels:
            first = (row & (2 * m - 1)) < m
            bnd = jnp.concatenate(
                [jnp.broadcast_to(cum[b * 2 * m + m - 1:b * 2 * m + m, :], (2 * m, dk))
                 for b in range(c // (2 * m))], axis=0)
            qs = jnp.where(first, 0.0, qf * jnp.exp(cum - bnd)).astype(BF16)
            ks = jnp.where(first, k * jnp.exp(bnd - cum), 0.0).astype(BF16)
            same = (row & -(2 * m)) == (col & -(2 * m))
            a = a + jnp.where(same, _dot_nt(qs, ks), 0.0)
        cum3 = cum.reshape(nsub, sub, dk)
        k3 = k.reshape(nsub, sub, dk)
        qf3 = qf.reshape(nsub, sub, dk)
        a3 = jnp.zeros((nsub, sub, c), F32)
        for s in range(sub):
            d = jnp.where(tl3 >= s, cum3 - cum3[:, s:s + 1, :], -jnp.inf)
            prod = qf3 * k3[:, s:s + 1, :] * jnp.exp(d)
            colsum = jnp.sum(prod, axis=-1, keepdims=True)
            a3 = a3 + jnp.where(col3 == g3 * sub + s, colsum, 0.0)
        a = a + a3.reshape(c, c)

        st = st_ref[hh]
        o = _dot(a.astype(BF16), v) + _dot_nt((qf * jnp.exp(cum)).astype(BF16), st.astype(BF16))
        last = cum[c - 1:c, :]
        kl = (k * jnp.exp(last - cum)).astype(BF16)
        st_ref[hh] = st * jnp.exp(last) + _dot_tn(v, kl)

        ms = jnp.mean(o * o, axis=-1, keepdims=True)
        gt = g_ref[pl.ds(r0, c), lanes]
        on = o * lax.rsqrt(ms + LN_EPS) * ng_ref[:, lanes]
        o_ref[pl.ds(r0, c), lanes] = (on * (gt * jax.nn.sigmoid(gt))).astype(o_ref.dtype)

    def chunk_body(ci, carry):
        r0 = pl.multiple_of(ci * c, c)
        for hh in range(REC_HEADS_PER_STEP):
            head_chunk(r0, hh)
        return carry

    lax.fori_loop(0, q_ref.shape[0] // c, chunk_body, 0)


def hgrn2(proj, rec_lb, norm_g, *, layer, batch, seq):
    tb = min(REC_TBLOCK, seq)
    nt = seq // tb
    hw = REC_HEADS_PER_STEP * HEAD_DIM
    ng = N_REC_HEADS // REC_HEADS_PER_STEP

    def seg(k):
        return pl.BlockSpec((tb, hw), lambda b, h, t, k=k: (b * nt + t, k * ng + h))

    return pl.pallas_call(
        functools.partial(_hgrn2_kernel, layer=layer),
        out_shape=jax.ShapeDtypeStruct((batch * seq, REC_WIDTH), BF16),
        grid=(batch, ng, nt),
        in_specs=[
            seg(0), seg(1), seg(2), seg(3),
            pl.BlockSpec((rec_lb.shape[0], hw), lambda b, h, t: (0, h)),
            pl.BlockSpec((1, hw), lambda b, h, t: (0, h)),
        ],
        out_specs=pl.BlockSpec((tb, hw), lambda b, h, t: (b * nt + t, h)),
        scratch_shapes=[pltpu.VMEM((REC_HEADS_PER_STEP, HEAD_DIM, HEAD_DIM), F32)],
        compiler_params=_cparams(("parallel", "parallel", "arbitrary")),
        name="hgrn2",
    )(proj, proj, proj, proj, rec_lb, norm_g)


def _pack_w_in(w):
    kv0 = ATT_WIDTH
    qi0 = kv0 + KV_LORA_RANK
    sm0 = qi0 + IDX_WIDTH
    sm1 = sm0 + IDX_DIM + N_IDX_HEADS
    pad = jnp.zeros((w.shape[0], SMALL_WIDTH - IDX_DIM - N_IDX_HEADS), BF16)
    parts = [w[:, :kv0], w[:, qi0:sm0], w[:, kv0:qi0], w[:, sm0:sm1]]
    w_att = jnp.concatenate([p.astype(BF16) for p in parts] + [pad], axis=1)
    return w_att, w[:, sm1:].astype(BF16)


def _row_tile(m, want):
    t = min(want, m)
    while m % t:
        t //= 2
    return t


def kernel(x, mem, w_in, w_uk, w_uv, kv_norm_g, idx_kn_g, idx_kn_b, rec_lb, rec_norm_g, w_out, rel_bias,
           wq_c, wk_c, wv_c, wo_c, ffn_gate, ffn_up, ffn_down, ln_g, ln_b):
    batch, seq, d = x.shape
    depth = w_in.shape[0]
    m = batch * seq
    alpha = (2 * depth) ** 0.25
    topk = min(TOPK_MAX, seq // 4)
    d_ff = ffn_gate.shape[-1]
    tf = 256 if d_ff % 256 == 0 else 128

    xf = x.reshape(m, d)
    mem_f = mem.reshape(batch * mem.shape[1], d)
    bias = bias_tables(rel_bias)
    rec_lb = rec_lb.astype(F32)

    ffn_order = [(l, i) for l in range(depth) for i in range(2)]
    ffn_w = {ffn_order[0]: tuple(w[0, 0].astype(BF16) for w in (ffn_gate, ffn_up, ffn_down))}

    def ffn(xf, l, i):
        pos = ffn_order.index((l, i))
        nxt = ffn_order[pos + 1] if pos + 1 < len(ffn_order) else None
        out, cast = ffn_ln(xf, *ffn_w.pop((l, i)), ln_g[l, 3 * i:3 * i + 1], ln_b[l, 3 * i:3 * i + 1],
                           alpha=alpha, tm=_row_tile(m, 512), tf=tf,
                           cast_next=None if nxt is None else (ffn_gate, ffn_up, ffn_down) + nxt)
        if nxt is not None:
            ffn_w[nxt] = cast
        return out

    for l in range(depth):
        xf = ffn(xf, l, 0)

        w_att, w_rec = _pack_w_in(w_in[l])
        proj_a = matmul(xf, w_att, tm=_row_tile(m, 512), tn=768, out_dtype=F32)
        proj_r = matmul(xf, w_rec, tm=_row_tile(m, 512), tn=1024, out_dtype=F32)
        ckv, ckvt, k3, qi3, wt = dsa_prep(proj_a, kv_norm_g[l][None], idx_kn_g[l][None], idx_kn_b[l][None], seq=seq)
        o_att = dsa_attention(proj_a, ckv, ckvt, k3, qi3, wt, w_uk[l].astype(BF16), w_uv[l].astype(BF16), bias,
                              batch=batch, seq=seq, topk=topk)
        o_rec = hgrn2(proj_r, rec_lb, rec_norm_g[l][None], layer=l, batch=batch, seq=seq)
        xf = proj_ln(o_att, o_rec, w_out[l].astype(BF16), xf, ln_g[l, 1:2], ln_b[l, 1:2],
                     alpha=alpha, tm=_row_tile(m, 512), tn=512)

        kv = matmul(mem_f, jnp.concatenate([wk_c[l], wv_c[l]], axis=1).astype(BF16),
                    tm=_row_tile(mem_f.shape[0], 512), tn=256, out_dtype=BF16)
        kv = kv.reshape(batch, mem.shape[1], kv.shape[-1])
        xf = cross_ln(xf, wq_c[l].astype(BF16), kv, wo_c[l].astype(BF16), ln_g[l, 2:3], ln_b[l, 2:3],
                      alpha=alpha, tm=_row_tile(seq, 256), rows_per_batch=seq)

        xf = ffn(xf, l, 1)
    return xf.reshape(batch, seq, d)
```

```python
import functools
import math

import jax
import jax.numpy as jnp
import numpy as np
from jax import lax
from jax.experimental import pallas as pl
from jax.experimental.pallas import tpu as pltpu

F32 = jnp.float32
BF16 = jnp.bfloat16
I32 = jnp.int32

LANE = 128
VMEM_LIMIT_V7X = 60 * 1024 * 1024

HEAD_DIM = 128
N_ATT_HEADS = 16
N_REC_HEADS = 16
KV_LORA_RANK = 512
N_IDX_HEADS = 16
IDX_DIM = 64
TOPK_MAX = 256
Q_BLOCK = 128
N_BUCKETS = 32
MAX_DISTANCE = 128
N_CROSS_HEADS = 4
LN_EPS = 1e-5
ATT_SCALE = HEAD_DIM ** -0.5
IDX_SCALE = IDX_DIM ** -0.5
IDX_HEAD_SCALE = N_IDX_HEADS ** -0.5

ATT_WIDTH = N_ATT_HEADS * HEAD_DIM
IDX_WIDTH = N_IDX_HEADS * IDX_DIM
REC_WIDTH = N_REC_HEADS * HEAD_DIM
COL_QIDX = ATT_WIDTH
COL_KV = COL_QIDX + IDX_WIDTH
COL_SMALL = COL_KV + KV_LORA_RANK
SMALL_WIDTH = 256
COL_REC = COL_SMALL + SMALL_WIDTH

NEG_BIG = -1e30
INT_MIN = -2 ** 31
KEY_NEG_INF = (0xFF800000 - 2 ** 32) ^ 0x7FFFFFFF

REC_CHUNK = 64
REC_TBLOCK = 512
REC_HEADS_PER_STEP = 16
KEY_GROUP = 256
N_BIAS_TABLES = 3
SCORE_GROUP = 512
IDX_K = 256
COUNT_ROWS = 64
LN_ROWS = 128
PREP_QBLOCKS = 2


def _cparams(sem):
    return pltpu.CompilerParams(dimension_semantics=sem, vmem_limit_bytes=VMEM_LIMIT_V7X)


def _single(shape, imap):
    return pl.BlockSpec(shape, imap, pipeline_mode=pl.Buffered(1))


def _layer_norm_rows(y, g, b):
    mu = jnp.mean(y, axis=-1, keepdims=True)
    d = y - mu
    var = jnp.mean(d * d, axis=-1, keepdims=True)
    return d * lax.rsqrt(var + LN_EPS) * g + b


def _residual_ln_inplace(x_ref, o_ref, g_ref, b_ref, *, alpha, scale):
    g = g_ref[...]
    b = b_ref[...]

    def body(i, carry):
        rows = pl.ds(pl.multiple_of(i * LN_ROWS, LN_ROWS), LN_ROWS)
        y = alpha * x_ref[rows, :] + scale * o_ref[rows, :]
        o_ref[rows, :] = _layer_norm_rows(y, g, b)
        return carry

    lax.fori_loop(0, o_ref.shape[0] // LN_ROWS, body, 0)


def _dot(a, b):
    return jnp.dot(a, b, preferred_element_type=F32)


def _dot_nt(a, b):
    return lax.dot_general(a, b, (((1,), (1,)), ((), ())), preferred_element_type=F32)


def _dot_tn(a, b):
    return lax.dot_general(a, b, (((0,), (0,)), ((), ())), preferred_element_type=F32)


def _ffn_ln_kernel(x_ref, wg_ref, wu_ref, wd_ref, g_ref, b_ref, *rest, alpha, n_cast):
    cast_in = rest[:n_cast]
    o_ref = rest[n_cast]
    cast_out = rest[n_cast + 1:2 * n_cast + 1]
    xb_ref = rest[2 * n_cast + 1]
    f = pl.program_id(1)

    @pl.when(f == 0)
    def _():
        xb_ref[...] = x_ref[...].astype(BF16)
        o_ref[...] = jnp.zeros_like(o_ref)

    xb = xb_ref[...]
    h = _dot(xb, wg_ref[...])
    u = _dot(xb, wu_ref[...])
    a = (h * jax.nn.sigmoid(h) * u).astype(BF16)
    o_ref[...] += _dot(a, wd_ref[...])

    for src, dst in zip(cast_in, cast_out):
        dst[...] = src[0, 0].astype(BF16)

    @pl.when(f == pl.num_programs(1) - 1)
    def _():
        _residual_ln_inplace(x_ref, o_ref, g_ref, b_ref, alpha=alpha, scale=0.5)


def ffn_ln(x, wg, wu, wd, g, b, *, alpha, tm, tf, cast_next=None):
    m, d = x.shape
    d_ff = wg.shape[1]
    ni, nf = m // tm, d_ff // tf
    in_specs = [
        pl.BlockSpec((tm, d), lambda i, f: (i, 0)),
        pl.BlockSpec((d, tf), lambda i, f: (0, f)),
        pl.BlockSpec((d, tf), lambda i, f: (0, f)),
        pl.BlockSpec((tf, d), lambda i, f: (f, 0)),
        pl.BlockSpec((1, d), lambda i, f: (0, 0)),
        pl.BlockSpec((1, d), lambda i, f: (0, 0)),
    ]
    out_shape = [jax.ShapeDtypeStruct((m, d), F32)]
    out_specs = [pl.BlockSpec((tm, d), lambda i, f: (i, 0))]
    operands = [x, wg, wu, wd, g, b]
    n_cast = 0
    if cast_next is not None:
        ng, nu, nd, nl, nw = cast_next
        td = d // ni
        n_cast = 3
        in_specs += [
            pl.BlockSpec((1, 1, td, tf), lambda i, f: (nl, nw, i, f)),
            pl.BlockSpec((1, 1, td, tf), lambda i, f: (nl, nw, i, f)),
            pl.BlockSpec((1, 1, tf, td), lambda i, f: (nl, nw, f, i)),
        ]
        out_shape += [jax.ShapeDtypeStruct((d, d_ff), BF16), jax.ShapeDtypeStruct((d, d_ff), BF16),
                      jax.ShapeDtypeStruct((d_ff, d), BF16)]
        out_specs += [
            pl.BlockSpec((td, tf), lambda i, f: (i, f)),
            pl.BlockSpec((td, tf), lambda i, f: (i, f)),
            pl.BlockSpec((tf, td), lambda i, f: (f, i)),
        ]
        operands += [ng, nu, nd]
    outs = pl.pallas_call(
        functools.partial(_ffn_ln_kernel, alpha=alpha, n_cast=n_cast),
        out_shape=tuple(out_shape),
        grid=(ni, nf),
        in_specs=in_specs,
        out_specs=tuple(out_specs),
        scratch_shapes=[pltpu.VMEM((tm, d), BF16)],
        compiler_params=_cparams(("parallel", "arbitrary")),
        name="ffn_ln",
    )(*operands)
    return outs[0], tuple(outs[1:])


def _matmul_kernel(x_ref, w_ref, o_ref, xb_ref):
    @pl.when(pl.program_id(1) == 0)
    def _():
        xb_ref[...] = x_ref[...].astype(BF16)

    o_ref[...] = _dot(xb_ref[...], w_ref[...]).astype(o_ref.dtype)


def matmul(x, w, *, tm, tn, out_dtype):
    m, k = x.shape
    n = w.shape[1]
    return pl.pallas_call(
        _matmul_kernel,
        out_shape=jax.ShapeDtypeStruct((m, n), out_dtype),
        grid=(m // tm, n // tn),
        in_specs=[
            pl.BlockSpec((tm, k), lambda i, j: (i, 0)),
            pl.BlockSpec((k, tn), lambda i, j: (0, j)),
        ],
        out_specs=pl.BlockSpec((tm, tn), lambda i, j: (i, j)),
        scratch_shapes=[pltpu.VMEM((tm, k), BF16)],
        compiler_params=_cparams(("parallel", "arbitrary")),
        name="matmul",
    )(x, w)


def _proj_ln_kernel(a1_ref, a2_ref, w1_ref, w2_ref, x_ref, g_ref, b_ref, o_ref, *, alpha, tn):
    n = pl.program_id(1)
    c0 = pl.multiple_of(n * tn, tn)
    o_ref[:, pl.ds(c0, tn)] = _dot(a1_ref[...], w1_ref[...]) + _dot(a2_ref[...], w2_ref[...])

    @pl.when(n == pl.num_programs(1) - 1)
    def _():
        _residual_ln_inplace(x_ref, o_ref, g_ref, b_ref, alpha=alpha, scale=1.0)


def proj_ln(a1, a2, w, x, g, b, *, alpha, tm, tn):
    m, kh = a1.shape
    d = w.shape[1]
    return pl.pallas_call(
        functools.partial(_proj_ln_kernel, alpha=alpha, tn=tn),
        out_shape=jax.ShapeDtypeStruct((m, d), F32),
        grid=(m // tm, d // tn),
        in_specs=[
            pl.BlockSpec((tm, kh), lambda i, n: (i, 0)),
            pl.BlockSpec((tm, kh), lambda i, n: (i, 0)),
            pl.BlockSpec((kh, tn), lambda i, n: (0, n)),
            pl.BlockSpec((kh, tn), lambda i, n: (1, n)),
            pl.BlockSpec((tm, d), lambda i, n: (i, 0)),
            pl.BlockSpec((1, d), lambda i, n: (0, 0)),
            pl.BlockSpec((1, d), lambda i, n: (0, 0)),
        ],
        out_specs=pl.BlockSpec((tm, d), lambda i, n: (i, 0)),
        compiler_params=_cparams(("parallel", "arbitrary")),
        name="proj_ln",
    )(a1, a2, w, w, x, g, b)


def _cross_ln_kernel(x_ref, wq_ref, kv_ref, wo_ref, g_ref, b_ref, o_ref, *, alpha, n_heads):
    x = x_ref[...]
    q = _dot(x.astype(BF16), wq_ref[...])
    kv = kv_ref[0]
    cw = n_heads * HEAD_DIM
    outs = []
    for h in range(n_heads):
        qh = q[:, h * HEAD_DIM:(h + 1) * HEAD_DIM].astype(BF16)
        kh = kv[:, h * HEAD_DIM:(h + 1) * HEAD_DIM]
        vh = kv[:, cw + h * HEAD_DIM:cw + (h + 1) * HEAD_DIM]
        logits = _dot_nt(qh, kh) * ATT_SCALE
        mx = jnp.max(logits, axis=-1, keepdims=True)
        e = jnp.exp(logits - mx)
        p = e / jnp.sum(e, axis=-1, keepdims=True)
        outs.append(_dot(p.astype(BF16), vh))
    o = jnp.concatenate(outs, axis=-1).astype(BF16)
    o_ref[...] = _dot(o, wo_ref[...])
    _residual_ln_inplace(x_ref, o_ref, g_ref, b_ref, alpha=alpha, scale=1.0)


def cross_ln(x, wq, kv, wo, g, b, *, alpha, tm, rows_per_batch):
    m, d = x.shape
    cw = wq.shape[1]
    nmem = kv.shape[1]
    steps_per_batch = rows_per_batch // tm
    return pl.pallas_call(
        functools.partial(_cross_ln_kernel, alpha=alpha, n_heads=cw // HEAD_DIM),
        out_shape=jax.ShapeDtypeStruct((m, d), F32),
        grid=(m // tm,),
        in_specs=[
            pl.BlockSpec((tm, d), lambda i: (i, 0)),
            _single((d, cw), lambda i: (0, 0)),
            pl.BlockSpec((1, nmem, 2 * cw), lambda i: (i // steps_per_batch, 0, 0)),
            _single((cw, d), lambda i: (0, 0)),
            pl.BlockSpec((1, d), lambda i: (0, 0)),
            pl.BlockSpec((1, d), lambda i: (0, 0)),
        ],
        out_specs=pl.BlockSpec((tm, d), lambda i: (i, 0)),
        compiler_params=_cparams(("parallel",)),
        name="cross_ln",
    )(x, wq, kv, wo, g, b)


def _split3(x, order):
    hi = x.astype(BF16).astype(F32)
    lo = x - hi
    parts = [hi if c == "h" else lo for c in order]
    parts.append(jnp.zeros((x.shape[0], IDX_K - 3 * x.shape[1]), F32))
    return jnp.concatenate(parts, axis=-1).astype(BF16)


def _dsa_prep_kernel(kv_ref, qi_ref, sm_ref, kvg_ref, kng_ref, knb_ref,
                     ckv_ref, ckvt_ref, k3_ref, qi3_ref, wt_ref):
    kv = kv_ref[...]
    ms = jnp.mean(kv * kv, axis=-1, keepdims=True)
    ckv = kv * lax.rsqrt(ms + LN_EPS) * kvg_ref[...]
    ckv_ref[...] = ckv.astype(BF16)
    ckvt_ref[0] = jnp.transpose(ckv).astype(BF16)

    sm = sm_ref[...]
    kidx = _layer_norm_rows(sm[:, :IDX_DIM], kng_ref[...], knb_ref[...])
    k3_ref[...] = _split3(kidx, "hlh")

    qi = qi_ref[...]
    for qb in range(PREP_QBLOCKS):
        rows = slice(qb * Q_BLOCK, (qb + 1) * Q_BLOCK)
        wt = jnp.transpose(sm[rows])
        wt_ref[qb * N_IDX_HEADS:(qb + 1) * N_IDX_HEADS, :] = (
            wt[IDX_DIM:IDX_DIM + N_IDX_HEADS, :] * (IDX_HEAD_SCALE * IDX_SCALE))
        for h in range(N_IDX_HEADS):
            qi3_ref[qb * N_IDX_HEADS + h] = _split3(qi[rows, h * IDX_DIM:(h + 1) * IDX_DIM], "hhl")


def dsa_prep(proj, kv_g, kn_g, kn_b, *, seq):
    m = proj.shape[0]
    tq = PREP_QBLOCKS * Q_BLOCK
    nq = m // tq
    nqb = seq // tq
    return pl.pallas_call(
        _dsa_prep_kernel,
        out_shape=(
            jax.ShapeDtypeStruct((m, KV_LORA_RANK), BF16),
            jax.ShapeDtypeStruct((m // seq, KV_LORA_RANK, seq), BF16),
            jax.ShapeDtypeStruct((m, IDX_K), BF16),
            jax.ShapeDtypeStruct((m // Q_BLOCK * N_IDX_HEADS, Q_BLOCK, IDX_K), BF16),
            jax.ShapeDtypeStruct((m // Q_BLOCK * N_IDX_HEADS, Q_BLOCK), F32),
        ),
        grid=(nq,),
        in_specs=[
            pl.BlockSpec((tq, KV_LORA_RANK), lambda i: (i, COL_KV // KV_LORA_RANK)),
            pl.BlockSpec((tq, IDX_WIDTH), lambda i: (i, COL_QIDX // IDX_WIDTH)),
            pl.BlockSpec((tq, LANE), lambda i: (i, COL_SMALL // LANE)),
            pl.BlockSpec((1, KV_LORA_RANK), lambda i: (0, 0)),
            pl.BlockSpec((1, IDX_DIM), lambda i: (0, 0)),
            pl.BlockSpec((1, IDX_DIM), lambda i: (0, 0)),
        ],
        out_specs=(
            pl.BlockSpec((tq, KV_LORA_RANK), lambda i: (i, 0)),
            pl.BlockSpec((1, KV_LORA_RANK, tq), lambda i: (i // nqb, 0, i % nqb)),
            pl.BlockSpec((tq, IDX_K), lambda i: (i, 0)),
            pl.BlockSpec((PREP_QBLOCKS * N_IDX_HEADS, Q_BLOCK, IDX_K), lambda i: (i, 0, 0)),
            pl.BlockSpec((PREP_QBLOCKS * N_IDX_HEADS, Q_BLOCK), lambda i: (i, 0)),
        ),
        compiler_params=_cparams(("parallel",)),
        name="dsa_prep",
    )(proj, proj, proj, kv_g, kn_g, kn_b)


def _bucket_thresholds():
    max_exact = N_BUCKETS // 2
    rel = np.arange(max_exact, 4 * MAX_DISTANCE, dtype=np.float32)
    large = max_exact + (np.log(rel / np.float32(max_exact)) / np.float32(math.log(MAX_DISTANCE / max_exact))
                         * np.float32(N_BUCKETS - max_exact)).astype(np.int32)
    large = np.minimum(large, N_BUCKETS - 1)
    return [int(rel[np.argmax(large >= bkt)]) for bkt in range(max_exact + 1, N_BUCKETS)]


def _bias_table_kernel(rb_ref, o_ref):
    di = pl.program_id(0)
    max_exact = N_BUCKETS // 2
    sl = lax.broadcasted_iota(I32, (KEY_GROUP, Q_BLOCK), 0)
    tl = lax.broadcasted_iota(I32, (KEY_GROUP, Q_BLOCK), 1)
    rel = jnp.maximum(di * Q_BLOCK + tl - sl, 0)
    large = jnp.full(rel.shape, max_exact, I32)
    for th in _bucket_thresholds():
        large = large + (rel >= th).astype(I32)
    bucket = jnp.where(rel < max_exact, rel, large)
    for h in range(N_ATT_HEADS):
        far = rb_ref[N_BUCKETS - 1, h]
        val = jnp.zeros(rel.shape, F32)
        for bkt in range(N_BUCKETS - 1):
            val = jnp.where(bucket == bkt, rb_ref[bkt, h] - far, val)
        o_ref[0, :, h * Q_BLOCK:(h + 1) * Q_BLOCK] = val


def bias_tables(rel_bias):
    nh = rel_bias.shape[1]
    return pl.pallas_call(
        _bias_table_kernel,
        out_shape=jax.ShapeDtypeStruct((N_BIAS_TABLES, KEY_GROUP, nh * Q_BLOCK), F32),
        grid=(N_BIAS_TABLES,),
        in_specs=[pl.BlockSpec(memory_space=pltpu.SMEM)],
        out_specs=pl.BlockSpec((1, KEY_GROUP, nh * Q_BLOCK), lambda d: (d, 0, 0)),
        compiler_params=_cparams(("arbitrary",)),
        name="bias_tables",
    )(rel_bias)


def _dsa_kernel(q_ref, ckv_ref, ckvt_ref, k3_ref, qi3_ref, wt_ref, wuk_ref, wuv_ref, bias_ref, o_ref,
                keys_ref, qabs_ref, acc_ref, *, topk):
    j = pl.program_id(1)
    nh = N_ATT_HEADS
    tq = Q_BLOCK
    t_lane = j * tq + lax.broadcasted_iota(I32, (1, tq), 1)

    n_sgrp = (j + 1 + (SCORE_GROUP // tq - 1)) // (SCORE_GROUP // tq)
    qi3 = qi3_ref[...].reshape(N_IDX_HEADS * tq, IDX_K)

    def score_body(g, carry):
        r0 = pl.multiple_of(g * SCORE_GROUP, SCORE_GROUP)
        lt = _dot_nt(k3_ref[pl.ds(r0, SCORE_GROUP), :], qi3)
        sc = jnp.zeros((SCORE_GROUP, tq), F32)
        for h in range(N_IDX_HEADS):
            sc = sc + jnp.maximum(lt[:, h * tq:(h + 1) * tq], 0.0) * wt_ref[h:h + 1, :]
        s_row = r0 + lax.broadcasted_iota(I32, (SCORE_GROUP, 1), 0)
        sc = jnp.where(s_row <= t_lane, sc + 0.0, -jnp.inf)
        bits = pltpu.bitcast(sc, I32)
        keys_ref[pl.ds(r0, SCORE_GROUP), :] = jnp.where(bits < 0, bits ^ 0x7FFFFFFF, bits)
        return carry

    lax.fori_loop(0, n_sgrp, score_body, 0)

    def count_ge(cand):
        def body(g, part):
            r0 = pl.multiple_of(g * SCORE_GROUP, SCORE_GROUP)
            ind = jnp.where(keys_ref[pl.ds(r0, SCORE_GROUP), :] >= cand, 1.0, 0.0)
            return part + jnp.sum(ind.reshape(SCORE_GROUP // COUNT_ROWS, COUNT_ROWS, tq), axis=0)
        part = lax.fori_loop(0, n_sgrp, body, jnp.zeros((COUNT_ROWS, tq), F32))
        return jnp.sum(part, axis=0, keepdims=True)

    thr0 = jnp.where(count_ge(jnp.zeros((1, tq), I32)) >= topk, 0, INT_MIN).astype(I32)

    def bit_body(i, thr):
        cand = thr | jnp.left_shift(jnp.int32(1), 30 - i)
        return jnp.where(count_ge(cand) >= topk, cand, thr)

    thr = lax.fori_loop(0, 31, bit_body, thr0)

    n_ge = count_ge(thr)
    tie_lane = (n_ge > topk) & (thr > KEY_NEG_INF)
    has_tie = jnp.max(jnp.where(tie_lane, 1.0, 0.0)) > 0.0

    @pl.when(has_tie)
    def _():
        need = topk - count_ge(thr + 1)
        r_i = lax.broadcasted_iota(I32, (tq, tq), 0)
        c_i = lax.broadcasted_iota(I32, (tq, tq), 1)
        tri = jnp.where(c_i <= r_i, 1.0, 0.0).astype(BF16)

        def tie_body(kb, seen):
            r0 = pl.multiple_of(kb * tq, tq)
            blk = keys_ref[pl.ds(r0, tq), :]
            s_row = r0 + lax.broadcasted_iota(I32, (tq, 1), 0)
            eq = (blk == thr) & (s_row <= t_lane) & tie_lane
            rank = seen + _dot(tri, jnp.where(eq, 1.0, 0.0).astype(BF16))
            keys_ref[pl.ds(r0, tq), :] = jnp.where(eq & (rank > need), thr - 1, blk)
            return rank[tq - 1:tq, :]

        lax.fori_loop(0, j + 1, tie_body, jnp.zeros((1, tq), F32))

    for h in range(nh):
        qh = q_ref[:, h * HEAD_DIM:(h + 1) * HEAD_DIM].astype(BF16)
        qabs_ref[:, h * tq:(h + 1) * tq] = (_dot_nt(wuk_ref[h], qh) * ATT_SCALE).astype(BF16)

    acc_ref[...] = jnp.zeros(acc_ref.shape, F32)

    def attend(c0, carry, bias_di):
        kg = KEY_GROUP
        m, l = carry
        s_all = _dot(ckv_ref[pl.ds(c0, kg), :], qabs_ref[...])
        s_row = c0 + lax.broadcasted_iota(I32, (kg, 1), 0)
        sel = (keys_ref[pl.ds(c0, kg), :] >= thr) & (s_row <= t_lane)
        addm = jnp.where(sel, 0.0, NEG_BIG)
        ps, ms, ls, corrs = [], [], [], []
        for h in range(nh):
            lanes = slice(h * tq, (h + 1) * tq)
            s = s_all[:, lanes] + addm
            if bias_di is not None:
                s = s + bias_ref[bias_di, :, lanes]
            m_prev = m[:, lanes]
            m_new = jnp.maximum(m_prev, jnp.max(s, axis=0, keepdims=True))
            p = jnp.exp(s - m_new)
            corr = jnp.exp(m_prev - m_new)
            ls.append(corr * l[:, lanes] + jnp.sum(p, axis=0, keepdims=True))
            ms.append(m_new)
            corrs.append(corr)
            ps.append(p.astype(BF16))
        pt = jnp.concatenate(ps, axis=1)
        corr_all = jnp.concatenate(corrs, axis=1)
        acc_ref[...] = acc_ref[...] * corr_all + _dot(ckvt_ref[0, :, pl.ds(c0, kg)], pt)
        return jnp.concatenate(ms, axis=1), jnp.concatenate(ls, axis=1)

    def at_group(g):
        return pl.multiple_of(g * KEY_GROUP, KEY_GROUP)

    carry = (jnp.full((1, nh * tq), NEG_BIG, F32), jnp.zeros((1, nh * tq), F32))
    n_far = jnp.maximum(j // 2 - 1 + j % 2, 0)
    carry = lax.fori_loop(0, n_far, lambda g, cr: attend(at_group(g), cr, None), carry)
    carry = lax.cond((j % 2 == 0) & (j >= 2), lambda cr: attend(at_group(j // 2 - 1), cr, 2), lambda cr: cr, carry)
    _, l = attend(at_group(j // 2), carry, j % 2)

    o_lat = (acc_ref[...] / l).astype(BF16)
    for h in range(nh):
        o_ref[:, h * HEAD_DIM:(h + 1) * HEAD_DIM] = _dot_tn(
            o_lat[:, h * tq:(h + 1) * tq], wuv_ref[h]).astype(o_ref.dtype)


def dsa_attention(proj, ckv, ckvt, k3, qi3, wt, wuk, wuv, bias, *, batch, seq, topk):
    tq = Q_BLOCK
    nq = seq // tq
    nh = N_ATT_HEADS
    r = KV_LORA_RANK
    seq_pad = -(-seq // SCORE_GROUP) * SCORE_GROUP
    return pl.pallas_call(
        functools.partial(_dsa_kernel, topk=topk),
        out_shape=jax.ShapeDtypeStruct((batch * seq, ATT_WIDTH), BF16),
        grid=(batch, nq),
        in_specs=[
            pl.BlockSpec((tq, ATT_WIDTH), lambda b, j: (b * nq + j, 0)),
            _single((seq, r), lambda b, j: (b, 0)),
            _single((1, r, seq), lambda b, j: (b, 0, 0)),
            _single((seq, IDX_K), lambda b, j: (b, 0)),
            pl.BlockSpec((N_IDX_HEADS, tq, IDX_K), lambda b, j: (b * nq + j, 0, 0)),
            pl.BlockSpec((N_IDX_HEADS, tq), lambda b, j: (b * nq + j, 0)),
            _single((nh, r, HEAD_DIM), lambda b, j: (0, 0, 0)),
            _single((nh, r, HEAD_DIM), lambda b, j: (0, 0, 0)),
            _single((N_BIAS_TABLES, KEY_GROUP, nh * tq), lambda b, j: (0, 0, 0)),
        ],
        out_specs=pl.BlockSpec((tq, ATT_WIDTH), lambda b, j: (b * nq + j, 0)),
        scratch_shapes=[
            pltpu.VMEM((seq_pad, tq), I32),
            pltpu.VMEM((r, nh * tq), BF16),
            pltpu.VMEM((r, nh * tq), F32),
        ],
        compiler_params=_cparams(("parallel", "arbitrary")),
        name="dsa",
    )(proj, ckv, ckvt, k3, qi3, wt, wuk, wuv, bias)


def _cumsum_rows(x):
    n = x.shape[0]
    row = lax.broadcasted_iota(I32, (n, 1), 0)
    sh = 1
    while sh < n:
        x = x + jnp.where(row >= sh, pltpu.roll(x, sh, 0), 0.0)
        sh *= 2
    return x


def _hgrn2_kernel(q_ref, f_ref, i_ref, g_ref, lb_ref, ng_ref, o_ref, st_ref, *, layer):
    c = REC_CHUNK
    dk = HEAD_DIM

    @pl.when(pl.program_id(2) == 0)
    def _():
        st_ref[...] = jnp.zeros_like(st_ref)

    lbr = lb_ref[...]
    e = jnp.exp(lbr - jnp.max(lbr, axis=0, keepdims=True))
    pr = e / jnp.sum(e, axis=0, keepdims=True)
    lb_all = jnp.zeros((1, lbr.shape[1]), F32)
    for i in range(1, layer + 1):
        lb_all = lb_all + pr[i:i + 1, :]

    row = lax.broadcasted_iota(I32, (c, 1), 0)
    col = lax.broadcasted_iota(I32, (1, c), 1)
    sub = 8
    nsub = c // sub
    tl3 = lax.broadcasted_iota(I32, (1, sub, 1), 1)
    g3 = lax.broadcasted_iota(I32, (nsub, 1, 1), 0)
    col3 = lax.broadcasted_iota(I32, (1, 1, c), 2)
    levels = []
    m = c // 2
    while m >= sub:
        levels.append(m)
        m //= 2

    def head_chunk(r0, hh):
        lanes = slice(hh * dk, (hh + 1) * dk)
        lb = lb_all[:, lanes]
        q = q_ref[pl.ds(r0, c), lanes]
        qf = q * jax.nn.sigmoid(q)
        f = lb + (1.0 - lb) * jax.nn.sigmoid(f_ref[pl.ds(r0, c), lanes])
        k = 1.0 - f
        v = i_ref[pl.ds(r0, c), lanes].astype(BF16)
        cum = _cumsum_rows(jnp.log(f))

        a = jnp.zeros((c, c), F32)
        for m in levels:
            first = (row & (2 * m - 1)) < m
            bnd = jnp.concatenate(
                [jnp.broadcast_to(cum[b * 2 * m + m - 1:b * 2 * m + m, :], (2 * m, dk))
                 for b in range(c // (2 * m))], axis=0)
            qs = jnp.where(first, 0.0, qf * jnp.exp(cum - bnd)).astype(BF16)
            ks = jnp.where(first, k * jnp.exp(bnd - cum), 0.0).astype(BF16)
            same = (row & -(2 * m)) == (col & -(2 * m))
            a = a + jnp.where(same, _dot_nt(qs, ks), 0.0)
        cum3 = cum.reshape(nsub, sub, dk)
        k3 = k.reshape(nsub, sub, dk)
        qf3 = qf.reshape(nsub, sub, dk)
        a3 = jnp.zeros((nsub, sub, c), F32)
        for s in range(sub):
            d = jnp.where(tl3 >= s, cum3 - cum3[:, s:s + 1, :], -jnp.inf)
            prod = qf3 * k3[:, s:s + 1, :] * jnp.exp(d)
            colsum = jnp.sum(prod, axis=-1, keepdims=True)
            a3 = a3 + jnp.where(col3 == g3 * sub + s, colsum, 0.0)
        a = a + a3.reshape(c, c)

        st = st_ref[hh]
        o = _dot(a.astype(BF16), v) + _dot_nt((qf * jnp.exp(cum)).astype(BF16), st.astype(BF16))
        last = cum[c - 1:c, :]
        kl = (k * jnp.exp(last - cum)).astype(BF16)
        st_ref[hh] = st * jnp.exp(last) + _dot_tn(v, kl)

        ms = jnp.mean(o * o, axis=-1, keepdims=True)
        gt = g_ref[pl.ds(r0, c), lanes]
        on = o * lax.rsqrt(ms + LN_EPS) * ng_ref[:, lanes]
        o_ref[pl.ds(r0, c), lanes] = (on * (gt * jax.nn.sigmoid(gt))).astype(o_ref.dtype)

    def chunk_body(ci, carry):
        r0 = pl.multiple_of(ci * c, c)
        for hh in range(REC_HEADS_PER_STEP):
            head_chunk(r0, hh)
        return carry

    lax.fori_loop(0, q_ref.shape[0] // c, chunk_body, 0)


def hgrn2(proj, rec_lb, norm_g, *, layer, batch, seq):
    tb = min(REC_TBLOCK, seq)
    nt = seq // tb
    hw = REC_HEADS_PER_STEP * HEAD_DIM
    ng = N_REC_HEADS // REC_HEADS_PER_STEP

    def seg(k):
        return pl.BlockSpec((tb, hw), lambda b, h, t, k=k: (b * nt + t, k * ng + h))

    return pl.pallas_call(
        functools.partial(_hgrn2_kernel, layer=layer),
        out_shape=jax.ShapeDtypeStruct((batch * seq, REC_WIDTH), BF16),
        grid=(batch, ng, nt),
        in_specs=[
            seg(0), seg(1), seg(2), seg(3),
            pl.BlockSpec((rec_lb.shape[0], hw), lambda b, h, t: (0, h)),
            pl.BlockSpec((1, hw), lambda b, h, t: (0, h)),
        ],
        out_specs=pl.BlockSpec((tb, hw), lambda b, h, t: (b * nt + t, h)),
        scratch_shapes=[pltpu.VMEM((REC_HEADS_PER_STEP, HEAD_DIM, HEAD_DIM), F32)],
        compiler_params=_cparams(("parallel", "parallel", "arbitrary")),
        name="hgrn2",
    )(proj, proj, proj, proj, rec_lb, norm_g)


def _pack_w_in(w):
    kv0 = ATT_WIDTH
    qi0 = kv0 + KV_LORA_RANK
    sm0 = qi0 + IDX_WIDTH
    sm1 = sm0 + IDX_DIM + N_IDX_HEADS
    pad = jnp.zeros((w.shape[0], SMALL_WIDTH - IDX_DIM - N_IDX_HEADS), BF16)
    parts = [w[:, :kv0], w[:, qi0:sm0], w[:, kv0:qi0], w[:, sm0:sm1]]
    w_att = jnp.concatenate([p.astype(BF16) for p in parts] + [pad], axis=1)
    return w_att, w[:, sm1:].astype(BF16)


def _row_tile(m, want):
    t = min(want, m)
    while m % t:
        t //= 2
    return t


def kernel(x, mem, w_in, w_uk, w_uv, kv_norm_g, idx_kn_g, idx_kn_b, rec_lb, rec_norm_g, w_out, rel_bias,
           wq_c, wk_c, wv_c, wo_c, ffn_gate, ffn_up, ffn_down, ln_g, ln_b):
    batch, seq, d = x.shape
    depth = w_in.shape[0]
    m = batch * seq
    alpha = (2 * depth) ** 0.25
    topk = min(TOPK_MAX, seq // 4)
    d_ff = ffn_gate.shape[-1]
    tf = 256 if d_ff % 256 == 0 else 128

    xf = x.reshape(m, d)
    mem_f = mem.reshape(batch * mem.shape[1], d)
    bias = bias_tables(rel_bias)
    rec_lb = rec_lb.astype(F32)

    ffn_order = [(l, i) for l in range(depth) for i in range(2)]
    ffn_w = {ffn_order[0]: tuple(w[0, 0].astype(BF16) for w in (ffn_gate, ffn_up, ffn_down))}

    def ffn(xf, l, i):
        pos = ffn_order.index((l, i))
        nxt = ffn_order[pos + 1] if pos + 1 < len(ffn_order) else None
        out, cast = ffn_ln(xf, *ffn_w.pop((l, i)), ln_g[l, 3 * i:3 * i + 1], ln_b[l, 3 * i:3 * i + 1],
                           alpha=alpha, tm=_row_tile(m, 512), tf=tf,
                           cast_next=None if nxt is None else (ffn_gate, ffn_up, ffn_down) + nxt)
        if nxt is not None:
            ffn_w[nxt] = cast
        return out

    for l in range(depth):
        xf = ffn(xf, l, 0)

        w_att, w_rec = _pack_w_in(w_in[l])
        proj_a = matmul(xf, w_att, tm=_row_tile(m, 512), tn=768, out_dtype=F32)
        proj_r = matmul(xf, w_rec, tm=_row_tile(m, 512), tn=1024, out_dtype=F32)
        ckv, ckvt, k3, qi3, wt = dsa_prep(proj_a, kv_norm_g[l][None], idx_kn_g[l][None], idx_kn_b[l][None], seq=seq)
        o_att = dsa_attention(proj_a, ckv, ckvt, k3, qi3, wt, w_uk[l].astype(BF16), w_uv[l].astype(BF16), bias,
                              batch=batch, seq=seq, topk=topk)
        o_rec = hgrn2(proj_r, rec_lb, rec_norm_g[l][None], layer=l, batch=batch, seq=seq)
        xf = proj_ln(o_att, o_rec, w_out[l].astype(BF16), xf, ln_g[l, 1:2], ln_b[l, 1:2],
                     alpha=alpha, tm=_row_tile(m, 512), tn=512)

        kv = matmul(mem_f, jnp.concatenate([wk_c[l], wv_c[l]], axis=1).astype(BF16),
                    tm=_row_tile(mem_f.shape[0], 512), tn=256, out_dtype=BF16)
        kv = kv.reshape(batch, mem.shape[1], kv.shape[-1])
        xf = cross_ln(xf, wq_c[l].astype(BF16), kv, wo_c[l].astype(BF16), ln_g[l, 2:3], ln_b[l, 2:3],
                      alpha=alpha, tm=_row_tile(seq, 256), rows_per_batch=seq)

        xf = ffn(xf, l, 1)
    return xf.reshape(batch, seq, d)
```

```python
import functools
import math

import jax
import jax.numpy as jnp
import numpy as np
from jax import lax
from jax.experimental import pallas as pl
from jax.experimental.pallas import tpu as pltpu

F32 = jnp.float32
BF16 = jnp.bfloat16
I32 = jnp.int32

LANE = 128
VMEM_LIMIT_V7X = 60 * 1024 * 1024

HEAD_DIM = 128
N_ATT_HEADS = 16
N_REC_HEADS = 16
KV_LORA_RANK = 512
N_IDX_HEADS = 16
IDX_DIM = 64
TOPK_MAX = 256
Q_BLOCK = 128
N_BUCKETS = 32
MAX_DISTANCE = 128
N_CROSS_HEADS = 4
LN_EPS = 1e-5
ATT_SCALE = HEAD_DIM ** -0.5
IDX_SCALE = IDX_DIM ** -0.5
IDX_HEAD_SCALE = N_IDX_HEADS ** -0.5

ATT_WIDTH = N_ATT_HEADS * HEAD_DIM
IDX_WIDTH = N_IDX_HEADS * IDX_DIM
REC_WIDTH = N_REC_HEADS * HEAD_DIM
COL_QIDX = ATT_WIDTH
COL_KV = COL_QIDX + IDX_WIDTH
COL_SMALL = COL_KV + KV_LORA_RANK
SMALL_WIDTH = 256
COL_REC = COL_SMALL + SMALL_WIDTH

NEG_BIG = -1e30
INT_MIN = -2 ** 31
KEY_NEG_INF = (0xFF800000 - 2 ** 32) ^ 0x7FFFFFFF

REC_CHUNK = 64
REC_TBLOCK = 512
REC_HEADS_PER_STEP = 16
KEY_GROUP = 256
N_BIAS_TABLES = 3
SCORE_GROUP = 512
IDX_K = 256
COUNT_ROWS = 64
LN_ROWS = 128
PREP_QBLOCKS = 4


def _cparams(sem):
    return pltpu.CompilerParams(dimension_semantics=sem, vmem_limit_bytes=VMEM_LIMIT_V7X)


def _single(shape, imap):
    return pl.BlockSpec(shape, imap, pipeline_mode=pl.Buffered(1))


def _layer_norm_rows(y, g, b):
    mu = jnp.mean(y, axis=-1, keepdims=True)
    d = y - mu
    var = jnp.mean(d * d, axis=-1, keepdims=True)
    return d * lax.rsqrt(var + LN_EPS) * g + b


def _residual_ln_inplace(x_ref, o_ref, g_ref, b_ref, *, alpha, scale):
    g = g_ref[...]
    b = b_ref[...]

    def body(i, carry):
        rows = pl.ds(pl.multiple_of(i * LN_ROWS, LN_ROWS), LN_ROWS)
        y = alpha * x_ref[rows, :] + scale * o_ref[rows, :]
        o_ref[rows, :] = _layer_norm_rows(y, g, b)
        return carry

    lax.fori_loop(0, o_ref.shape[0] // LN_ROWS, body, 0)


def _dot(a, b):
    return jnp.dot(a, b, preferred_element_type=F32)


def _dot_nt(a, b):
    return lax.dot_general(a, b, (((1,), (1,)), ((), ())), preferred_element_type=F32)


def _dot_tn(a, b):
    return lax.dot_general(a, b, (((0,), (0,)), ((), ())), preferred_element_type=F32)


def _ffn_ln_kernel(x_ref, wg_ref, wu_ref, wd_ref, g_ref, b_ref, *rest, alpha, n_cast):
    cast_in = rest[:n_cast]
    o_ref = rest[n_cast]
    cast_out = rest[n_cast + 1:2 * n_cast + 1]
    xb_ref = rest[2 * n_cast + 1]
    f = pl.program_id(1)

    @pl.when(f == 0)
    def _():
        xb_ref[...] = x_ref[...].astype(BF16)
        o_ref[...] = jnp.zeros_like(o_ref)

    xb = xb_ref[...]
    h = _dot(xb, wg_ref[...])
    u = _dot(xb, wu_ref[...])
    a = (h * jax.nn.sigmoid(h) * u).astype(BF16)
    o_ref[...] += _dot(a, wd_ref[...])

    for src, dst in zip(cast_in, cast_out):
        dst[...] = src[0, 0].astype(BF16)

    @pl.when(f == pl.num_programs(1) - 1)
    def _():
        _residual_ln_inplace(x_ref, o_ref, g_ref, b_ref, alpha=alpha, scale=0.5)


def ffn_ln(x, wg, wu, wd, g, b, *, alpha, tm, tf, cast_next=None):
    m, d = x.shape
    d_ff = wg.shape[1]
    ni, nf = m // tm, d_ff // tf
    in_specs = [
        pl.BlockSpec((tm, d), lambda i, f: (i, 0)),
        pl.BlockSpec((d, tf), lambda i, f: (0, f)),
        pl.BlockSpec((d, tf), lambda i, f: (0, f)),
        pl.BlockSpec((tf, d), lambda i, f: (f, 0)),
        pl.BlockSpec((1, d), lambda i, f: (0, 0)),
        pl.BlockSpec((1, d), lambda i, f: (0, 0)),
    ]
    out_shape = [jax.ShapeDtypeStruct((m, d), F32)]
    out_specs = [pl.BlockSpec((tm, d), lambda i, f: (i, 0))]
    operands = [x, wg, wu, wd, g, b]
    n_cast = 0
    if cast_next is not None:
        ng, nu, nd, nl, nw = cast_next
        td = d // ni
        n_cast = 3
        in_specs += [
            pl.BlockSpec((1, 1, td, tf), lambda i, f: (nl, nw, i, f)),
            pl.BlockSpec((1, 1, td, tf), lambda i, f: (nl, nw, i, f)),
            pl.BlockSpec((1, 1, tf, td), lambda i, f: (nl, nw, f, i)),
        ]
        out_shape += [jax.ShapeDtypeStruct((d, d_ff), BF16), jax.ShapeDtypeStruct((d, d_ff), BF16),
                      jax.ShapeDtypeStruct((d_ff, d), BF16)]
        out_specs += [
            pl.BlockSpec((td, tf), lambda i, f: (i, f)),
            pl.BlockSpec((td, tf), lambda i, f: (i, f)),
            pl.BlockSpec((tf, td), lambda i, f: (f, i)),
        ]
        operands += [ng, nu, nd]
    outs = pl.pallas_call(
        functools.partial(_ffn_ln_kernel, alpha=alpha, n_cast=n_cast),
        out_shape=tuple(out_shape),
        grid=(ni, nf),
        in_specs=in_specs,
        out_specs=tuple(out_specs),
        scratch_shapes=[pltpu.VMEM((tm, d), BF16)],
        compiler_params=_cparams(("parallel", "arbitrary")),
        name="ffn_ln",
    )(*operands)
    return outs[0], tuple(outs[1:])


def _matmul_kernel(x_ref, w_ref, o_ref, xb_ref):
    @pl.when(pl.program_id(1) == 0)
    def _():
        xb_ref[...] = x_ref[...].astype(BF16)

    o_ref[...] = _dot(xb_ref[...], w_ref[...]).astype(o_ref.dtype)


def matmul(x, w, *, tm, tn, out_dtype):
    m, k = x.shape
    n = w.shape[1]
    return pl.pallas_call(
        _matmul_kernel,
        out_shape=jax.ShapeDtypeStruct((m, n), out_dtype),
        grid=(m // tm, n // tn),
        in_specs=[
            pl.BlockSpec((tm, k), lambda i, j: (i, 0)),
            pl.BlockSpec((k, tn), lambda i, j: (0, j)),
        ],
        out_specs=pl.BlockSpec((tm, tn), lambda i, j: (i, j)),
        scratch_shapes=[pltpu.VMEM((tm, k), BF16)],
        compiler_params=_cparams(("parallel", "arbitrary")),
        name="matmul",
    )(x, w)


def _proj_ln_kernel(a1_ref, a2_ref, w1_ref, w2_ref, x_ref, g_ref, b_ref, o_ref, *, alpha, tn):
    n = pl.program_id(1)
    c0 = pl.multiple_of(n * tn, tn)
    o_ref[:, pl.ds(c0, tn)] = _dot(a1_ref[...], w1_ref[...]) + _dot(a2_ref[...], w2_ref[...])

    @pl.when(n == pl.num_programs(1) - 1)
    def _():
        _residual_ln_inplace(x_ref, o_ref, g_ref, b_ref, alpha=alpha, scale=1.0)


def proj_ln(a1, a2, w, x, g, b, *, alpha, tm, tn):
    m, kh = a1.shape
    d = w.shape[1]
    return pl.pallas_call(
        functools.partial(_proj_ln_kernel, alpha=alpha, tn=tn),
        out_shape=jax.ShapeDtypeStruct((m, d), F32),
        grid=(m // tm, d // tn),
        in_specs=[
            pl.BlockSpec((tm, kh), lambda i, n: (i, 0)),
            pl.BlockSpec((tm, kh), lambda i, n: (i, 0)),
            pl.BlockSpec((kh, tn), lambda i, n: (0, n)),
            pl.BlockSpec((kh, tn), lambda i, n: (1, n)),
            pl.BlockSpec((tm, d), lambda i, n: (i, 0)),
            pl.BlockSpec((1, d), lambda i, n: (0, 0)),
            pl.BlockSpec((1, d), lambda i, n: (0, 0)),
        ],
        out_specs=pl.BlockSpec((tm, d), lambda i, n: (i, 0)),
        compiler_params=_cparams(("parallel", "arbitrary")),
        name="proj_ln",
    )(a1, a2, w, w, x, g, b)


def _cross_ln_kernel(x_ref, wq_ref, kv_ref, wo_ref, g_ref, b_ref, o_ref, *, alpha, n_heads):
    x = x_ref[...]
    q = _dot(x.astype(BF16), wq_ref[...])
    kv = kv_ref[0]
    cw = n_heads * HEAD_DIM
    outs = []
    for h in range(n_heads):
        qh = q[:, h * HEAD_DIM:(h + 1) * HEAD_DIM].astype(BF16)
        kh = kv[:, h * HEAD_DIM:(h + 1) * HEAD_DIM]
        vh = kv[:, cw + h * HEAD_DIM:cw + (h + 1) * HEAD_DIM]
        logits = _dot_nt(qh, kh) * ATT_SCALE
        mx = jnp.max(logits, axis=-1, keepdims=True)
        e = jnp.exp(logits - mx)
        p = e / jnp.sum(e, axis=-1, keepdims=True)
        outs.append(_dot(p.astype(BF16), vh))
    o = jnp.concatenate(outs, axis=-1).astype(BF16)
    o_ref[...] = _dot(o, wo_ref[...])
    _residual_ln_inplace(x_ref, o_ref, g_ref, b_ref, alpha=alpha, scale=1.0)


def cross_ln(x, wq, kv, wo, g, b, *, alpha, tm, rows_per_batch):
    m, d = x.shape
    cw = wq.shape[1]
    nmem = kv.shape[1]
    steps_per_batch = rows_per_batch // tm
    return pl.pallas_call(
        functools.partial(_cross_ln_kernel, alpha=alpha, n_heads=cw // HEAD_DIM),
        out_shape=jax.ShapeDtypeStruct((m, d), F32),
        grid=(m // tm,),
        in_specs=[
            pl.BlockSpec((tm, d), lambda i: (i, 0)),
            _single((d, cw), lambda i: (0, 0)),
            pl.BlockSpec((1, nmem, 2 * cw), lambda i: (i // steps_per_batch, 0, 0)),
            _single((cw, d), lambda i: (0, 0)),
            pl.BlockSpec((1, d), lambda i: (0, 0)),
            pl.BlockSpec((1, d), lambda i: (0, 0)),
        ],
        out_specs=pl.BlockSpec((tm, d), lambda i: (i, 0)),
        compiler_params=_cparams(("parallel",)),
        name="cross_ln",
    )(x, wq, kv, wo, g, b)


def _split3(x, order):
    hi = x.astype(BF16).astype(F32)
    lo = x - hi
    parts = [hi if c == "h" else lo for c in order]
    parts.append(jnp.zeros((x.shape[0], IDX_K - 3 * x.shape[1]), F32))
    return jnp.concatenate(parts, axis=-1).astype(BF16)


def _dsa_prep_kernel(kv_ref, qi_ref, sm_ref, kvg_ref, kng_ref, knb_ref,
                     ckv_ref, ckvt_ref, k3_ref, qi3_ref, wt_ref):
    kv = kv_ref[...]
    ms = jnp.mean(kv * kv, axis=-1, keepdims=True)
    ckv = kv * lax.rsqrt(ms + LN_EPS) * kvg_ref[...]
    ckv_ref[...] = ckv.astype(BF16)
    ckvt_ref[0] = jnp.transpose(ckv).astype(BF16)

    sm = sm_ref[...]
    kidx = _layer_norm_rows(sm[:, :IDX_DIM], kng_ref[...], knb_ref[...])
    k3_ref[...] = _split3(kidx, "hlh")

    qi = qi_ref[...]
    for qb in range(PREP_QBLOCKS):
        rows = slice(qb * Q_BLOCK, (qb + 1) * Q_BLOCK)
        wt = jnp.transpose(sm[rows])
        wt_ref[qb * N_IDX_HEADS:(qb + 1) * N_IDX_HEADS, :] = (
            wt[IDX_DIM:IDX_DIM + N_IDX_HEADS, :] * (IDX_HEAD_SCALE * IDX_SCALE))
        for h in range(N_IDX_HEADS):
            qi3_ref[qb * N_IDX_HEADS + h] = _split3(qi[rows, h * IDX_DIM:(h + 1) * IDX_DIM], "hhl")


def dsa_prep(proj, kv_g, kn_g, kn_b, *, seq):
    m = proj.shape[0]
    tq = PREP_QBLOCKS * Q_BLOCK
    nq = m // tq
    nqb = seq // tq
    return pl.pallas_call(
        _dsa_prep_kernel,
        out_shape=(
            jax.ShapeDtypeStruct((m, KV_LORA_RANK), BF16),
            jax.ShapeDtypeStruct((m // seq, KV_LORA_RANK, seq), BF16),
            jax.ShapeDtypeStruct((m, IDX_K), BF16),
            jax.ShapeDtypeStruct((m // Q_BLOCK * N_IDX_HEADS, Q_BLOCK, IDX_K), BF16),
            jax.ShapeDtypeStruct((m // Q_BLOCK * N_IDX_HEADS, Q_BLOCK), F32),
        ),
        grid=(nq,),
        in_specs=[
            pl.BlockSpec((tq, KV_LORA_RANK), lambda i: (i, COL_KV // KV_LORA_RANK)),
            pl.BlockSpec((tq, IDX_WIDTH), lambda i: (i, COL_QIDX // IDX_WIDTH)),
            pl.BlockSpec((tq, LANE), lambda i: (i, COL_SMALL // LANE)),
            pl.BlockSpec((1, KV_LORA_RANK), lambda i: (0, 0)),
            pl.BlockSpec((1, IDX_DIM), lambda i: (0, 0)),
            pl.BlockSpec((1, IDX_DIM), lambda i: (0, 0)),
        ],
        out_specs=(
            pl.BlockSpec((tq, KV_LORA_RANK), lambda i: (i, 0)),
            pl.BlockSpec((1, KV_LORA_RANK, tq), lambda i: (i // nqb, 0, i % nqb)),
            pl.BlockSpec((tq, IDX_K), lambda i: (i, 0)),
            pl.BlockSpec((PREP_QBLOCKS * N_IDX_HEADS, Q_BLOCK, IDX_K), lambda i: (i, 0, 0)),
            pl.BlockSpec((PREP_QBLOCKS * N_IDX_HEADS, Q_BLOCK), lambda i: (i, 0)),
        ),
        compiler_params=_cparams(("parallel",)),
        name="dsa_prep",
    )(proj, proj, proj, kv_g, kn_g, kn_b)


def _bucket_thresholds():
    max_exact = N_BUCKETS // 2
    rel = np.arange(max_exact, 4 * MAX_DISTANCE, dtype=np.float32)
    large = max_exact + (np.log(rel / np.float32(max_exact)) / np.float32(math.log(MAX_DISTANCE / max_exact))
                         * np.float32(N_BUCKETS - max_exact)).astype(np.int32)
    large = np.minimum(large, N_BUCKETS - 1)
    return [int(rel[np.argmax(large >= bkt)]) for bkt in range(max_exact + 1, N_BUCKETS)]


def _bias_table_kernel(rb_ref, o_ref):
    di = pl.program_id(0)
    max_exact = N_BUCKETS // 2
    sl = lax.broadcasted_iota(I32, (KEY_GROUP, Q_BLOCK), 0)
    tl = lax.broadcasted_iota(I32, (KEY_GROUP, Q_BLOCK), 1)
    rel = jnp.maximum(di * Q_BLOCK + tl - sl, 0)
    large = jnp.full(rel.shape, max_exact, I32)
    for th in _bucket_thresholds():
        large = large + (rel >= th).astype(I32)
    bucket = jnp.where(rel < max_exact, rel, large)
    for h in range(N_ATT_HEADS):
        far = rb_ref[N_BUCKETS - 1, h]
        val = jnp.zeros(rel.shape, F32)
        for bkt in range(N_BUCKETS - 1):
            val = jnp.where(bucket == bkt, rb_ref[bkt, h] - far, val)
        o_ref[0, :, h * Q_BLOCK:(h + 1) * Q_BLOCK] = val


def bias_tables(rel_bias):
    nh = rel_bias.shape[1]
    return pl.pallas_call(
        _bias_table_kernel,
        out_shape=jax.ShapeDtypeStruct((N_BIAS_TABLES, KEY_GROUP, nh * Q_BLOCK), F32),
        grid=(N_BIAS_TABLES,),
        in_specs=[pl.BlockSpec(memory_space=pltpu.SMEM)],
        out_specs=pl.BlockSpec((1, KEY_GROUP, nh * Q_BLOCK), lambda d: (d, 0, 0)),
        compiler_params=_cparams(("arbitrary",)),
        name="bias_tables",
    )(rel_bias)


def _dsa_kernel(q_ref, ckv_ref, ckvt_ref, k3_ref, qi3_ref, wt_ref, wuk_ref, wuv_ref, bias_ref, o_ref,
                keys_ref, qabs_ref, acc_ref, *, topk):
    j = pl.program_id(1)
    nh = N_ATT_HEADS
    tq = Q_BLOCK
    t_lane = j * tq + lax.broadcasted_iota(I32, (1, tq), 1)

    n_sgrp = (j + 1 + (SCORE_GROUP // tq - 1)) // (SCORE_GROUP // tq)
    qi3 = qi3_ref[...].reshape(N_IDX_HEADS * tq, IDX_K)

    def score_body(g, carry):
        r0 = pl.multiple_of(g * SCORE_GROUP, SCORE_GROUP)
        lt = _dot_nt(k3_ref[pl.ds(r0, SCORE_GROUP), :], qi3)
        sc = jnp.zeros((SCORE_GROUP, tq), F32)
        for h in range(N_IDX_HEADS):
            sc = sc + jnp.maximum(lt[:, h * tq:(h + 1) * tq], 0.0) * wt_ref[h:h + 1, :]
        s_row = r0 + lax.broadcasted_iota(I32, (SCORE_GROUP, 1), 0)
        sc = jnp.where(s_row <= t_lane, sc + 0.0, -jnp.inf)
        bits = pltpu.bitcast(sc, I32)
        keys_ref[pl.ds(r0, SCORE_GROUP), :] = jnp.where(bits < 0, bits ^ 0x7FFFFFFF, bits)
        return carry

    lax.fori_loop(0, n_sgrp, score_body, 0)

    def count_ge(cand):
        def body(g, part):
            r0 = pl.multiple_of(g * SCORE_GROUP, SCORE_GROUP)
            ind = jnp.where(keys_ref[pl.ds(r0, SCORE_GROUP), :] >= cand, 1.0, 0.0)
            return part + jnp.sum(ind.reshape(SCORE_GROUP // COUNT_ROWS, COUNT_ROWS, tq), axis=0)
        part = lax.fori_loop(0, n_sgrp, body, jnp.zeros((COUNT_ROWS, tq), F32))
        return jnp.sum(part, axis=0, keepdims=True)

    thr0 = jnp.where(count_ge(jnp.zeros((1, tq), I32)) >= topk, 0, INT_MIN).astype(I32)

    def bit_body(i, thr):
        cand = thr | jnp.left_shift(jnp.int32(1), 30 - i)
        return jnp.where(count_ge(cand) >= topk, cand, thr)

    thr = lax.fori_loop(0, 31, bit_body, thr0)

    n_ge = count_ge(thr)
    tie_lane = (n_ge > topk) & (thr > KEY_NEG_INF)
    has_tie = jnp.max(jnp.where(tie_lane, 1.0, 0.0)) > 0.0

    @pl.when(has_tie)
    def _():
        need = topk - count_ge(thr + 1)
        r_i = lax.broadcasted_iota(I32, (tq, tq), 0)
        c_i = lax.broadcasted_iota(I32, (tq, tq), 1)
        tri = jnp.where(c_i <= r_i, 1.0, 0.0).astype(BF16)

        def tie_body(kb, seen):
            r0 = pl.multiple_of(kb * tq, tq)
            blk = keys_ref[pl.ds(r0, tq), :]
            s_row = r0 + lax.broadcasted_iota(I32, (tq, 1), 0)
            eq = (blk == thr) & (s_row <= t_lane) & tie_lane
            rank = seen + _dot(tri, jnp.where(eq, 1.0, 0.0).astype(BF16))
            keys_ref[pl.ds(r0, tq), :] = jnp.where(eq & (rank > need), thr - 1, blk)
            return rank[tq - 1:tq, :]

        lax.fori_loop(0, j + 1, tie_body, jnp.zeros((1, tq), F32))

    for h in range(nh):
        qh = q_ref[:, h * HEAD_DIM:(h + 1) * HEAD_DIM].astype(BF16)
        qabs_ref[:, h * tq:(h + 1) * tq] = (_dot_nt(wuk_ref[h], qh) * ATT_SCALE).astype(BF16)

    acc_ref[...] = jnp.zeros(acc_ref.shape, F32)

    def attend(c0, carry, bias_di):
        kg = KEY_GROUP
        m, l = carry
        s_all = _dot(ckv_ref[pl.ds(c0, kg), :], qabs_ref[...])
        s_row = c0 + lax.broadcasted_iota(I32, (kg, 1), 0)
        sel = (keys_ref[pl.ds(c0, kg), :] >= thr) & (s_row <= t_lane)
        addm = jnp.where(sel, 0.0, NEG_BIG)
        ps, ms, ls, corrs = [], [], [], []
        for h in range(nh):
            lanes = slice(h * tq, (h + 1) * tq)
            s = s_all[:, lanes] + addm
            if bias_di is not None:
                s = s + bias_ref[bias_di, :, lanes]
            m_prev = m[:, lanes]
            m_new = jnp.maximum(m_prev, jnp.max(s, axis=0, keepdims=True))
            p = jnp.exp(s - m_new)
            corr = jnp.exp(m_prev - m_new)
            ls.append(corr * l[:, lanes] + jnp.sum(p, axis=0, keepdims=True))
            ms.append(m_new)
            corrs.append(corr)
            ps.append(p.astype(BF16))
        pt = jnp.concatenate(ps, axis=1)
        corr_all = jnp.concatenate(corrs, axis=1)
        acc_ref[...] = acc_ref[...] * corr_all + _dot(ckvt_ref[0, :, pl.ds(c0, kg)], pt)
        return jnp.concatenate(ms, axis=1), jnp.concatenate(ls, axis=1)

    def at_group(g):
        return pl.multiple_of(g * KEY_GROUP, KEY_GROUP)

    carry = (jnp.full((1, nh * tq), NEG_BIG, F32), jnp.zeros((1, nh * tq), F32))
    n_far = jnp.maximum(j // 2 - 1 + j % 2, 0)
    carry = lax.fori_loop(0, n_far, lambda g, cr: attend(at_group(g), cr, None), carry)
    carry = lax.cond((j % 2 == 0) & (j >= 2), lambda cr: attend(at_group(j // 2 - 1), cr, 2), lambda cr: cr, carry)
    _, l = attend(at_group(j // 2), carry, j % 2)

    o_lat = (acc_ref[...] / l).astype(BF16)
    for h in range(nh):
        o_ref[:, h * HEAD_DIM:(h + 1) * HEAD_DIM] = _dot_tn(
            o_lat[:, h * tq:(h + 1) * tq], wuv_ref[h]).astype(o_ref.dtype)


def dsa_attention(proj, ckv, ckvt, k3, qi3, wt, wuk, wuv, bias, *, batch, seq, topk):
    tq = Q_BLOCK
    nq = seq // tq
    nh = N_ATT_HEADS
    r = KV_LORA_RANK
    seq_pad = -(-seq // SCORE_GROUP) * SCORE_GROUP
    return pl.pallas_call(
        functools.partial(_dsa_kernel, topk=topk),
        out_shape=jax.ShapeDtypeStruct((batch * seq, ATT_WIDTH), BF16),
        grid=(batch, nq),
        in_specs=[
            pl.BlockSpec((tq, ATT_WIDTH), lambda b, j: (b * nq + j, 0)),
            _single((seq, r), lambda b, j: (b, 0)),
            _single((1, r, seq), lambda b, j: (b, 0, 0)),
            _single((seq, IDX_K), lambda b, j: (b, 0)),
            pl.BlockSpec((N_IDX_HEADS, tq, IDX_K), lambda b, j: (b * nq + j, 0, 0)),
            pl.BlockSpec((N_IDX_HEADS, tq), lambda b, j: (b * nq + j, 0)),
            _single((nh, r, HEAD_DIM), lambda b, j: (0, 0, 0)),
            _single((nh, r, HEAD_DIM), lambda b, j: (0, 0, 0)),
            _single((N_BIAS_TABLES, KEY_GROUP, nh * tq), lambda b, j: (0, 0, 0)),
        ],
        out_specs=pl.BlockSpec((tq, ATT_WIDTH), lambda b, j: (b * nq + j, 0)),
        scratch_shapes=[
            pltpu.VMEM((seq_pad, tq), I32),
            pltpu.VMEM((r, nh * tq), BF16),
            pltpu.VMEM((r, nh * tq), F32),
        ],
        compiler_params=_cparams(("parallel", "arbitrary")),
        name="dsa",
    )(proj, ckv, ckvt, k3, qi3, wt, wuk, wuv, bias)


def _cumsum_rows(x):
    n = x.shape[0]
    row = lax.broadcasted_iota(I32, (n, 1), 0)
    sh = 1
    while sh < n:
        x = x + jnp.where(row >= sh, pltpu.roll(x, sh, 0), 0.0)
        sh *= 2
    return x


def _hgrn2_kernel(q_ref, f_ref, i_ref, g_ref, lb_ref, ng_ref, o_ref, st_ref, *, layer):
    c = REC_CHUNK
    dk = HEAD_DIM

    @pl.when(pl.program_id(2) == 0)
    def _():
        st_ref[...] = jnp.zeros_like(st_ref)

    lbr = lb_ref[...]
    e = jnp.exp(lbr - jnp.max(lbr, axis=0, keepdims=True))
    pr = e / jnp.sum(e, axis=0, keepdims=True)
    lb_all = jnp.zeros((1, lbr.shape[1]), F32)
    for i in range(1, layer + 1):
        lb_all = lb_all + pr[i:i + 1, :]

    row = lax.broadcasted_iota(I32, (c, 1), 0)
    col = lax.broadcasted_iota(I32, (1, c), 1)
    sub = 8
    nsub = c // sub
    tl3 = lax.broadcasted_iota(I32, (1, sub, 1), 1)
    g3 = lax.broadcasted_iota(I32, (nsub, 1, 1), 0)
    col3 = lax.broadcasted_iota(I32, (1, 1, c), 2)
    levels = []
    m = c // 2
    while m >= sub:
        levels.append(m)
        m //= 2

    def head_chunk(r0, hh):
        lanes = slice(hh * dk, (hh + 1) * dk)
        lb = lb_all[:, lanes]
        q = q_ref[pl.ds(r0, c), lanes]
        qf = q * jax.nn.sigmoid(q)
        f = lb + (1.0 - lb) * jax.nn.sigmoid(f_ref[pl.ds(r0, c), lanes])
        k = 1.0 - f
        v = i_ref[pl.ds(r0, c), lanes].astype(BF16)
        cum = _cumsum_rows(jnp.log(f))

        a = jnp.zeros((c, c), F32)
        for m in levels:
            first = (row & (2 * m - 1)) < m
            bnd = jnp.concatenate(
                [jnp.broadcast_to(cum[b * 2 * m + m - 1:b * 2 * m + m, :], (2 * m, dk))
                 for b in range(c // (2 * m))], axis=0)
            qs = jnp.where(first, 0.0, qf * jnp.exp(cum - bnd)).astype(BF16)
            ks = jnp.where(first, k * jnp.exp(bnd - cum), 0.0).astype(BF16)
            same = (row & -(2 * m)) == (col & -(2 * m))
            a = a + jnp.where(same, _dot_nt(qs, ks), 0.0)
        cum3 = cum.reshape(nsub, sub, dk)
        k3 = k.reshape(nsub, sub, dk)
        qf3 = qf.reshape(nsub, sub, dk)
        a3 = jnp.zeros((nsub, sub, c), F32)
        for s in range(sub):
            d = jnp.where(tl3 >= s, cum3 - cum3[:, s:s + 1, :], -jnp.inf)
            prod = qf3 * k3[:, s:s + 1, :] * jnp.exp(d)
            colsum = jnp.sum(prod, axis=-1, keepdims=True)
            a3 = a3 + jnp.where(col3 == g3 * sub + s, colsum, 0.0)
        a = a + a3.reshape(c, c)

        st = st_ref[hh]
        o = _dot(a.astype(BF16), v) + _dot_nt((qf * jnp.exp(cum)).astype(BF16), st.astype(BF16))
        last = cum[c - 1:c, :]
        kl = (k * jnp.exp(last - cum)).astype(BF16)
        st_ref[hh] = st * jnp.exp(last) + _dot_tn(v, kl)

        ms = jnp.mean(o * o, axis=-1, keepdims=True)
        gt = g_ref[pl.ds(r0, c), lanes]
        on = o * lax.rsqrt(ms + LN_EPS) * ng_ref[:, lanes]
        o_ref[pl.ds(r0, c), lanes] = (on * (gt * jax.nn.sigmoid(gt))).astype(o_ref.dtype)

    def chunk_body(ci, carry):
        r0 = pl.multiple_of(ci * c, c)
        for hh in range(REC_HEADS_PER_STEP):
            head_chunk(r0, hh)
        return carry

    lax.fori_loop(0, q_ref.shape[0] // c, chunk_body, 0)


def hgrn2(proj, rec_lb, norm_g, *, layer, batch, seq):
    tb = min(REC_TBLOCK, seq)
    nt = seq // tb
    hw = REC_HEADS_PER_STEP * HEAD_DIM
    ng = N_REC_HEADS // REC_HEADS_PER_STEP

    def seg(k):
        return pl.BlockSpec((tb, hw), lambda b, h, t, k=k: (b * nt + t, k * ng + h))

    return pl.pallas_call(
        functools.partial(_hgrn2_kernel, layer=layer),
        out_shape=jax.ShapeDtypeStruct((batch * seq, REC_WIDTH), BF16),
        grid=(batch, ng, nt),
        in_specs=[
            seg(0), seg(1), seg(2), seg(3),
            pl.BlockSpec((rec_lb.shape[0], hw), lambda b, h, t: (0, h)),
            pl.BlockSpec((1, hw), lambda b, h, t: (0, h)),
        ],
        out_specs=pl.BlockSpec((tb, hw), lambda b, h, t: (b * nt + t, h)),
        scratch_shapes=[pltpu.VMEM((REC_HEADS_PER_STEP, HEAD_DIM, HEAD_DIM), F32)],
        compiler_params=_cparams(("parallel", "parallel", "arbitrary")),
        name="hgrn2",
    )(proj, proj, proj, proj, rec_lb, norm_g)


def _pack_w_in(w):
    kv0 = ATT_WIDTH
    qi0 = kv0 + KV_LORA_RANK
    sm0 = qi0 + IDX_WIDTH
    sm1 = sm0 + IDX_DIM + N_IDX_HEADS
    pad = jnp.zeros((w.shape[0], SMALL_WIDTH - IDX_DIM - N_IDX_HEADS), BF16)
    parts = [w[:, :kv0], w[:, qi0:sm0], w[:, kv0:qi0], w[:, sm0:sm1]]
    w_att = jnp.concatenate([p.astype(BF16) for p in parts] + [pad], axis=1)
    return w_att, w[:, sm1:].astype(BF16)


def _row_tile(m, want):
    t = min(want, m)
    while m % t:
        t //= 2
    return t


def kernel(x, mem, w_in, w_uk, w_uv, kv_norm_g, idx_kn_g, idx_kn_b, rec_lb, rec_norm_g, w_out, rel_bias,
           wq_c, wk_c, wv_c, wo_c, ffn_gate, ffn_up, ffn_down, ln_g, ln_b):
    batch, seq, d = x.shape
    depth = w_in.shape[0]
    m = batch * seq
    alpha = (2 * depth) ** 0.25
    topk = min(TOPK_MAX, seq // 4)
    d_ff = ffn_gate.shape[-1]
    tf = 256 if d_ff % 256 == 0 else 128

    xf = x.reshape(m, d)
    mem_f = mem.reshape(batch * mem.shape[1], d)
    bias = bias_tables(rel_bias)
    rec_lb = rec_lb.astype(F32)

    ffn_order = [(l, i) for l in range(depth) for i in range(2)]
    ffn_w = {ffn_order[0]: tuple(w[0, 0].astype(BF16) for w in (ffn_gate, ffn_up, ffn_down))}

    def ffn(xf, l, i):
        pos = ffn_order.index((l, i))
        nxt = ffn_order[pos + 1] if pos + 1 < len(ffn_order) else None
        out, cast = ffn_ln(xf, *ffn_w.pop((l, i)), ln_g[l, 3 * i:3 * i + 1], ln_b[l, 3 * i:3 * i + 1],
                           alpha=alpha, tm=_row_tile(m, 512), tf=tf,
                           cast_next=None if nxt is None else (ffn_gate, ffn_up, ffn_down) + nxt)
        if nxt is not None:
            ffn_w[nxt] = cast
        return out

    for l in range(depth):
        xf = ffn(xf, l, 0)

        w_att, w_rec = _pack_w_in(w_in[l])
        proj_a = matmul(xf, w_att, tm=_row_tile(m, 512), tn=1280, out_dtype=F32)
        proj_r = matmul(xf, w_rec, tm=_row_tile(m, 512), tn=1024, out_dtype=F32)
        ckv, ckvt, k3, qi3, wt = dsa_prep(proj_a, kv_norm_g[l][None], idx_kn_g[l][None], idx_kn_b[l][None], seq=seq)
        o_att = dsa_attention(proj_a, ckv, ckvt, k3, qi3, wt, w_uk[l].astype(BF16), w_uv[l].astype(BF16), bias,
                              batch=batch, seq=seq, topk=topk)
        o_rec = hgrn2(proj_r, rec_lb, rec_norm_g[l][None], layer=l, batch=batch, seq=seq)
        xf = proj_ln(o_att, o_rec, w_out[l].astype(BF16), xf, ln_g[l, 1:2], ln_b[l, 1:2],
                     alpha=alpha, tm=_row_tile(m, 512), tn=512)

        kv = matmul(mem_f, jnp.concatenate([wk_c[l], wv_c[l]], axis=1).astype(BF16),
                    tm=_row_tile(mem_f.shape[0], 512), tn=256, out_dtype=BF16)
        kv = kv.reshape(batch, mem.shape[1], kv.shape[-1])
        xf = cross_ln(xf, wq_c[l].astype(BF16), kv, wo_c[l].astype(BF16), ln_g[l, 2:3], ln_b[l, 2:3],
                      alpha=alpha, tm=_row_tile(seq, 256), rows_per_batch=seq)

        xf = ffn(xf, l, 1)
    return xf.reshape(batch, seq, d)
```
